```python
import math
import jax
import jax.numpy as jnp
from jax import lax
import numpy as np

D_MODEL = 1024
BATCH = 4
SEQ = 4096
DEPTH = 4
DEC_BATCH = 128
DEC_SEQ = 1
PAST_LEN = 2048
PAGE_SIZE = 128

N_MIXERS = 3
N_A = (DEPTH + 2) // 3
N_B = (DEPTH + 1) // 3
N_C = DEPTH // 3

EPS = 1e-6
CONV_W = 4
PLE_DIM = 256

LRU_W = D_MODEL
LRU_BLOCKS = 16
LRU_BW = LRU_W // LRU_BLOCKS
LRU_C = 8.0

SSM_DI = 2 * D_MODEL
SSM_P = 64
SSM_H = SSM_DI // SSM_P
SSM_G = 8
SSM_N = 128
SSM_CONV_DIM = SSM_DI + 2 * SSM_G * SSM_N
SSM_IN = SSM_DI + SSM_CONV_DIM + SSM_H
SSM_CHUNK = 128

ATT_HD = 64
ATT_H = D_MODEL // ATT_HD
ATT_IN = 4 * D_MODEL + ATT_H
ATT_QB = 128
ATT_SCALE = ATT_HD ** -0.5

MOE_GROUPS = 4
MOE_EPG = 4
MOE_E = MOE_GROUPS * MOE_EPG
MOE_TOPK = 2
MOE_F = D_MODEL // 2

kernel_name = 'hybrid_lru_ssd_fox_hmoe_step'


def rms_norm(x, g):
    xf = x.astype(jnp.float32)
    y = xf * lax.rsqrt(jnp.mean(xf * xf, axis=-1, keepdims=True) + EPS)
    return (y * g.astype(jnp.float32)).astype(x.dtype)


def causal_dwconv(x, buf, w, b):
    L = x.shape[1]
    xp = jnp.concatenate([buf, x], axis=1)
    out = b + sum(xp[:, k:k + L] * w[k] for k in range(CONV_W))
    return out, xp[:, -(CONV_W - 1):]


def rglru_mixer(xn, conv_buf, h0, w_in, conv_w, conv_b, w_a, b_a, w_i, b_i, lam, w_out):
    f32 = jnp.float32
    bsz, L, _ = xn.shape
    xb, yb = jnp.split(xn @ w_in, 2, axis=-1)
    yb = jax.nn.gelu(yb)
    xc, new_buf = causal_dwconv(xb, conv_buf, conv_w, conv_b)
    xblk = xc.reshape(bsz, L, LRU_BLOCKS, LRU_BW)
    r = jax.nn.sigmoid(jnp.einsum('blhi,hij->blhj', xblk, w_a) + b_a).reshape(bsz, L, LRU_W)
    ig = jax.nn.sigmoid(jnp.einsum('blhi,hij->blhj', xblk, w_i) + b_i).reshape(bsz, L, LRU_W)
    log_a = -LRU_C * r.astype(f32) * jax.nn.softplus(-lam.astype(f32))
    a = jnp.exp(log_a)
    bterm = jnp.sqrt(-jnp.expm1(2.0 * log_a)) * (ig * xc).astype(f32)
    bterm = bterm.at[:, 0].add(a[:, 0] * h0.astype(f32))

    def combine(lhs, rhs):
        a1, b1 = lhs
        a2, b2 = rhs
        return a1 * a2, a2 * b1 + b2

    _, h = lax.associative_scan(combine, (a, bterm), axis=1)
    y = (h.astype(xn.dtype) * yb) @ w_out
    return y, new_buf, h[:, -1].astype(xn.dtype)


def ssd_scan(x, dt, A, Bm, Cm, h0):
    b, L = x.shape[:2]
    Q = min(SSM_CHUNK, L)
    Lp = -(-L // Q) * Q
    pad = Lp - L

    def pad_len(t):
        return jnp.pad(t, [(0, 0), (0, pad)] + [(0, 0)] * (t.ndim - 2))

    x, dt, Bm, Cm = pad_len(x), pad_len(dt), pad_len(Bm), pad_len(Cm)
    nc = Lp // Q
    R = SSM_H // SSM_G
    xr = x.reshape(b, nc, Q, SSM_G, R, SSM_P)
    dtr = dt.reshape(b, nc, Q, SSM_G, R)
    Br = Bm.reshape(b, nc, Q, SSM_G, SSM_N)
    Cr = Cm.reshape(b, nc, Q, SSM_G, SSM_N)
    acum = jnp.cumsum(dtr * A.reshape(SSM_G, R), axis=2)
    xdt = xr * dtr[..., None]
    tri = jnp.tril(jnp.ones((Q, Q), dtype=bool))[None, None, :, :, None, None]
    seg = acum[:, :, :, None] - acum[:, :, None, :]
    decay_ts = jnp.exp(jnp.where(tri, seg, -jnp.inf))
    cb = jnp.einsum('bctgn,bcsgn->bctsg', Cr, Br)
    y_diag = jnp.einsum('bctsg,bctsgr,bcsgrp->bctgrp', cb, decay_ts, xdt)
    decay_end = jnp.exp(acum[:, :, -1:] - acum)
    chunk_states = jnp.einsum('bcsgn,bcsgr,bcsgrp->bcgrpn', Br, decay_end, xdt)
    chunk_decay = jnp.exp(acum[:, :, -1])

    def step(h, inp):
        st, dec = inp
        return h * dec[..., None, None] + st, h

    h_last, h_in = lax.scan(step, h0.reshape(b, SSM_G, R, SSM_P, SSM_N),
                            (jnp.moveaxis(chunk_states, 1, 0), jnp.moveaxis(chunk_decay, 1, 0)))
    h_in = jnp.moveaxis(h_in, 0, 1)
    y_off = jnp.einsum('bctgn,bcgrpn,bctgr->bctgrp', Cr, h_in, jnp.exp(acum))
    y = (y_diag + y_off).reshape(b, Lp, SSM_H, SSM_P)[:, :L]
    return y, h_last.reshape(b, SSM_H, SSM_P, SSM_N)


def mamba2_mixer(xn, conv_buf, h0, w_in, conv_w, conv_b, dt_bias, a_log, d_skip, norm_g, w_out):
    f32 = jnp.float32
    b, L, _ = xn.shape
    z, xbc, dt = jnp.split(xn @ w_in, [SSM_DI, SSM_DI + SSM_CONV_DIM], axis=-1)
    xbc, new_buf = causal_dwconv(xbc, conv_buf, conv_w, conv_b)
    xbc = jax.nn.silu(xbc)
    xs, Bm, Cm = jnp.split(xbc, [SSM_DI, SSM_DI + SSM_G * SSM_N], axis=-1)
    dt = jax.nn.softplus(dt.astype(f32) + dt_bias.astype(f32))
    A = -jnp.exp(a_log.astype(f32))
    xh = xs.reshape(b, L, SSM_H, SSM_P).astype(f32)
    y, h_last = ssd_scan(xh, dt, A,
                         Bm.reshape(b, L, SSM_G, SSM_N).astype(f32),
                         Cm.reshape(b, L, SSM_G, SSM_N).astype(f32),
                         h0.astype(f32))
    y = y + xh * d_skip.astype(f32)[:, None]
    y = y.reshape(b, L, SSM_DI) * jax.nn.silu(z.astype(f32))
    yg = y.reshape(b, L, SSM_G, SSM_DI // SSM_G)
    yg = yg * lax.rsqrt(jnp.mean(yg * yg, axis=-1, keepdims=True) + EPS)
    y = (yg.reshape(b, L, SSM_DI) * norm_g.astype(f32)).astype(xn.dtype)
    return y @ w_out, new_buf, h_last.astype(xn.dtype)


def fox_attend(q, k, v, cq, ck, qpos, kpos):
    ckT = jnp.swapaxes(ck, 1, 2)[:, :, None, :]

    def block(args):
        qb, cqb, qpb = args
        s = jnp.einsum('bqhd,bkhd->bhqk', qb, k).astype(jnp.float32) * ATT_SCALE
        s = s + jnp.swapaxes(cqb, 1, 2)[..., None] - ckT
        s = jnp.where(kpos[None, :] <= qpb[:, None], s, -jnp.inf)
        pr = jax.nn.softmax(s, axis=-1).astype(v.dtype)
        return jnp.einsum('bhqk,bkhd->bqhd', pr, v)

    b, Lq = q.shape[:2]
    if Lq > ATT_QB and Lq % ATT_QB == 0:
        nb = Lq // ATT_QB
        qs = jnp.moveaxis(q.reshape(b, nb, ATT_QB, ATT_H, ATT_HD), 1, 0)
        cqs = jnp.moveaxis(cq.reshape(b, nb, ATT_QB, ATT_H), 1, 0)
        o = lax.map(block, (qs, cqs, qpos.reshape(nb, ATT_QB)))
        return jnp.moveaxis(o, 0, 1).reshape(b, Lq, ATT_H, ATT_HD)
    return block((q, cq, qpos))


def fox_mixer(xn, past, w_in, b_f, q_g, k_g, w_out):
    f32 = jnp.float32
    b, L, _ = xn.shape
    q, k, v, g, fl = jnp.split(xn @ w_in, [D_MODEL, 2 * D_MODEL, 3 * D_MODEL, 4 * D_MODEL], axis=-1)
    q = rms_norm(q.reshape(b, L, ATT_H, ATT_HD), q_g)
    k = rms_norm(k.reshape(b, L, ATT_H, ATT_HD), k_g)
    v = v.reshape(b, L, ATT_H, ATT_HD)
    logf = jax.nn.log_sigmoid(fl.astype(f32) + b_f.astype(f32))
    c_new = jnp.cumsum(logf, axis=1)
    if past is None:
        pos = jnp.arange(L)
        o = fox_attend(q, k, v, c_new, c_new, pos, pos)
    else:
        k_p, v_p, logf_p = past
        P = k_p.shape[1]
        c_p = jnp.cumsum(logf_p.astype(f32), axis=1)
        c_q = c_p[:, -1:] + c_new
        o = fox_attend(q, jnp.concatenate([k_p, k], axis=1), jnp.concatenate([v_p, v], axis=1),
                       c_q, jnp.concatenate([c_p, c_q], axis=1),
                       P + jnp.arange(L), jnp.arange(P + L))
    y = (o.reshape(b, L, D_MODEL) * jax.nn.sigmoid(g)) @ w_out
    return y, k, v, logf.astype(xn.dtype)


def hier_moe(xn, w_group, b_group, w_expert, b_expert, w_gate, w_up, w_down):
    f32 = jnp.float32
    shp = xn.shape
    t = xn.reshape(-1, D_MODEL)
    glog = (t @ w_group).astype(f32) + b_group.astype(f32)
    gsel = jnp.argmax(glog, axis=-1)
    gprob = jnp.take_along_axis(jax.nn.softmax(glog, axis=-1), gsel[:, None], axis=-1)
    elog = ((t @ w_expert).astype(f32) + b_expert.astype(f32)).reshape(-1, MOE_GROUPS, MOE_EPG)
    within = jnp.take_along_axis(elog, gsel[:, None, None], axis=1)[:, 0]
    top_v, top_i = lax.top_k(within, MOE_TOPK)
    wts = jax.nn.softmax(top_v, axis=-1) * gprob
    eid = gsel[:, None] * MOE_EPG + top_i
    comb = jnp.sum(jax.nn.one_hot(eid, MOE_E, dtype=f32) * wts[..., None], axis=1)
    hg = jnp.einsum('td,edf->tef', t, w_gate)
    hu = jnp.einsum('td,edf->tef', t, w_up)
    hid = jax.nn.silu(hg) * hu * comb[..., None].astype(t.dtype)
    return jnp.einsum('tef,efd->td', hid, w_down).reshape(shp)


def trunk(x, p, lru_h, lru_conv, ssm_h, ssm_conv, fox_past, prm):
    h = x
    o_lru_h, o_lru_conv, o_ssm_h, o_ssm_conv, o_k, o_v, o_lf = [], [], [], [], [], [], []
    for i in range(DEPTH):
        j = i // N_MIXERS
        xn = rms_norm(h, prm['norm_mix'][i])
        if i % N_MIXERS == 0:
            y, cbuf, hl = rglru_mixer(xn, lru_conv[j], lru_h[j], prm['lru_w_in'][j], prm['lru_conv_w'][j],
                                      prm['lru_conv_b'][j], prm['lru_w_a'][j], prm['lru_b_a'][j],
                                      prm['lru_w_i'][j], prm['lru_b_i'][j], prm['lru_lambda'][j],
                                      prm['lru_w_out'][j])
            o_lru_conv.append(cbuf)
            o_lru_h.append(hl)
        elif i % N_MIXERS == 1:
            y, cbuf, hl = mamba2_mixer(xn, ssm_conv[j], ssm_h[j], prm['ssm_w_in'][j], prm['ssm_conv_w'][j],
                                       prm['ssm_conv_b'][j], prm['ssm_dt_bias'][j], prm['ssm_a_log'][j],
                                       prm['ssm_d'][j], prm['ssm_norm'][j], prm['ssm_w_out'][j])
            o_ssm_conv.append(cbuf)
            o_ssm_h.append(hl)
        else:
            y, k, v, lf = fox_mixer(xn, fox_past(j), prm['fox_w_in'][j], prm['fox_b_f'][j],
                                    prm['fox_q_norm'][j], prm['fox_k_norm'][j], prm['fox_w_out'][j])
            o_k.append(k)
            o_v.append(v)
            o_lf.append(lf)
        h = h + y
        h = h + hier_moe(rms_norm(h, prm['norm_ffn'][i]), prm['moe_w_group'][i], prm['moe_b_group'][i],
                         prm['moe_w_expert'][i], prm['moe_b_expert'][i], prm['moe_w_gate'][i],
                         prm['moe_w_up'][i], prm['moe_w_down'][i])
        gate = jax.nn.sigmoid(rms_norm(h, prm['norm_ple'][i]) @ prm['ple_w_gate'][i])
        h = h + gate * (p[i] @ prm['ple_w_proj'][i])
    states = (jnp.stack(o_lru_h), jnp.stack(o_lru_conv), jnp.stack(o_ssm_h), jnp.stack(o_ssm_conv),
              jnp.stack(o_k), jnp.stack(o_v), jnp.stack(o_lf))
    return rms_norm(h, prm['norm_final']), states


def setup_inputs(seed: int = 0) -> dict:
    key = jax.random.key(seed)
    keys = iter(jax.random.split(key, 64))
    f32 = jnp.float32

    def normal(shape, scale):
        return jax.random.normal(next(keys), shape, f32) * scale

    def gain(shape):
        return 1.0 + 0.05 * jax.random.normal(next(keys), shape, f32)

    def uniform(shape, lo, hi):
        return jax.random.uniform(next(keys), shape, f32, lo, hi)

    n_pages = PAST_LEN // PAGE_SIZE
    n_used = DEC_BATCH * n_pages
    n_phys = n_used + n_used // 4

    x_prompt = normal((BATCH, SEQ, D_MODEL), 1.0)
    x_sample = normal((DEC_BATCH, DEC_SEQ, D_MODEL), 1.0)
    state_lru_h = normal((N_A, DEC_BATCH, LRU_W), 0.5)
    state_lru_conv = normal((N_A, DEC_BATCH, CONV_W - 1, LRU_W), 0.5)
    state_ssm_h = normal((N_B, DEC_BATCH, SSM_H, SSM_P, SSM_N), 0.1)
    state_ssm_conv = normal((N_B, DEC_BATCH, CONV_W - 1, SSM_CONV_DIM), 0.5)
    cache_k = normal((N_C, n_phys, PAGE_SIZE, ATT_H, ATT_HD), 1.0)
    cache_v = normal((N_C, n_phys, PAGE_SIZE, ATT_H, ATT_HD), 1.0)
    cache_logf = jax.nn.log_sigmoid(3.0 + normal((N_C, n_phys, PAGE_SIZE, ATT_H), 1.0))
    page_table = jax.random.permutation(next(keys), n_phys)[:n_used].reshape(DEC_BATCH, n_pages).astype(jnp.int32)
    p_prompt = normal((DEPTH, BATCH, SEQ, PLE_DIM), 1.0)
    p_sample = normal((DEPTH, DEC_BATCH, DEC_SEQ, PLE_DIM), 1.0)

    a8 = uniform((N_A, LRU_W), 0.9, 0.999)
    a_base = a8 ** (1.0 / LRU_C)
    lru_lambda = jnp.log(a_base) - jnp.log1p(-a_base)
    dt0 = jnp.exp(uniform((N_B, SSM_H), math.log(1e-3), math.log(1e-1)))
    ssm_dt_bias = dt0 + jnp.log(-jnp.expm1(-dt0))

    return {
        'x_prompt': x_prompt,
        'x_sample': x_sample,
        'state_lru_h': state_lru_h,
        'state_lru_conv': state_lru_conv,
        'state_ssm_h': state_ssm_h,
        'state_ssm_conv': state_ssm_conv,
        'cache_k': cache_k,
        'cache_v': cache_v,
        'cache_logf': cache_logf,
        'page_table': page_table,
        'p_prompt': p_prompt,
        'p_sample': p_sample,
        'lru_w_in': normal((N_A, D_MODEL, 2 * LRU_W), D_MODEL ** -0.5),
        'lru_conv_w': normal((N_A, CONV_W, LRU_W), CONV_W ** -0.5),
        'lru_conv_b': normal((N_A, LRU_W), 0.01),
        'lru_w_a': normal((N_A, LRU_BLOCKS, LRU_BW, LRU_BW), LRU_BW ** -0.5),
        'lru_b_a': normal((N_A, LRU_BLOCKS, LRU_BW), 0.01),
        'lru_w_i': normal((N_A, LRU_BLOCKS, LRU_BW, LRU_BW), LRU_BW ** -0.5),
        'lru_b_i': normal((N_A, LRU_BLOCKS, LRU_BW), 0.01),
        'lru_lambda': lru_lambda,
        'lru_w_out': normal((N_A, LRU_W, D_MODEL), LRU_W ** -0.5),
        'ssm_w_in': normal((N_B, D_MODEL, SSM_IN), D_MODEL ** -0.5),
        'ssm_conv_w': normal((N_B, CONV_W, SSM_CONV_DIM), CONV_W ** -0.5),
        'ssm_conv_b': normal((N_B, SSM_CONV_DIM), 0.01),
        'ssm_dt_bias': ssm_dt_bias,
        'ssm_a_log': jnp.log(uniform((N_B, SSM_H), 1.0, 16.0)),
        'ssm_d': gain((N_B, SSM_H)),
        'ssm_norm': gain((N_B, SSM_DI)),
        'ssm_w_out': normal((N_B, SSM_DI, D_MODEL), SSM_DI ** -0.5),
        'fox_w_in': normal((N_C, D_MODEL, ATT_IN), D_MODEL ** -0.5),
        'fox_b_f': uniform((N_C, ATT_H), 1.0, 6.0),
        'fox_q_norm': gain((N_C, ATT_HD)),
        'fox_k_norm': gain((N_C, ATT_HD)),
        'fox_w_out': normal((N_C, D_MODEL, D_MODEL), D_MODEL ** -0.5),
        'moe_w_group': normal((DEPTH, D_MODEL, MOE_GROUPS), D_MODEL ** -0.5),
        'moe_b_group': normal((DEPTH, MOE_GROUPS), 0.01),
        'moe_w_expert': normal((DEPTH, D_MODEL, MOE_E), D_MODEL ** -0.5),
        'moe_b_expert': normal((DEPTH, MOE_E), 0.01),
        'moe_w_gate': normal((DEPTH, MOE_E, D_MODEL, MOE_F), D_MODEL ** -0.5),
        'moe_w_up': normal((DEPTH, MOE_E, D_MODEL, MOE_F), D_MODEL ** -0.5),
        'moe_w_down': normal((DEPTH, MOE_E, MOE_F, D_MODEL), MOE_F ** -0.5),
        'ple_w_proj': normal((DEPTH, PLE_DIM, D_MODEL), PLE_DIM ** -0.5),
        'ple_w_gate': normal((DEPTH, D_MODEL, D_MODEL), D_MODEL ** -0.5),
        'norm_mix': gain((DEPTH, D_MODEL)),
        'norm_ffn': gain((DEPTH, D_MODEL)),
        'norm_ple': gain((DEPTH, D_MODEL)),
        'norm_final': gain((D_MODEL,)),
    }


def reference(x_prompt, x_sample, state_lru_h, state_lru_conv, state_ssm_h, state_ssm_conv,
              cache_k, cache_v, cache_logf, page_table, p_prompt, p_sample,
              lru_w_in, lru_conv_w, lru_conv_b, lru_w_a, lru_b_a, lru_w_i, lru_b_i, lru_lambda, lru_w_out,
              ssm_w_in, ssm_conv_w, ssm_conv_b, ssm_dt_bias, ssm_a_log, ssm_d, ssm_norm, ssm_w_out,
              fox_w_in, fox_b_f, fox_q_norm, fox_k_norm, fox_w_out,
              moe_w_group, moe_b_group, moe_w_expert, moe_b_expert, moe_w_gate, moe_w_up, moe_w_down,
              ple_w_proj, ple_w_gate, norm_mix, norm_ffn, norm_ple, norm_final):
    prm = dict(lru_w_in=lru_w_in, lru_conv_w=lru_conv_w, lru_conv_b=lru_conv_b, lru_w_a=lru_w_a,
               lru_b_a=lru_b_a, lru_w_i=lru_w_i, lru_b_i=lru_b_i, lru_lambda=lru_lambda, lru_w_out=lru_w_out,
               ssm_w_in=ssm_w_in, ssm_conv_w=ssm_conv_w, ssm_conv_b=ssm_conv_b, ssm_dt_bias=ssm_dt_bias,
               ssm_a_log=ssm_a_log, ssm_d=ssm_d, ssm_norm=ssm_norm, ssm_w_out=ssm_w_out,
               fox_w_in=fox_w_in, fox_b_f=fox_b_f, fox_q_norm=fox_q_norm, fox_k_norm=fox_k_norm,
               fox_w_out=fox_w_out, moe_w_group=moe_w_group, moe_b_group=moe_b_group,
               moe_w_expert=moe_w_expert, moe_b_expert=moe_b_expert, moe_w_gate=moe_w_gate,
               moe_w_up=moe_w_up, moe_w_down=moe_w_down, ple_w_proj=ple_w_proj, ple_w_gate=ple_w_gate,
               norm_mix=norm_mix, norm_ffn=norm_ffn, norm_ple=norm_ple, norm_final=norm_final)

    bp = x_prompt.shape[0]
    dt = x_prompt.dtype
    y_prompt, sp = trunk(x_prompt, p_prompt,
                         jnp.zeros((N_A, bp, LRU_W), dt),
                         jnp.zeros((N_A, bp, CONV_W - 1, LRU_W), dt),
                         jnp.zeros((N_B, bp, SSM_H, SSM_P, SSM_N), dt),
                         jnp.zeros((N_B, bp, CONV_W - 1, SSM_CONV_DIM), dt),
                         lambda j: None, prm)

    n_pages = PAST_LEN // PAGE_SIZE
    bs = page_table.shape[0]

    def fox_past(j):
        def gather(c):
            return c[j, page_table].reshape((bs, n_pages * PAGE_SIZE) + c.shape[3:])
        return gather(cache_k), gather(cache_v), gather(cache_logf)

    y_sample, ss = trunk(x_sample, p_sample, state_lru_h, state_lru_conv, state_ssm_h, state_ssm_conv,
                         fox_past, prm)

    lru_h_p, lru_conv_p, ssm_h_p, ssm_conv_p, k_p, v_p, logf_p = sp
    lru_h_s, lru_conv_s, ssm_h_s, ssm_conv_s, k_s, v_s, logf_s = ss
    return (y_prompt, y_sample, lru_h_p, lru_h_s, lru_conv_p, lru_conv_s, ssm_h_p, ssm_h_s,
            ssm_conv_p, ssm_conv_s, k_p, k_s, v_p, v_s, logf_p, logf_s)
```

```python
import functools

import jax
import jax.numpy as jnp
from jax import lax
from jax.experimental import pallas as pl
from jax.experimental.pallas import tpu as pltpu

F32 = jnp.float32
BF16 = jnp.bfloat16
HI = lax.Precision.HIGHEST

EPS = 1e-6
CONV_W = 4
LANES = 128
SUBLANES = 8
LRU_C = 8.0
LRU_BLOCKS = 16
SSM_P = 64
SSM_G = 8
SSM_N = 128
ATT_HD = 64
MOE_GROUPS = 4
MOE_EPG = 4
MOE_E = MOE_GROUPS * MOE_EPG
NEG = -1e30
MIB = 1024 * 1024


def _cparams(sem, vmem_mib=48):
    return pltpu.CompilerParams(dimension_semantics=sem, vmem_limit_bytes=vmem_mib * MIB)


def _rms(x, g):
    ms = jnp.mean(x * x, axis=-1, keepdims=True)
    return x * lax.rsqrt(ms + EPS) * g


def _softplus(z):
    return jnp.maximum(z, 0.0) + jnp.log1p(jnp.exp(-jnp.abs(z)))


def _sigmoid(z):
    return 1.0 / (1.0 + jnp.exp(-z))


def _silu(z):
    return z * _sigmoid(z)


def _gelu_tanh(z):
    c = 0.7978845608028654
    return 0.5 * z * (1.0 + jnp.tanh(c * (z + 0.044715 * (z * z * z))))


def _dot(a, b):
    return jnp.dot(a, b, preferred_element_type=F32)


def _dot_hi(a, b):
    return jnp.dot(a, b, preferred_element_type=F32, precision=HI)


def _dot_nt(a, b):
    return lax.dot_general(a, b, (((1,), (1,)), ((), ())), preferred_element_type=F32)


def _dot_nt_hi(a, b):
    return lax.dot_general(a, b, (((1,), (1,)), ((), ())), preferred_element_type=F32, precision=HI)


def _pad_cols(w, n):
    return jnp.pad(w, ((0, 0), (0, n - w.shape[1])))


def _norm_matmul_kernel(x_ref, g_ref, w_ref, o_ref, xn_ref):
    @pl.when(pl.program_id(1) == 0)
    def _():
        xn_ref[...] = _rms(x_ref[...], g_ref[...]).astype(xn_ref.dtype)

    o_ref[...] = _dot(xn_ref[...], w_ref[...]).astype(o_ref.dtype)


def _norm_matmul_hi_kernel(x_ref, g_ref, w_ref, o_ref):
    o_ref[...] = _dot_hi(_rms(x_ref[...], g_ref[...]), w_ref[...])


def norm_matmul(x, g, w, *, tm=1024, tn=1024, out_dtype=F32):
    T, D = x.shape
    N = w.shape[1]
    tm, tn = min(tm, T), min(tn, N)
    assert T % tm == 0 and N % tn == 0
    return pl.pallas_call(
        _norm_matmul_kernel,
        grid=(T // tm, N // tn),
        in_specs=[pl.BlockSpec((tm, D), lambda i, j: (i, 0)),
                  pl.BlockSpec((1, D), lambda i, j: (0, 0)),
                  pl.BlockSpec((D, tn), lambda i, j: (0, j))],
        out_specs=pl.BlockSpec((tm, tn), lambda i, j: (i, j)),
        out_shape=jax.ShapeDtypeStruct((T, N), out_dtype),
        scratch_shapes=[pltpu.VMEM((tm, D), BF16)],
        compiler_params=_cparams(("parallel", "arbitrary")),
        name="norm_matmul",
    )(x, g.reshape(1, D), w)


def norm_matmul_hi(x, g, w, *, tm=512):
    T, D = x.shape
    N = w.shape[1]
    tm = min(tm, T)
    assert T % tm == 0
    return pl.pallas_call(
        _norm_matmul_hi_kernel,
        grid=(T // tm,),
        in_specs=[pl.BlockSpec((tm, D), lambda i: (i, 0)),
                  pl.BlockSpec((1, D), lambda i: (0, 0)),
                  pl.BlockSpec((D, N), lambda i: (0, 0))],
        out_specs=pl.BlockSpec((tm, N), lambda i: (i, 0)),
        out_shape=jax.ShapeDtypeStruct((T, N), F32),
        compiler_params=_cparams(("parallel",)),
        name="norm_matmul_hi",
    )(x, g.reshape(1, D), w)


def _matmul_res_kernel(a_ref, w_ref, r_ref, o_ref):
    o_ref[...] = r_ref[...] + _dot(a_ref[...], w_ref[...])


def matmul_res(a, w, res, *, tm=512):
    T, K = a.shape
    N = w.shape[1]
    tm = min(tm, T)
    assert T % tm == 0
    return pl.pallas_call(
        _matmul_res_kernel,
        grid=(T // tm,),
        in_specs=[pl.BlockSpec((tm, K), lambda i: (i, 0)),
                  pl.BlockSpec((K, N), lambda i: (0, 0)),
                  pl.BlockSpec((tm, N), lambda i: (i, 0))],
        out_specs=pl.BlockSpec((tm, N), lambda i: (i, 0)),
        out_shape=jax.ShapeDtypeStruct((T, N), F32),
        compiler_params=_cparams(("parallel",)),
        name="matmul_res",
    )(a, w, res)


def _ple_kernel(h_ref, p_ref, g_ref, wg_ref, wp_ref, gf_ref, o_ref, *maybe_final):
    h = h_ref[...]
    xn = _rms(h, g_ref[...]).astype(BF16)
    gate = _sigmoid(_dot(xn, wg_ref[...]))
    out = h + gate * _dot(p_ref[...].astype(BF16), wp_ref[...])
    o_ref[...] = out
    if maybe_final:
        maybe_final[0][...] = _rms(out, gf_ref[...])


def ple(h, p, g, w_gate, w_proj, g_final=None, *, tm=512):
    T, D = h.shape
    P = p.shape[1]
    tm = min(tm, T)
    assert T % tm == 0
    final = g_final is not None
    gf = (g_final if final else g).reshape(1, D)
    row = pl.BlockSpec((tm, D), lambda i: (i, 0))
    vec = pl.BlockSpec((1, D), lambda i: (0, 0))
    out_shape = jax.ShapeDtypeStruct((T, D), F32)
    return pl.pallas_call(
        _ple_kernel,
        grid=(T // tm,),
        in_specs=[row, pl.BlockSpec((tm, P), lambda i: (i, 0)), vec,
                  pl.BlockSpec((D, D), lambda i: (0, 0)),
                  pl.BlockSpec((P, D), lambda i: (0, 0)), vec],
        out_specs=(row, row) if final else row,
        out_shape=(out_shape, out_shape) if final else out_shape,
        compiler_params=_cparams(("parallel",)),
        name="ple",
    )(h, p, g.reshape(1, D), w_gate, w_proj, gf)


def _route(logits):
    lane = lax.broadcasted_iota(jnp.int32, logits.shape, 1)
    big = jnp.int32(1 << 20)
    is_g = (lane >= MOE_E) & (lane < MOE_E + MOE_GROUPS)
    glog = jnp.where(is_g, logits, NEG)
    gmax = jnp.max(glog, axis=-1, keepdims=True)
    gsel = jnp.min(jnp.where(is_g & (glog == gmax), lane, big), axis=-1, keepdims=True) - MOE_E
    gden = jnp.sum(jnp.where(is_g, jnp.exp(glog - gmax), 0.0), axis=-1, keepdims=True)
    gprob = 1.0 / gden
    in_g = (lane < MOE_E) & ((lane // MOE_EPG) == gsel)
    e1 = jnp.where(in_g, logits, NEG)
    v1 = jnp.max(e1, axis=-1, keepdims=True)
    i1 = jnp.min(jnp.where(in_g & (e1 == v1), lane, big), axis=-1, keepdims=True)
    in_g2 = in_g & (lane != i1)
    e2 = jnp.where(in_g2, logits, NEG)
    v2 = jnp.max(e2, axis=-1, keepdims=True)
    i2 = jnp.min(jnp.where(in_g2 & (e2 == v2), lane, big), axis=-1, keepdims=True)
    t = jnp.exp(v2 - v1)
    w1 = gprob / (1.0 + t)
    w2 = gprob * t / (1.0 + t)
    return jnp.where(lane == i1, w1, jnp.where(lane == i2, w2, 0.0))


def _moe_dense_kernel(h_ref, g_ref, wr_ref, br_ref, wg_ref, wu_ref, wd_ref, o_ref, xn_ref, comb_ref, acc_ref):
    e = pl.program_id(1)

    @pl.when(e == 0)
    def _():
        xn = _rms(h_ref[...], g_ref[...])
        comb_ref[...] = _route(_dot_hi(xn, wr_ref[...]) + br_ref[...])
        xn_ref[...] = xn.astype(BF16)
        acc_ref[...] = jnp.zeros_like(acc_ref)

    xn = xn_ref[...]
    hg = _dot(xn, wg_ref[0])
    hu = _dot(xn, wu_ref[0])
    comb = comb_ref[...]
    lane = lax.broadcasted_iota(jnp.int32, comb.shape, 1)
    c = jnp.sum(jnp.where(lane == e, comb, 0.0), axis=-1, keepdims=True)
    hid = (_silu(hg) * hu * c).astype(BF16)
    acc_ref[...] += _dot(hid, wd_ref[0])

    @pl.when(e == pl.num_programs(1) - 1)
    def _():
        o_ref[...] = h_ref[...] + acc_ref[...]


def moe_dense(h, g, w_router, b_router, w_gate, w_up, w_down, *, tm=1024):
    T, D = h.shape
    E, _, Fd = w_gate.shape
    tm = min(tm, T)
    assert T % tm == 0
    row = pl.BlockSpec((tm, D), lambda i, e: (i, 0))
    return pl.pallas_call(
        _moe_dense_kernel,
        grid=(T // tm, E),
        in_specs=[row, pl.BlockSpec((1, D), lambda i, e: (0, 0)),
                  pl.BlockSpec((D, LANES), lambda i, e: (0, 0)),
                  pl.BlockSpec((1, LANES), lambda i, e: (0, 0)),
                  pl.BlockSpec((1, D, Fd), lambda i, e: (e, 0, 0)),
                  pl.BlockSpec((1, D, Fd), lambda i, e: (e, 0, 0)),
                  pl.BlockSpec((1, Fd, D), lambda i, e: (e, 0, 0))],
        out_specs=row,
        out_shape=jax.ShapeDtypeStruct((T, D), F32),
        scratch_shapes=[pltpu.VMEM((tm, D), BF16), pltpu.VMEM((tm, LANES), F32), pltpu.VMEM((tm, D), F32)],
        compiler_params=_cparams(("parallel", "arbitrary")),
        name="moe_dense",
    )(h, g.reshape(1, D), w_router, b_router, w_gate, w_up, w_down)


def _conv_tile(xpad_ref, x, cw_ref, cb_ref, first):
    tl = x.shape[0]

    @pl.when(first)
    def _():
        xpad_ref[0:SUBLANES, :] = jnp.zeros((SUBLANES, x.shape[1]), F32)

    xpad_ref[SUBLANES:SUBLANES + tl, :] = x
    out = cb_ref[...] + x * cw_ref[CONV_W - 1:CONV_W, :]
    for k in range(CONV_W - 1):
        lo = SUBLANES - (CONV_W - 1) + k
        out = out + xpad_ref[lo:lo + tl, :] * cw_ref[k:k + 1, :]
    tail = xpad_ref[tl:tl + SUBLANES, :]
    xpad_ref[0:SUBLANES, :] = tail
    return out, tail[SUBLANES - (CONV_W - 1):, :]


def _lru_gates(xc, wa_ref, ba_ref, wi_ref, bi_ref, lam_ref):
    xcb = xc.astype(BF16)
    nb = wa_ref.shape[0]
    cw = wa_ref.shape[1]
    ra = jnp.concatenate([_dot(xcb[:, c * cw:(c + 1) * cw], wa_ref[c]) for c in range(nb)], axis=1)
    ia = jnp.concatenate([_dot(xcb[:, c * cw:(c + 1) * cw], wi_ref[c]) for c in range(nb)], axis=1)
    r = _sigmoid(ra + ba_ref[...])
    ig = _sigmoid(ia + bi_ref[...])
    log_a = (-LRU_C) * r * _softplus(-lam_ref[...])
    a = jnp.exp(log_a)
    b = jnp.sqrt(-jnp.tanh(log_a) * (a * a + 1.0)) * (ig * xc)
    return a, b


def _lru_seq_kernel(xb_ref, yb_ref, cw_ref, cb_ref, wa_ref, ba_ref, wi_ref, bi_ref, lam_ref,
                    o_ref, buf_ref, hl_ref, xpad_ref, a_ref, b_ref, hc_ref):
    t = pl.program_id(1)
    tl, W = xb_ref.shape[1], xb_ref.shape[2]
    xc, tail = _conv_tile(xpad_ref, xb_ref[0], cw_ref, cb_ref, t == 0)
    a, b = _lru_gates(xc, wa_ref, ba_ref, wi_ref, bi_ref, lam_ref)
    a_ref[...] = a
    b_ref[...] = b

    @pl.when(t == 0)
    def _():
        hc_ref[...] = jnp.zeros_like(hc_ref)

    row = lax.broadcasted_iota(jnp.int32, (SUBLANES, W), 0)

    def body(i, hc):
        r0 = pl.multiple_of(i * SUBLANES, SUBLANES)
        av = a_ref[pl.ds(r0, SUBLANES), :]
        bv = b_ref[pl.ds(r0, SUBLANES), :]
        for s in (1, 2, 4):
            keep = row >= s
            a_sh = pltpu.roll(av, s, 0)
            b_sh = pltpu.roll(bv, s, 0)
            bv = jnp.where(keep, av * b_sh + bv, bv)
            av = jnp.where(keep, av * a_sh, av)
        h = av * hc + bv
        b_ref[pl.ds(r0, SUBLANES), :] = h
        return jnp.broadcast_to(h[SUBLANES - 1:SUBLANES, :], (SUBLANES, W))

    hc = lax.fori_loop(0, tl // SUBLANES, body, hc_ref[...])
    hc_ref[...] = hc
    o_ref[0] = (b_ref[...] * _gelu_tanh(yb_ref[0])).astype(o_ref.dtype)
    buf_ref[0] = tail
    hl_ref[0] = hc[0:1, :]


def lru_seq(xy, conv_w, conv_b, wa_bd, b_a, wi_bd, b_i, lam, *, tl=256):
    B, L, W2 = xy.shape
    W = W2 // 2
    tl = min(tl, L)
    assert L % tl == 0 and tl % SUBLANES == 0
    vec = pl.BlockSpec((1, W), lambda b, t: (0, 0))
    wbd = pl.BlockSpec(wa_bd.shape, lambda b, t: (0, 0, 0))
    return pl.pallas_call(
        _lru_seq_kernel,
        grid=(B, L // tl),
        in_specs=[pl.BlockSpec((1, tl, W), lambda b, t: (b, t, 0)),
                  pl.BlockSpec((1, tl, W), lambda b, t: (b, t, 1)),
                  pl.BlockSpec((CONV_W, W), lambda b, t: (0, 0)), vec, wbd, vec, wbd, vec, vec],
        out_specs=(pl.BlockSpec((1, tl, W), lambda b, t: (b, t, 0)),
                   pl.BlockSpec((1, CONV_W - 1, W), lambda b, t: (b, 0, 0)),
                   pl.BlockSpec((1, 1, W), lambda b, t: (b, 0, 0))),
        out_shape=(jax.ShapeDtypeStruct((B, L, W), BF16),
                   jax.ShapeDtypeStruct((B, CONV_W - 1, W), F32),
                   jax.ShapeDtypeStruct((B, 1, W), F32)),
        scratch_shapes=[pltpu.VMEM((tl + SUBLANES, W), F32), pltpu.VMEM((tl, W), F32),
                        pltpu.VMEM((tl, W), F32), pltpu.VMEM((SUBLANES, W), F32)],
        compiler_params=_cparams(("parallel", "arbitrary")),
        name="lru_seq",
    )(xy, xy, conv_w, conv_b.reshape(1, W), wa_bd, b_a.reshape(1, W), wi_bd, b_i.reshape(1, W), lam.reshape(1, W))


def _lru_step_kernel(xb_ref, yb_ref, buf_ref, h0_ref, cw_ref, cb_ref, wa_ref, ba_ref, wi_ref, bi_ref, lam_ref,
                     o_ref, nbuf_ref, hl_ref):
    x = xb_ref[...]
    xc = cb_ref[...] + x * cw_ref[CONV_W - 1:CONV_W, :]
    for k in range(CONV_W - 1):
        xc = xc + buf_ref[k] * cw_ref[k:k + 1, :]
    a, b = _lru_gates(xc, wa_ref, ba_ref, wi_ref, bi_ref, lam_ref)
    h = a * h0_ref[...] + b
    o_ref[...] = (h * _gelu_tanh(yb_ref[...])).astype(o_ref.dtype)
    hl_ref[...] = h
    for k in range(CONV_W - 2):
        nbuf_ref[k] = buf_ref[k + 1]
    nbuf_ref[CONV_W - 2] = x


def lru_step(xy, buf_t, h0, conv_w, conv_b, wa_bd, b_a, wi_bd, b_i, lam):
    B, W2 = xy.shape
    W = W2 // 2
    vec = pl.BlockSpec((1, W), lambda i: (0, 0))
    wbd = pl.BlockSpec(wa_bd.shape, lambda i: (0, 0, 0))
    mat = pl.BlockSpec((B, W), lambda i: (0, 0))
    cube = pl.BlockSpec((CONV_W - 1, B, W), lambda i: (0, 0, 0))
    return pl.pallas_call(
        _lru_step_kernel,
        grid=(1,),
        in_specs=[mat, pl.BlockSpec((B, W), lambda i: (0, 1)), cube, mat,
                  pl.BlockSpec((CONV_W, W), lambda i: (0, 0)), vec, wbd, vec, wbd, vec, vec],
        out_specs=(mat, cube, mat),
        out_shape=(jax.ShapeDtypeStruct((B, W), BF16),
                   jax.ShapeDtypeStruct((CONV_W - 1, B, W), F32),
                   jax.ShapeDtypeStruct((B, W), F32)),
        compiler_params=_cparams(("arbitrary",)),
        name="lru_step",
    )(xy, xy, buf_t, h0, conv_w, conv_b.reshape(1, W), wa_bd, b_a.reshape(1, W), wi_bd, b_i.reshape(1, W),
      lam.reshape(1, W))


def _group_norm_gate(y, z, ng, n_groups):
    y = y * _silu(z)
    gw = y.shape[1] // n_groups
    outs = []
    for g in range(n_groups):
        yg = y[:, g * gw:(g + 1) * gw]
        ms = jnp.mean(yg * yg, axis=-1, keepdims=True)
        outs.append(yg * lax.rsqrt(ms + EPS))
    return jnp.concatenate(outs, axis=1) * ng


def _ssd_seq_kernel(z_ref, xbc_ref, dt_ref, cw_ref, cb_ref, dtb_ref, alog_ref, dsk_ref, ng_ref, ex_ref,
                    o_ref, buf_ref, st_ref, xpad_ref, s_ref):
    t = pl.program_id(1)
    Q = xbc_ref.shape[1]
    DI = z_ref.shape[2]
    GN = SSM_G * SSM_N
    RP = DI // SSM_G
    xc, tail = _conv_tile(xpad_ref, xbc_ref[0], cw_ref, cb_ref, t == 0)
    xc = _silu(xc)
    xs = xc[:, :DI]
    bm = xc[:, DI:DI + GN]
    cm = xc[:, DI + GN:]

    @pl.when(t == 0)
    def _():
        s_ref[...] = jnp.zeros_like(s_ref)

    dt = _softplus(dt_ref[0] + dtb_ref[...])
    a_neg = -jnp.exp(alog_ref[...])
    ri = lax.broadcasted_iota(jnp.int32, (Q, Q), 0)
    ci = lax.broadcasted_iota(jnp.int32, (Q, Q), 1)
    tri = ci <= ri
    acum = _dot_hi(tri.astype(F32), dt * a_neg)
    acum_t = acum.T
    ex = ex_ref[...]
    dt_e = _dot_hi(dt, ex)
    acum_e = _dot_hi(acum, ex)
    last_e = acum_e[Q - 1:Q, :]
    xdt = xs * dt_e
    xdtw = (xdt * jnp.exp(last_e - acum_e)).astype(BF16)
    xdt = xdt.astype(BF16)
    eacum = jnp.exp(acum_e)
    edec = jnp.exp(last_e)
    lane = lax.broadcasted_iota(jnp.int32, (Q, LANES), 1)
    lo = lane < SSM_P
    ys = []
    for g in range(SSM_G):
        cg = cm[:, g * SSM_N:(g + 1) * SSM_N].astype(BF16)
        bg32 = bm[:, g * SSM_N:(g + 1) * SSM_N]
        bg = bg32.astype(BF16)
        cb = _dot_nt(cg, bg)
        st = s_ref[g]
        yoff = _dot(cg, st.astype(BF16)) * eacum[:, g * RP:(g + 1) * RP]
        for pr in range(RP // LANES):
            ms = []
            for k in range(LANES // SSM_P):
                hd = (g * RP + pr * LANES) // SSM_P + k
                seg = acum[:, hd:hd + 1] - acum_t[hd:hd + 1, :]
                ms.append((cb * jnp.exp(jnp.where(tri, seg, NEG))).astype(BF16))
            c0 = g * RP + pr * LANES
            xp = xdt[:, c0:c0 + LANES]
            zero = jnp.zeros_like(xp)
            rhs = jnp.concatenate([jnp.where(lo, xp, zero), jnp.where(lo, zero, xp)], axis=0)
            ydiag = _dot(jnp.concatenate(ms, axis=1), rhs)
            ys.append(ydiag + yoff[:, pr * LANES:(pr + 1) * LANES])
        new = st * edec[:, g * RP:(g + 1) * RP] + _dot(bg32.T.astype(BF16), xdtw[:, g * RP:(g + 1) * RP])
        s_ref[g] = new
    y = jnp.concatenate(ys, axis=1) + xs * dsk_ref[...]
    o_ref[0] = _group_norm_gate(y, z_ref[0], ng_ref[...], SSM_G).astype(o_ref.dtype)
    buf_ref[0] = tail

    @pl.when(t == pl.num_programs(1) - 1)
    def _():
        hpg = RP // SSM_P
        for g in range(SSM_G):
            st_ref[0, g * hpg:(g + 1) * hpg] = s_ref[g].T.reshape(hpg, SSM_P, SSM_N)


def ssd_seq(z, xbc, dt, conv_w, conv_b, dt_bias, a_log, d_lanes, norm_g, expand, *, q=128):
    B, L, DI = z.shape
    C = xbc.shape[2]
    H = DI // SSM_P
    q = min(q, L)
    assert L % q == 0
    vecd = pl.BlockSpec((1, DI), lambda b, t: (0, 0))
    vecl = pl.BlockSpec((1, LANES), lambda b, t: (0, 0))
    return pl.pallas_call(
        _ssd_seq_kernel,
        grid=(B, L // q),
        in_specs=[pl.BlockSpec((1, q, DI), lambda b, t: (b, t, 0)),
                  pl.BlockSpec((1, q, C), lambda b, t: (b, t, 0)),
                  pl.BlockSpec((1, q, LANES), lambda b, t: (b, t, 0)),
                  pl.BlockSpec((CONV_W, C), lambda b, t: (0, 0)),
                  pl.BlockSpec((1, C), lambda b, t: (0, 0)),
                  vecl, vecl, vecd, vecd,
                  pl.BlockSpec((LANES, DI), lambda b, t: (0, 0))],
        out_specs=(pl.BlockSpec((1, q, DI), lambda b, t: (b, t, 0)),
                   pl.BlockSpec((1, CONV_W - 1, C), lambda b, t: (b, 0, 0)),
                   pl.BlockSpec((1, H, SSM_P, SSM_N), lambda b, t: (b, 0, 0, 0))),
        out_shape=(jax.ShapeDtypeStruct((B, L, DI), BF16),
                   jax.ShapeDtypeStruct((B, CONV_W - 1, C), F32),
                   jax.ShapeDtypeStruct((B, H, SSM_P, SSM_N), F32)),
        scratch_shapes=[pltpu.VMEM((q + SUBLANES, C), F32),
                        pltpu.VMEM((SSM_G, SSM_N, DI // SSM_G), F32)],
        compiler_params=_cparams(("parallel", "arbitrary")),
        name="ssd_seq",
    )(z, xbc, dt, conv_w, conv_b.reshape(1, C), dt_bias, a_log, d_lanes, norm_g.reshape(1, DI), expand)


def _ssd_step_kernel(z_ref, xbc_ref, dt_ref, buf_ref, h0_ref, cw_ref, cb_ref, dtb_ref, alog_ref, dsk_ref, ng_ref,
                     ex_ref, o_ref, nbuf_ref, hn_ref):
    DI = z_ref.shape[2]
    GN = SSM_G * SSM_N
    hpg = DI // SSM_P // SSM_G
    x = xbc_ref[0]
    buf = buf_ref[0]
    xc = cb_ref[...] + x * cw_ref[CONV_W - 1:CONV_W, :]
    for k in range(CONV_W - 1):
        xc = xc + buf[k:k + 1, :] * cw_ref[k:k + 1, :]
    nbuf_ref[0, 0:CONV_W - 2, :] = buf[1:CONV_W - 1, :]
    nbuf_ref[0, CONV_W - 2:CONV_W - 1, :] = x
    xc = _silu(xc)
    xs = xc[:, :DI]
    dt = _softplus(dt_ref[0] + dtb_ref[...])
    dta = dt * (-jnp.exp(alog_ref[...]))
    ex = ex_ref[...]
    dec = jnp.exp(dta)
    dt_e = _dot_hi(jnp.broadcast_to(dt, (SUBLANES, LANES)), ex)[0:1, :]
    xdt = xs * dt_e
    RP = hpg * SSM_P
    eye = (lax.broadcasted_iota(jnp.int32, (RP, RP), 0) == lax.broadcasted_iota(jnp.int32, (RP, RP), 1))
    ys = []
    for g in range(SSM_G):
        brow = xc[:, DI + g * SSM_N:DI + (g + 1) * SSM_N]
        crow = xc[:, DI + GN + g * SSM_N:DI + GN + (g + 1) * SSM_N]
        xg = xdt[:, g * RP:(g + 1) * RP]
        xcol = jnp.sum(jnp.where(eye, jnp.broadcast_to(xg, (RP, RP)), 0.0), axis=-1, keepdims=True)
        news = []
        for r in range(hpg):
            hd = g * hpg + r
            new = h0_ref[0, hd] * dec[:, hd:hd + 1] + xcol[r * SSM_P:(r + 1) * SSM_P, :] * brow
            hn_ref[0, hd] = new
            news.append(new)
        new_g = jnp.concatenate(news, axis=0).astype(BF16)
        ys.append(_dot_nt(jnp.broadcast_to(crow, (SUBLANES, SSM_N)).astype(BF16), new_g)[0:1, :])
    y = jnp.concatenate(ys, axis=1) + xs * dsk_ref[...]
    o_ref[0] = _group_norm_gate(y, z_ref[0], ng_ref[...], SSM_G).astype(o_ref.dtype)


def ssd_step(z, xbc, dt, buf, h0, conv_w, conv_b, dt_bias, a_log, d_lanes, norm_g, expand):
    B, _, DI = z.shape
    C = xbc.shape[2]
    H = DI // SSM_P
    vecd = pl.BlockSpec((1, DI), lambda b: (0, 0))
    vecl = pl.BlockSpec((1, LANES), lambda b: (0, 0))
    st = pl.BlockSpec((1, H, SSM_P, SSM_N), lambda b: (b, 0, 0, 0))
    return pl.pallas_call(
        _ssd_step_kernel,
        grid=(B,),
        in_specs=[pl.BlockSpec((1, 1, DI), lambda b: (b, 0, 0)),
                  pl.BlockSpec((1, 1, C), lambda b: (b, 0, 0)),
                  pl.BlockSpec((1, 1, LANES), lambda b: (b, 0, 0)),
                  pl.BlockSpec((1, CONV_W - 1, C), lambda b: (b, 0, 0)), st,
                  pl.BlockSpec((CONV_W, C), lambda b: (0, 0)),
                  pl.BlockSpec((1, C), lambda b: (0, 0)),
                  vecl, vecl, vecd, vecd,
                  pl.BlockSpec((LANES, DI), lambda b: (0, 0))],
        out_specs=(pl.BlockSpec((1, 1, DI), lambda b: (b, 0, 0)),
                   pl.BlockSpec((1, CONV_W - 1, C), lambda b: (b, 0, 0)), st),
        out_shape=(jax.ShapeDtypeStruct((B, 1, DI), BF16),
                   jax.ShapeDtypeStruct((B, CONV_W - 1, C), F32),
                   jax.ShapeDtypeStruct((B, H, SSM_P, SSM_N), F32)),
        compiler_params=_cparams(("parallel",)),
        name="ssd_step",
    )(z, xbc, dt, buf, h0, conv_w, conv_b.reshape(1, C), dt_bias, a_log, d_lanes, norm_g.reshape(1, DI), expand)


def _split_dot(x, m):
    hi = x.astype(BF16)
    lo = (x - hi.astype(F32)).astype(BF16)
    return _dot(hi, m) + _dot(lo, m)


def _fox_prep_kernel(q_ref, k_ref, fl_ref, bf_ref, qg_ref, kg_ref, hs_ref, he_ref,
                     qo_ref, ko_ref, lf_ref, ct_ref, carry_ref):
    t = pl.program_id(1)
    tl = q_ref.shape[1]

    def head_norm(x, g):
        ss = _split_dot(x * x, hs_ref[...])
        inv = lax.rsqrt(ss * (1.0 / ATT_HD) + EPS)
        return x * _split_dot(inv, he_ref[...]) * g

    qo_ref[0] = head_norm(q_ref[0], qg_ref[...]).astype(qo_ref.dtype)
    ko_ref[0] = head_norm(k_ref[0], kg_ref[...])
    z = fl_ref[0] + bf_ref[...]
    logf = jnp.minimum(z, 0.0) - jnp.log1p(jnp.exp(-jnp.abs(z)))
    lf_ref[0] = logf

    @pl.when(t == 0)
    def _():
        carry_ref[...] = jnp.zeros_like(carry_ref)

    ri = lax.broadcasted_iota(jnp.int32, (tl, tl), 0)
    ci = lax.broadcasted_iota(jnp.int32, (tl, tl), 1)
    c = _dot_hi((ci <= ri).astype(F32), logf) + carry_ref[0:1, :]
    carry_ref[...] = jnp.broadcast_to(c[tl - 1:tl, :], carry_ref.shape)
    ct_ref[0] = c.T


def fox_prep(q, k, fl, b_f, q_g, k_g, head_sum, head_expand, *, tl=256):
    B, L, D = q.shape
    tl = min(tl, L)
    assert L % tl == 0
    row = pl.BlockSpec((1, tl, D), lambda b, t: (b, t, 0))
    nar = pl.BlockSpec((1, tl, LANES), lambda b, t: (b, t, 0))
    vecd = pl.BlockSpec((1, D), lambda b, t: (0, 0))
    return pl.pallas_call(
        _fox_prep_kernel,
        grid=(B, L // tl),
        in_specs=[row, row, nar, pl.BlockSpec((1, LANES), lambda b, t: (0, 0)), vecd, vecd,
                  pl.BlockSpec((D, LANES), lambda b, t: (0, 0)),
                  pl.BlockSpec((LANES, D), lambda b, t: (0, 0))],
        out_specs=(row, row, nar, pl.BlockSpec((1, LANES, tl), lambda b, t: (b, 0, t))),
        out_shape=(jax.ShapeDtypeStruct((B, L, D), BF16),
                   jax.ShapeDtypeStruct((B, L, D), F32),
                   jax.ShapeDtypeStruct((B, L, LANES), F32),
                   jax.ShapeDtypeStruct((B, LANES, L), F32)),
        scratch_shapes=[pltpu.VMEM((SUBLANES, LANES), F32)],
        compiler_params=_cparams(("parallel", "arbitrary")),
        name="fox_prep",
    )(q, k, fl, b_f, q_g, k_g, head_sum, head_expand)


def _fox_attn_kernel(q_ref, k_ref, v_ref, ct_ref, g_ref, o_ref, m_ref, l_ref, acc_ref, *, scale):
    qi = pl.program_id(2)
    kj = pl.program_id(3)
    tq, tk = q_ref.shape[1], k_ref.shape[1]
    nh = LANES // ATT_HD
    p_id = pl.program_id(1)

    @pl.when(kj == 0)
    def _():
        m_ref[...] = jnp.full_like(m_ref, NEG)
        l_ref[...] = jnp.zeros_like(l_ref)
        acc_ref[...] = jnp.zeros_like(acc_ref)

    @pl.when(kj <= qi)
    def _():
        q = q_ref[0]
        kb = k_ref[0].astype(BF16)
        vb = v_ref[0].astype(BF16)
        lane = lax.broadcasted_iota(jnp.int32, (tq, LANES), 1)
        rows = lax.broadcasted_iota(jnp.int32, (tq, tk), 0)
        cols = lax.broadcasted_iota(jnp.int32, (tq, tk), 1)
        causal = (cols <= rows) | (kj < qi)
        for k in range(nh):
            mine = (lane >= k * ATT_HD) & (lane < (k + 1) * ATT_HD)
            qm = jnp.where(mine, q, jnp.zeros_like(q))
            s = _dot_nt(qm, kb) * scale - ct_ref[0, pl.ds(p_id * nh + k, 1), :]
            s = jnp.where(causal, s, NEG)
            m_old = m_ref[k]
            m_new = jnp.maximum(m_old, jnp.max(s, axis=-1, keepdims=True))
            alpha = jnp.exp(m_old - m_new)
            p = jnp.exp(s - m_new)
            l_ref[k] = l_ref[k] * alpha + jnp.sum(p, axis=-1, keepdims=True)
            m_ref[k] = m_new
            acc_ref[k] = acc_ref[k] * alpha + _dot(p.astype(BF16), vb)

    @pl.when(kj == pl.num_programs(3) - 1)
    def _():
        lane = lax.broadcasted_iota(jnp.int32, (tq, LANES), 1)
        o = jnp.zeros((tq, LANES), F32)
        for k in range(nh):
            mine = (lane >= k * ATT_HD) & (lane < (k + 1) * ATT_HD)
            o = jnp.where(mine, acc_ref[k] / l_ref[k], o)
        o_ref[0] = (o * _sigmoid(g_ref[0])).astype(o_ref.dtype)


def fox_attn(q, k, v, ct, g, *, tq=512, tk=512):
    B, L, D = q.shape
    tq, tk = min(tq, L), min(tk, L)
    assert L % tq == 0 and L % tk == 0 and tq == tk
    nh = LANES // ATT_HD
    kv = pl.BlockSpec((1, tk, LANES), lambda b, p, i, j: (b, jnp.minimum(i, j), p))
    qs = pl.BlockSpec((1, tq, LANES), lambda b, p, i, j: (b, i, p))
    return pl.pallas_call(
        functools.partial(_fox_attn_kernel, scale=ATT_HD ** -0.5),
        grid=(B, D // LANES, L // tq, L // tk),
        in_specs=[qs, kv, kv,
                  pl.BlockSpec((1, LANES, tk), lambda b, p, i, j: (b, 0, jnp.minimum(i, j))),
                  qs],
        out_specs=qs,
        out_shape=jax.ShapeDtypeStruct((B, L, D), BF16),
        scratch_shapes=[pltpu.VMEM((nh, tq, 1), F32), pltpu.VMEM((nh, tq, 1), F32),
                        pltpu.VMEM((nh, tq, LANES), F32)],
        compiler_params=_cparams(("parallel", "parallel", "parallel", "arbitrary")),
        name="fox_attn",
    )(q, k, v, ct, g)


def _fox_decode_kernel(pt_ref, q_ref, kn_ref, vn_ref, lfn_ref, g_ref, he_ref, *rest, npg, scale):
    k_refs = rest[:npg]
    v_refs = rest[npg:2 * npg]
    lf_refs = rest[2 * npg:3 * npg]
    o_ref = rest[3 * npg]
    qbd_ref, m_ref, l_ref, acc_ref, coff_ref = rest[3 * npg + 1:]
    s_id = pl.program_id(1)
    D = q_ref.shape[2]
    H = D // ATT_HD
    he = he_ref[...]

    @pl.when(s_id == 0)
    def _():
        hrow = lax.broadcasted_iota(jnp.int32, (LANES, D), 0)
        hcol = lax.broadcasted_iota(jnp.int32, (LANES, D), 1) // ATT_HD
        qb = jnp.broadcast_to(q_ref[0], (LANES, D))
        qbd_ref[...] = jnp.where(hrow == hcol, qb * scale, 0.0).astype(BF16)
        m_ref[...] = jnp.full_like(m_ref, NEG)
        l_ref[...] = jnp.zeros_like(l_ref)
        acc_ref[...] = jnp.zeros_like(acc_ref)
        coff_ref[...] = jnp.zeros_like(coff_ref)

    ps = k_refs[0].shape[1]
    ri = lax.broadcasted_iota(jnp.int32, (ps, ps), 0)
    ci = lax.broadcasted_iota(jnp.int32, (ps, ps), 1)
    tri = (ci <= ri).astype(F32)

    def absorb(s, vals):
        n = s.shape[0]
        m_old = m_ref[0:1, 0:H]
        m_new = jnp.maximum(m_old, jnp.max(s, axis=0, keepdims=True))
        alpha = jnp.exp(m_old - m_new)
        p = jnp.exp(s - m_new)
        l_ref[0:1, 0:H] = l_ref[0:1, 0:H] * alpha + jnp.sum(p, axis=0, keepdims=True)
        m_ref[0:1, 0:H] = m_new
        a_e = _dot_hi(jnp.broadcast_to(alpha, (SUBLANES, H)), he[0:H, :])[0:1, :]
        if n >= SUBLANES:
            pv = _dot(p.astype(BF16), he[0:H, :].astype(BF16)) * vals
            red = jnp.sum(pv.reshape(n // SUBLANES, SUBLANES, D), axis=0)
        else:
            pe = _dot_hi(jnp.broadcast_to(p, (SUBLANES, H)), he[0:H, :])[0:1, :]
            row0 = lax.broadcasted_iota(jnp.int32, (SUBLANES, D), 0) == 0
            red = jnp.where(row0, jnp.broadcast_to(pe * vals, (SUBLANES, D)), 0.0)
        acc_ref[...] = acc_ref[...] * a_e + red

    for i in range(npg):
        kb = k_refs[i][0].astype(BF16)
        qk = _dot_nt(kb, qbd_ref[...])[:, 0:H]
        lf = lf_refs[i][0]
        c = _dot_hi(tri, lf) + coff_ref[0:1, 0:H]
        coff_ref[0:1, 0:H] = c[ps - 1:ps, :]
        absorb(qk - c, v_refs[i][0])

    @pl.when(s_id == pl.num_programs(1) - 1)
    def _():
        qrow = q_ref[0] * scale
        qk_new = _dot_nt_hi(jnp.broadcast_to(qrow * kn_ref[0], (SUBLANES, D)), he[0:H, :])[0:1, :]
        c_q = coff_ref[0:1, 0:H] + lfn_ref[0][:, 0:H]
        absorb(qk_new - c_q, vn_ref[0])
        l_e = _dot_hi(jnp.broadcast_to(l_ref[0:1, 0:H], (SUBLANES, H)), he[0:H, :])[0:1, :]
        o = jnp.sum(acc_ref[...], axis=0, keepdims=True) / l_e
        o_ref[0] = (o * _sigmoid(g_ref[0])).astype(o_ref.dtype)


def fox_decode(q, k_new, v_new, lf_new, g, cache_k, cache_v, cache_lf, page_table, head_expand, *, npg=4):
    B, _, D = q.shape
    n_pages = page_table.shape[1]
    ps = cache_k.shape[1]
    H = cache_lf.shape[2]
    npg = min(npg, n_pages)
    assert n_pages % npg == 0
    steps = n_pages // npg
    row = pl.BlockSpec((1, 1, D), lambda b, s, pt: (b, 0, 0))

    def page(i, width):
        return pl.BlockSpec((1, ps, width), lambda b, s, pt: (pt[b * n_pages + s * npg + i], 0, 0))

    in_specs = ([row, row, row, pl.BlockSpec((1, 1, LANES), lambda b, s, pt: (b, 0, 0)), row,
                 pl.BlockSpec((LANES, D), lambda b, s, pt: (0, 0))]
                + [page(i, D) for i in range(npg)] + [page(i, D) for i in range(npg)]
                + [page(i, H) for i in range(npg)])
    grid_spec = pltpu.PrefetchScalarGridSpec(
        num_scalar_prefetch=1, grid=(B, steps), in_specs=in_specs, out_specs=row,
        scratch_shapes=[pltpu.VMEM((LANES, D), BF16), pltpu.VMEM((SUBLANES, LANES), F32),
                        pltpu.VMEM((SUBLANES, LANES), F32), pltpu.VMEM((SUBLANES, D), F32),
                        pltpu.VMEM((SUBLANES, LANES), F32)])
    return pl.pallas_call(
        functools.partial(_fox_decode_kernel, npg=npg, scale=ATT_HD ** -0.5),
        grid_spec=grid_spec,
        out_shape=jax.ShapeDtypeStruct((B, 1, D), BF16),
        compiler_params=_cparams(("parallel", "arbitrary")),
        name="fox_decode",
    )(page_table.reshape(-1), q, k_new, v_new, lf_new, g, head_expand,
      *([cache_k] * npg), *([cache_v] * npg), *([cache_lf] * npg))


def _block_diag_chunks(w, per_chunk):
    nblk, bw, _ = w.shape
    w = w.reshape(nblk // per_chunk, per_chunk, bw, bw)
    eye = jnp.eye(per_chunk, dtype=w.dtype)
    out = jnp.einsum('cpij,pq->cpiqj', w, eye)
    return out.reshape(nblk // per_chunk, per_chunk * bw, per_chunk * bw)


def _head_expand(n_heads, width):
    r = jnp.arange(LANES)[:, None]
    c = jnp.arange(n_heads * width)[None, :] // width
    return (r == c).astype(F32)


def _pad_lanes(v):
    return jnp.pad(v.reshape(1, -1), ((0, 0), (0, LANES - v.size)))


def _moe_and_ple(h, p_i, i, prm, last):
    h = moe_dense(h, prm['norm_ffn'][i], prm['moe_w_router'][i], prm['moe_b_router'][i],
                  prm['moe_w_gate'][i], prm['moe_w_up'][i], prm['moe_w_down'][i])
    return ple(h, p_i, prm['norm_ple'][i], prm['ple_w_gate'][i], prm['ple_w_proj'][i],
               prm['norm_final'] if last else None)


def _lru_layer_prompt(h, B, L, j, g, prm):
    D = h.shape[1]
    xy = norm_matmul(h, g, prm['lru_w_in'][j])
    gated, buf, hl = lru_seq(xy.reshape(B, L, -1), prm['lru_conv_w'][j], prm['lru_conv_b'][j],
                             prm['lru_wa_bd'][j], prm['lru_b_a'][j], prm['lru_wi_bd'][j], prm['lru_b_i'][j],
                             prm['lru_lambda'][j])
    h = matmul_res(gated.reshape(B * L, -1), prm['lru_w_out'][j], h)
    return h, buf, hl.reshape(B, D)


def _lru_layer_sample(h, buf, h0, j, g, prm):
    xy = norm_matmul(h, g, prm['lru_w_in'][j])
    gated, nbuf, hl = lru_step(xy, jnp.swapaxes(buf, 0, 1), h0, prm['lru_conv_w'][j], prm['lru_conv_b'][j],
                               prm['lru_wa_bd'][j], prm['lru_b_a'][j], prm['lru_wi_bd'][j], prm['lru_b_i'][j],
                               prm['lru_lambda'][j])
    h = matmul_res(gated, prm['lru_w_out'][j], h)
    return h, jnp.swapaxes(nbuf, 0, 1), hl


def _ssd_proj(h, g, j, prm):
    z = norm_matmul(h, g, prm['ssm_w_z'][j])
    xbc = norm_matmul(h, g, prm['ssm_w_xbc'][j])
    dt = norm_matmul_hi(h, g, prm['ssm_w_dt'][j])
    return z, xbc, dt


def _ssd_args(j, prm):
    return (prm['ssm_conv_w'][j], prm['ssm_conv_b'][j], prm['ssm_dt_bias'][j], prm['ssm_a_log'][j],
            prm['ssm_d_lanes'][j], prm['ssm_norm'][j], prm['ssm_expand'])


def _ssd_layer_prompt(h, B, L, j, g, prm):
    z, xbc, dt = _ssd_proj(h, g, j, prm)
    y, buf, st = ssd_seq(z.reshape(B, L, -1), xbc.reshape(B, L, -1), dt.reshape(B, L, -1), *_ssd_args(j, prm))
    return matmul_res(y.reshape(B * L, -1), prm['ssm_w_out'][j], h), buf, st


def _ssd_layer_sample(h, buf, h0, j, g, prm):
    B = h.shape[0]
    z, xbc, dt = _ssd_proj(h, g, j, prm)
    y, nbuf, st = ssd_step(z.reshape(B, 1, -1), xbc.reshape(B, 1, -1), dt.reshape(B, 1, -1), buf, h0,
                           *_ssd_args(j, prm))
    return matmul_res(y.reshape(B, -1), prm['ssm_w_out'][j], h), nbuf, st


def _fox_proj(h, g, j, prm):
    q = norm_matmul(h, g, prm['fox_w_q'][j])
    k = norm_matmul(h, g, prm['fox_w_k'][j])
    v = norm_matmul(h, g, prm['fox_w_v'][j])
    og = norm_matmul(h, g, prm['fox_w_g'][j])
    fl = norm_matmul_hi(h, g, prm['fox_w_f'][j])
    return q, k, v, og, fl


def _fox_prep_args(j, prm):
    return (prm['fox_b_f'][j], prm['fox_q_norm'][j], prm['fox_k_norm'][j], prm['fox_head_sum'],
            prm['fox_head_expand'])


def _fox_layer_prompt(h, B, L, j, g, prm, tq=512):
    q, k, v, og, fl = _fox_proj(h, g, j, prm)
    shp = (B, L, -1)
    qn, kn, logf, ct = fox_prep(q.reshape(shp), k.reshape(shp), fl.reshape(shp), *_fox_prep_args(j, prm))
    o = fox_attn(qn, kn, v.reshape(shp), ct, og.reshape(shp), tq=tq, tk=tq)
    return matmul_res(o.reshape(B * L, -1), prm['fox_w_out'][j], h), kn, v.reshape(shp), logf


def _fox_layer_sample(h, cache, n_phys, page_table, j, g, prm):
    B, D = h.shape
    q, k, v, og, fl = _fox_proj(h, g, j, prm)
    one = (1, B, -1)
    qn, kn, logf, _ = fox_prep(q.reshape(one), k.reshape(one), fl.reshape(one), *_fox_prep_args(j, prm))
    ck, cv, clf = cache
    tok = (B, 1, -1)
    o = fox_decode(qn.astype(F32).reshape(tok), kn.reshape(tok), v.reshape(tok), logf.reshape(tok), og.reshape(tok),
                   ck, cv, clf, page_table + j * n_phys, prm['fox_head_expand'])
    return matmul_res(o.reshape(B, D), prm['fox_w_out'][j], h), kn.reshape(tok), v.reshape(tok), logf.reshape(tok)


def _prepare_params(raw):
    prm = dict(raw)
    D = raw['norm_final'].shape[0]
    for name in ('lru_w_in', 'lru_w_out', 'ssm_w_out', 'fox_w_out', 'moe_w_gate', 'moe_w_up', 'moe_w_down',
                 'ple_w_proj', 'ple_w_gate'):
        prm[name] = raw[name].astype(BF16)
    per = 2 * LANES // (D // LRU_BLOCKS)
    prm['lru_wa_bd'] = jax.vmap(lambda w: _block_diag_chunks(w, per))(raw['lru_w_a']).astype(BF16)
    prm['lru_wi_bd'] = jax.vmap(lambda w: _block_diag_chunks(w, per))(raw['lru_w_i']).astype(BF16)
    prm['lru_b_a'] = raw['lru_b_a'].reshape(raw['lru_b_a'].shape[0], -1)
    prm['lru_b_i'] = raw['lru_b_i'].reshape(raw['lru_b_i'].shape[0], -1)
    n_h = raw['ssm_a_log'].shape[1]
    di = n_h * SSM_P
    w = raw['ssm_w_in']
    conv_dim = raw['ssm_conv_w'].shape[2]
    prm['ssm_w_z'] = w[:, :, :di].astype(BF16)
    prm['ssm_w_xbc'] = w[:, :, di:di + conv_dim].astype(BF16)
    prm['ssm_w_dt'] = jax.vmap(lambda m: _pad_cols(m, LANES))(w[:, :, di + conv_dim:])
    prm['ssm_dt_bias'] = jax.vmap(_pad_lanes)(raw['ssm_dt_bias'])
    prm['ssm_a_log'] = jax.vmap(_pad_lanes)(raw['ssm_a_log'])
    prm['ssm_d_lanes'] = jnp.repeat(raw['ssm_d'], SSM_P, axis=1)[:, None, :]
    prm['ssm_expand'] = _head_expand(n_h, SSM_P)
    w = raw['fox_w_in']
    for n, name in enumerate(('fox_w_q', 'fox_w_k', 'fox_w_v', 'fox_w_g')):
        prm[name] = w[:, :, n * D:(n + 1) * D].astype(BF16)
    prm['fox_w_f'] = jax.vmap(lambda m: _pad_cols(m, LANES))(w[:, :, 4 * D:])
    prm['fox_b_f'] = jax.vmap(_pad_lanes)(raw['fox_b_f'])
    n_ah = D // ATT_HD
    prm['fox_q_norm'] = jnp.tile(raw['fox_q_norm'], (1, n_ah))[:, None, :]
    prm['fox_k_norm'] = jnp.tile(raw['fox_k_norm'], (1, n_ah))[:, None, :]
    prm['fox_head_expand'] = _head_expand(n_ah, ATT_HD)
    prm['fox_head_sum'] = prm['fox_head_expand'].T.astype(BF16)
    prm['moe_w_router'] = jax.vmap(lambda we, wg: _pad_cols(jnp.concatenate([we, wg], axis=1), LANES))(
        raw['moe_w_expert'], raw['moe_w_group'])
    prm['moe_b_router'] = jax.vmap(lambda be, bg: _pad_lanes(jnp.concatenate([be, bg])))(
        raw['moe_b_expert'], raw['moe_b_group'])
    return prm


def kernel(x_prompt, x_sample, state_lru_h, state_lru_conv, state_ssm_h, state_ssm_conv, cache_k, cache_v, cache_logf, page_table, p_prompt, p_sample, lru_w_in, lru_conv_w, lru_conv_b, lru_w_a, lru_b_a, lru_w_i, lru_b_i, lru_lambda, lru_w_out, ssm_w_in, ssm_conv_w, ssm_conv_b, ssm_dt_bias, ssm_a_log, ssm_d, ssm_norm, ssm_w_out, fox_w_in, fox_b_f, fox_q_norm, fox_k_norm, fox_w_out, moe_w_group, moe_b_group, moe_w_expert, moe_b_expert, moe_w_gate, moe_w_up, moe_w_down, ple_w_proj, ple_w_gate, norm_mix, norm_ffn, norm_ple, norm_final):
    prm = _prepare_params(dict(
        lru_w_in=lru_w_in, lru_conv_w=lru_conv_w, lru_conv_b=lru_conv_b, lru_w_a=lru_w_a, lru_b_a=lru_b_a,
        lru_w_i=lru_w_i, lru_b_i=lru_b_i, lru_lambda=lru_lambda, lru_w_out=lru_w_out, ssm_w_in=ssm_w_in,
        ssm_conv_w=ssm_conv_w, ssm_conv_b=ssm_conv_b, ssm_dt_bias=ssm_dt_bias, ssm_a_log=ssm_a_log, ssm_d=ssm_d,
        ssm_norm=ssm_norm, ssm_w_out=ssm_w_out, fox_w_in=fox_w_in, fox_b_f=fox_b_f, fox_q_norm=fox_q_norm,
        fox_k_norm=fox_k_norm, fox_w_out=fox_w_out, moe_w_group=moe_w_group, moe_b_group=moe_b_group,
        moe_w_expert=moe_w_expert, moe_b_expert=moe_b_expert, moe_w_gate=moe_w_gate, moe_w_up=moe_w_up,
        moe_w_down=moe_w_down, ple_w_proj=ple_w_proj, ple_w_gate=ple_w_gate, norm_mix=norm_mix,
        norm_ffn=norm_ffn, norm_ple=norm_ple, norm_final=norm_final))
    depth = norm_mix.shape[0]
    B, L, D = x_prompt.shape
    Bs = x_sample.shape[0]
    n_mix = 3
    att_h = D // ATT_HD
    n_phys, page = cache_k.shape[1], cache_k.shape[2]
    cache = (cache_k.reshape(-1, page, D), cache_v.reshape(-1, page, D), cache_logf.reshape(-1, page, att_h))

    hp = x_prompt.reshape(B * L, D)
    hs = x_sample.reshape(Bs, D)
    outs = {k: [] for k in ('lru_h_p', 'lru_h_s', 'lru_c_p', 'lru_c_s', 'ssm_h_p', 'ssm_h_s', 'ssm_c_p', 'ssm_c_s',
                            'k_p', 'k_s', 'v_p', 'v_s', 'lf_p', 'lf_s')}
    yp = ys = None
    for i in range(depth):
        j = i // n_mix
        g = prm['norm_mix'][i]
        if i % n_mix == 0:
            hp, buf, hl = _lru_layer_prompt(hp, B, L, j, g, prm)
            outs['lru_c_p'].append(buf)
            outs['lru_h_p'].append(hl)
            hs, buf, hl = _lru_layer_sample(hs, state_lru_conv[j], state_lru_h[j], j, g, prm)
            outs['lru_c_s'].append(buf)
            outs['lru_h_s'].append(hl)
        elif i % n_mix == 1:
            hp, buf, st = _ssd_layer_prompt(hp, B, L, j, g, prm)
            outs['ssm_c_p'].append(buf)
            outs['ssm_h_p'].append(st)
            hs, buf, st = _ssd_layer_sample(hs, state_ssm_conv[j], state_ssm_h[j], j, g, prm)
            outs['ssm_c_s'].append(buf)
            outs['ssm_h_s'].append(st)
        else:
            hp, k, v, lf = _fox_layer_prompt(hp, B, L, j, g, prm)
            outs['k_p'].append(k.reshape(B, L, att_h, ATT_HD))
            outs['v_p'].append(v.reshape(B, L, att_h, ATT_HD))
            outs['lf_p'].append(lf[:, :, :att_h])
            hs, k, v, lf = _fox_layer_sample(hs, cache, n_phys, page_table, j, g, prm)
            outs['k_s'].append(k.reshape(Bs, 1, att_h, ATT_HD))
            outs['v_s'].append(v.reshape(Bs, 1, att_h, ATT_HD))
            outs['lf_s'].append(lf[:, :, :att_h])
        last = i == depth - 1
        hp = _moe_and_ple(hp, p_prompt[i].reshape(B * L, -1), i, prm, last)
        hs = _moe_and_ple(hs, p_sample[i].reshape(Bs, -1), i, prm, last)
        if last:
            hp, yp = hp
            hs, ys = hs
    st = {k: jnp.stack(v) for k, v in outs.items()}
    return (yp.reshape(B, L, D), ys.reshape(Bs, 1, D), st['lru_h_p'], st['lru_h_s'], st['lru_c_p'], st['lru_c_s'],
            st['ssm_h_p'], st['ssm_h_s'], st['ssm_c_p'], st['ssm_c_s'], st['k_p'], st['k_s'], st['v_p'], st['v_s'],
            st['lf_p'], st['lf_s'])
```

```python
import functools

import jax
import jax.numpy as jnp
from jax import lax
from jax.experimental import pallas as pl
from jax.experimental.pallas import tpu as pltpu

F32 = jnp.float32
BF16 = jnp.bfloat16
HI = lax.Precision.HIGHEST

EPS = 1e-6
CONV_W = 4
LANES = 128
SUBLANES = 8
LRU_C = 8.0
LRU_BLOCKS = 16
SSM_P = 64
SSM_G = 8
SSM_N = 128
ATT_HD = 64
MOE_GROUPS = 4
MOE_EPG = 4
MOE_E = MOE_GROUPS * MOE_EPG
NEG = -1e30
MIB = 1024 * 1024


def _cparams(sem, vmem_mib=48):
    return pltpu.CompilerParams(dimension_semantics=sem, vmem_limit_bytes=vmem_mib * MIB)


def _rms(x, g):
    ms = jnp.mean(x * x, axis=-1, keepdims=True)
    return x * lax.rsqrt(ms + EPS) * g


def _softplus(z):
    return jnp.maximum(z, 0.0) + jnp.log1p(jnp.exp(-jnp.abs(z)))


def _sigmoid(z):
    return 1.0 / (1.0 + jnp.exp(-z))


def _silu(z):
    return z * _sigmoid(z)


def _gelu_tanh(z):
    c = 0.7978845608028654
    return 0.5 * z * (1.0 + jnp.tanh(c * (z + 0.044715 * (z * z * z))))


def _dot(a, b):
    return jnp.dot(a, b, preferred_element_type=F32)


def _dot_hi(a, b):
    return jnp.dot(a, b, preferred_element_type=F32, precision=HI)


def _dot_nt(a, b):
    return lax.dot_general(a, b, (((1,), (1,)), ((), ())), preferred_element_type=F32)


def _dot_nt_hi(a, b):
    return lax.dot_general(a, b, (((1,), (1,)), ((), ())), preferred_element_type=F32, precision=HI)


def _pad_cols(w, n):
    return jnp.pad(w, ((0, 0), (0, n - w.shape[1])))


def _norm_matmul_kernel(x_ref, g_ref, w_ref, o_ref, xn_ref):
    @pl.when(pl.program_id(1) == 0)
    def _():
        xn_ref[...] = _rms(x_ref[...], g_ref[...]).astype(xn_ref.dtype)

    o_ref[...] = _dot(xn_ref[...], w_ref[...]).astype(o_ref.dtype)


def _norm_matmul_hi_kernel(x_ref, g_ref, w_ref, o_ref):
    o_ref[...] = _dot_hi(_rms(x_ref[...], g_ref[...]), w_ref[...])


def norm_matmul(x, g, w, *, tm=1024, tn=1024, out_dtype=F32):
    T, D = x.shape
    N = w.shape[1]
    tm, tn = min(tm, T), min(tn, N)
    assert T % tm == 0 and N % tn == 0
    return pl.pallas_call(
        _norm_matmul_kernel,
        grid=(T // tm, N // tn),
        in_specs=[pl.BlockSpec((tm, D), lambda i, j: (i, 0)),
                  pl.BlockSpec((1, D), lambda i, j: (0, 0)),
                  pl.BlockSpec((D, tn), lambda i, j: (0, j))],
        out_specs=pl.BlockSpec((tm, tn), lambda i, j: (i, j)),
        out_shape=jax.ShapeDtypeStruct((T, N), out_dtype),
        scratch_shapes=[pltpu.VMEM((tm, D), BF16)],
        compiler_params=_cparams(("parallel", "arbitrary")),
        name="norm_matmul",
    )(x, g.reshape(1, D), w)


def norm_matmul_hi(x, g, w, *, tm=512):
    T, D = x.shape
    N = w.shape[1]
    tm = min(tm, T)
    assert T % tm == 0
    return pl.pallas_call(
        _norm_matmul_hi_kernel,
        grid=(T // tm,),
        in_specs=[pl.BlockSpec((tm, D), lambda i: (i, 0)),
                  pl.BlockSpec((1, D), lambda i: (0, 0)),
                  pl.BlockSpec((D, N), lambda i: (0, 0))],
        out_specs=pl.BlockSpec((tm, N), lambda i: (i, 0)),
        out_shape=jax.ShapeDtypeStruct((T, N), F32),
        compiler_params=_cparams(("parallel",)),
        name="norm_matmul_hi",
    )(x, g.reshape(1, D), w)


def _matmul_res_kernel(a_ref, w_ref, r_ref, o_ref):
    o_ref[...] = r_ref[...] + _dot(a_ref[...], w_ref[...])


def matmul_res(a, w, res, *, tm=512):
    T, K = a.shape
    N = w.shape[1]
    tm = min(tm, T)
    assert T % tm == 0
    return pl.pallas_call(
        _matmul_res_kernel,
        grid=(T // tm,),
        in_specs=[pl.BlockSpec((tm, K), lambda i: (i, 0)),
                  pl.BlockSpec((K, N), lambda i: (0, 0)),
                  pl.BlockSpec((tm, N), lambda i: (i, 0))],
        out_specs=pl.BlockSpec((tm, N), lambda i: (i, 0)),
        out_shape=jax.ShapeDtypeStruct((T, N), F32),
        compiler_params=_cparams(("parallel",)),
        name="matmul_res",
    )(a, w, res)


def _ple_kernel(h_ref, p_ref, g_ref, wg_ref, wp_ref, gf_ref, o_ref, *maybe_final):
    h = h_ref[...]
    xn = _rms(h, g_ref[...]).astype(BF16)
    gate = _sigmoid(_dot(xn, wg_ref[...]))
    out = h + gate * _dot(p_ref[...].astype(BF16), wp_ref[...])
    o_ref[...] = out
    if maybe_final:
        maybe_final[0][...] = _rms(out, gf_ref[...])


def ple(h, p, g, w_gate, w_proj, g_final=None, *, tm=512):
    T, D = h.shape
    P = p.shape[1]
    tm = min(tm, T)
    assert T % tm == 0
    final = g_final is not None
    gf = (g_final if final else g).reshape(1, D)
    row = pl.BlockSpec((tm, D), lambda i: (i, 0))
    vec = pl.BlockSpec((1, D), lambda i: (0, 0))
    out_shape = jax.ShapeDtypeStruct((T, D), F32)
    return pl.pallas_call(
        _ple_kernel,
        grid=(T // tm,),
        in_specs=[row, pl.BlockSpec((tm, P), lambda i: (i, 0)), vec,
                  pl.BlockSpec((D, D), lambda i: (0, 0)),
                  pl.BlockSpec((P, D), lambda i: (0, 0)), vec],
        out_specs=(row, row) if final else row,
        out_shape=(out_shape, out_shape) if final else out_shape,
        compiler_params=_cparams(("parallel",)),
        name="ple",
    )(h, p, g.reshape(1, D), w_gate, w_proj, gf)


def _route(logits):
    lane = lax.broadcasted_iota(jnp.int32, logits.shape, 1)
    big = jnp.int32(1 << 20)
    is_g = (lane >= MOE_E) & (lane < MOE_E + MOE_GROUPS)
    glog = jnp.where(is_g, logits, NEG)
    gmax = jnp.max(glog, axis=-1, keepdims=True)
    gsel = jnp.min(jnp.where(is_g & (glog == gmax), lane, big), axis=-1, keepdims=True) - MOE_E
    gden = jnp.sum(jnp.where(is_g, jnp.exp(glog - gmax), 0.0), axis=-1, keepdims=True)
    gprob = 1.0 / gden
    in_g = (lane < MOE_E) & ((lane // MOE_EPG) == gsel)
    e1 = jnp.where(in_g, logits, NEG)
    v1 = jnp.max(e1, axis=-1, keepdims=True)
    i1 = jnp.min(jnp.where(in_g & (e1 == v1), lane, big), axis=-1, keepdims=True)
    in_g2 = in_g & (lane != i1)
    e2 = jnp.where(in_g2, logits, NEG)
    v2 = jnp.max(e2, axis=-1, keepdims=True)
    i2 = jnp.min(jnp.where(in_g2 & (e2 == v2), lane, big), axis=-1, keepdims=True)
    t = jnp.exp(v2 - v1)
    w1 = gprob / (1.0 + t)
    w2 = gprob * t / (1.0 + t)
    return jnp.where(lane == i1, w1, jnp.where(lane == i2, w2, 0.0))


def _moe_dense_kernel(h_ref, g_ref, wr_ref, br_ref, wg_ref, wu_ref, wd_ref, o_ref, xn_ref, comb_ref, acc_ref):
    e = pl.program_id(1)

    @pl.when(e == 0)
    def _():
        xn = _rms(h_ref[...], g_ref[...])
        comb_ref[...] = _route(_dot_hi(xn, wr_ref[...]) + br_ref[...])
        xn_ref[...] = xn.astype(BF16)
        acc_ref[...] = jnp.zeros_like(acc_ref)

    xn = xn_ref[...]
    hg = _dot(xn, wg_ref[0])
    hu = _dot(xn, wu_ref[0])
    comb = comb_ref[...]
    lane = lax.broadcasted_iota(jnp.int32, comb.shape, 1)
    c = jnp.sum(jnp.where(lane == e, comb, 0.0), axis=-1, keepdims=True)
    hid = (_silu(hg) * hu * c).astype(BF16)
    acc_ref[...] += _dot(hid, wd_ref[0])

    @pl.when(e == pl.num_programs(1) - 1)
    def _():
        o_ref[...] = h_ref[...] + acc_ref[...]


def moe_dense(h, g, w_router, b_router, w_gate, w_up, w_down, *, tm=1024):
    T, D = h.shape
    E, _, Fd = w_gate.shape
    tm = min(tm, T)
    assert T % tm == 0
    row = pl.BlockSpec((tm, D), lambda i, e: (i, 0))
    return pl.pallas_call(
        _moe_dense_kernel,
        grid=(T // tm, E),
        in_specs=[row, pl.BlockSpec((1, D), lambda i, e: (0, 0)),
                  pl.BlockSpec((D, LANES), lambda i, e: (0, 0)),
                  pl.BlockSpec((1, LANES), lambda i, e: (0, 0)),
                  pl.BlockSpec((1, D, Fd), lambda i, e: (e, 0, 0)),
                  pl.BlockSpec((1, D, Fd), lambda i, e: (e, 0, 0)),
                  pl.BlockSpec((1, Fd, D), lambda i, e: (e, 0, 0))],
        out_specs=row,
        out_shape=jax.ShapeDtypeStruct((T, D), F32),
        scratch_shapes=[pltpu.VMEM((tm, D), BF16), pltpu.VMEM((tm, LANES), F32), pltpu.VMEM((tm, D), F32)],
        compiler_params=_cparams(("parallel", "arbitrary")),
        name="moe_dense",
    )(h, g.reshape(1, D), w_router, b_router, w_gate, w_up, w_down)


def _conv_tile(xpad_ref, x, cw_ref, cb_ref, first):
    tl = x.shape[0]

    @pl.when(first)
    def _():
        xpad_ref[0:SUBLANES, :] = jnp.zeros((SUBLANES, x.shape[1]), F32)

    xpad_ref[SUBLANES:SUBLANES + tl, :] = x
    out = cb_ref[...] + x * cw_ref[CONV_W - 1:CONV_W, :]
    for k in range(CONV_W - 1):
        lo = SUBLANES - (CONV_W - 1) + k
        out = out + xpad_ref[lo:lo + tl, :] * cw_ref[k:k + 1, :]
    tail = xpad_ref[tl:tl + SUBLANES, :]
    xpad_ref[0:SUBLANES, :] = tail
    return out, tail[SUBLANES - (CONV_W - 1):, :]


def _lru_gates(xc, wa_ref, ba_ref, wi_ref, bi_ref, lam_ref):
    xcb = xc.astype(BF16)
    nb = wa_ref.shape[0]
    cw = wa_ref.shape[1]
    ra = jnp.concatenate([_dot(xcb[:, c * cw:(c + 1) * cw], wa_ref[c]) for c in range(nb)], axis=1)
    ia = jnp.concatenate([_dot(xcb[:, c * cw:(c + 1) * cw], wi_ref[c]) for c in range(nb)], axis=1)
    r = _sigmoid(ra + ba_ref[...])
    ig = _sigmoid(ia + bi_ref[...])
    log_a = (-LRU_C) * r * _softplus(-lam_ref[...])
    a = jnp.exp(log_a)
    b = jnp.sqrt(-jnp.tanh(log_a) * (a * a + 1.0)) * (ig * xc)
    return a, b


def _lru_seq_kernel(xb_ref, yb_ref, cw_ref, cb_ref, wa_ref, ba_ref, wi_ref, bi_ref, lam_ref,
                    o_ref, buf_ref, hl_ref, xpad_ref, a_ref, b_ref, hc_ref):
    t = pl.program_id(1)
    tl, W = xb_ref.shape[1], xb_ref.shape[2]
    xc, tail = _conv_tile(xpad_ref, xb_ref[0], cw_ref, cb_ref, t == 0)
    a, b = _lru_gates(xc, wa_ref, ba_ref, wi_ref, bi_ref, lam_ref)
    a_ref[...] = a
    b_ref[...] = b

    @pl.when(t == 0)
    def _():
        hc_ref[...] = jnp.zeros_like(hc_ref)

    row = lax.broadcasted_iota(jnp.int32, (SUBLANES, W), 0)

    def body(i, hc):
        r0 = pl.multiple_of(i * SUBLANES, SUBLANES)
        av = a_ref[pl.ds(r0, SUBLANES), :]
        bv = b_ref[pl.ds(r0, SUBLANES), :]
        for s in (1, 2, 4):
            keep = row >= s
            a_sh = pltpu.roll(av, s, 0)
            b_sh = pltpu.roll(bv, s, 0)
            bv = jnp.where(keep, av * b_sh + bv, bv)
            av = jnp.where(keep, av * a_sh, av)
        h = av * hc + bv
        b_ref[pl.ds(r0, SUBLANES), :] = h
        return jnp.broadcast_to(h[SUBLANES - 1:SUBLANES, :], (SUBLANES, W))

    hc = lax.fori_loop(0, tl // SUBLANES, body, hc_ref[...])
    hc_ref[...] = hc
    o_ref[0] = (b_ref[...] * _gelu_tanh(yb_ref[0])).astype(o_ref.dtype)
    buf_ref[0] = tail
    hl_ref[0] = hc[0:1, :]


def lru_seq(xy, conv_w, conv_b, wa_bd, b_a, wi_bd, b_i, lam, *, tl=256):
    B, L, W2 = xy.shape
    W = W2 // 2
    tl = min(tl, L)
    assert L % tl == 0 and tl % SUBLANES == 0
    vec = pl.BlockSpec((1, W), lambda b, t: (0, 0))
    wbd = pl.BlockSpec(wa_bd.shape, lambda b, t: (0, 0, 0))
    return pl.pallas_call(
        _lru_seq_kernel,
        grid=(B, L // tl),
        in_specs=[pl.BlockSpec((1, tl, W), lambda b, t: (b, t, 0)),
                  pl.BlockSpec((1, tl, W), lambda b, t: (b, t, 1)),
                  pl.BlockSpec((CONV_W, W), lambda b, t: (0, 0)), vec, wbd, vec, wbd, vec, vec],
        out_specs=(pl.BlockSpec((1, tl, W), lambda b, t: (b, t, 0)),
                   pl.BlockSpec((1, CONV_W - 1, W), lambda b, t: (b, 0, 0)),
                   pl.BlockSpec((1, 1, W), lambda b, t: (b, 0, 0))),
        out_shape=(jax.ShapeDtypeStruct((B, L, W), BF16),
                   jax.ShapeDtypeStruct((B, CONV_W - 1, W), F32),
                   jax.ShapeDtypeStruct((B, 1, W), F32)),
        scratch_shapes=[pltpu.VMEM((tl + SUBLANES, W), F32), pltpu.VMEM((tl, W), F32),
                        pltpu.VMEM((tl, W), F32), pltpu.VMEM((SUBLANES, W), F32)],
        compiler_params=_cparams(("parallel", "arbitrary")),
        name="lru_seq",
    )(xy, xy, conv_w, conv_b.reshape(1, W), wa_bd, b_a.reshape(1, W), wi_bd, b_i.reshape(1, W), lam.reshape(1, W))


def _lru_step_kernel(xb_ref, yb_ref, buf_ref, h0_ref, cw_ref, cb_ref, wa_ref, ba_ref, wi_ref, bi_ref, lam_ref,
                     o_ref, nbuf_ref, hl_ref):
    x = xb_ref[...]
    xc = cb_ref[...] + x * cw_ref[CONV_W - 1:CONV_W, :]
    for k in range(CONV_W - 1):
        xc = xc + buf_ref[k] * cw_ref[k:k + 1, :]
    a, b = _lru_gates(xc, wa_ref, ba_ref, wi_ref, bi_ref, lam_ref)
    h = a * h0_ref[...] + b
    o_ref[...] = (h * _gelu_tanh(yb_ref[...])).astype(o_ref.dtype)
    hl_ref[...] = h
    for k in range(CONV_W - 2):
        nbuf_ref[k] = buf_ref[k + 1]
    nbuf_ref[CONV_W - 2] = x


def lru_step(xy, buf_t, h0, conv_w, conv_b, wa_bd, b_a, wi_bd, b_i, lam):
    B, W2 = xy.shape
    W = W2 // 2
    vec = pl.BlockSpec((1, W), lambda i: (0, 0))
    wbd = pl.BlockSpec(wa_bd.shape, lambda i: (0, 0, 0))
    mat = pl.BlockSpec((B, W), lambda i: (0, 0))
    cube = pl.BlockSpec((CONV_W - 1, B, W), lambda i: (0, 0, 0))
    return pl.pallas_call(
        _lru_step_kernel,
        grid=(1,),
        in_specs=[mat, pl.BlockSpec((B, W), lambda i: (0, 1)), cube, mat,
                  pl.BlockSpec((CONV_W, W), lambda i: (0, 0)), vec, wbd, vec, wbd, vec, vec],
        out_specs=(mat, cube, mat),
        out_shape=(jax.ShapeDtypeStruct((B, W), BF16),
                   jax.ShapeDtypeStruct((CONV_W - 1, B, W), F32),
                   jax.ShapeDtypeStruct((B, W), F32)),
        compiler_params=_cparams(("arbitrary",)),
        name="lru_step",
    )(xy, xy, buf_t, h0, conv_w, conv_b.reshape(1, W), wa_bd, b_a.reshape(1, W), wi_bd, b_i.reshape(1, W),
      lam.reshape(1, W))


def _group_norm_gate(y, z, ng, n_groups):
    y = y * _silu(z)
    gw = y.shape[1] // n_groups
    outs = []
    for g in range(n_groups):
        yg = y[:, g * gw:(g + 1) * gw]
        ms = jnp.mean(yg * yg, axis=-1, keepdims=True)
        outs.append(yg * lax.rsqrt(ms + EPS))
    return jnp.concatenate(outs, axis=1) * ng


def _ssd_seq_kernel(z_ref, xbc_ref, dt_ref, cw_ref, cb_ref, dtb_ref, alog_ref, dsk_ref, ng_ref, ex_ref,
                    o_ref, buf_ref, st_ref, xpad_ref, s_ref):
    t = pl.program_id(1)
    Q = xbc_ref.shape[1]
    DI = z_ref.shape[2]
    GN = SSM_G * SSM_N
    RP = DI // SSM_G
    xc, tail = _conv_tile(xpad_ref, xbc_ref[0], cw_ref, cb_ref, t == 0)
    xc = _silu(xc)
    xs = xc[:, :DI]
    bm = xc[:, DI:DI + GN]
    cm = xc[:, DI + GN:]

    @pl.when(t == 0)
    def _():
        s_ref[...] = jnp.zeros_like(s_ref)

    dt = _softplus(dt_ref[0] + dtb_ref[...])
    a_neg = -jnp.exp(alog_ref[...])
    ri = lax.broadcasted_iota(jnp.int32, (Q, Q), 0)
    ci = lax.broadcasted_iota(jnp.int32, (Q, Q), 1)
    tri = ci <= ri
    acum = _dot_hi(tri.astype(F32), dt * a_neg)
    acum_t = acum.T
    ex = ex_ref[...]
    dt_e = _dot_hi(dt, ex)
    acum_e = _dot_hi(acum, ex)
    last_e = acum_e[Q - 1:Q, :]
    xdt = xs * dt_e
    xdtw = (xdt * jnp.exp(last_e - acum_e)).astype(BF16)
    xdt = xdt.astype(BF16)
    eacum = jnp.exp(acum_e)
    edec = jnp.exp(last_e)
    lane = lax.broadcasted_iota(jnp.int32, (Q, LANES), 1)
    lo = lane < SSM_P
    ys = []
    for g in range(SSM_G):
        cg = cm[:, g * SSM_N:(g + 1) * SSM_N].astype(BF16)
        bg32 = bm[:, g * SSM_N:(g + 1) * SSM_N]
        bg = bg32.astype(BF16)
        cb = _dot_nt(cg, bg)
        st = s_ref[g]
        yoff = _dot(cg, st.astype(BF16)) * eacum[:, g * RP:(g + 1) * RP]
        for pr in range(RP // LANES):
            ms = []
            for k in range(LANES // SSM_P):
                hd = (g * RP + pr * LANES) // SSM_P + k
                seg = acum[:, hd:hd + 1] - acum_t[hd:hd + 1, :]
                ms.append((cb * jnp.exp(jnp.where(tri, seg, NEG))).astype(BF16))
            c0 = g * RP + pr * LANES
            xp = xdt[:, c0:c0 + LANES]
            zero = jnp.zeros_like(xp)
            rhs = jnp.concatenate([jnp.where(lo, xp, zero), jnp.where(lo, zero, xp)], axis=0)
            ydiag = _dot(jnp.concatenate(ms, axis=1), rhs)
            ys.append(ydiag + yoff[:, pr * LANES:(pr + 1) * LANES])
        new = st * edec[:, g * RP:(g + 1) * RP] + _dot(bg32.T.astype(BF16), xdtw[:, g * RP:(g + 1) * RP])
        s_ref[g] = new
    y = jnp.concatenate(ys, axis=1) + xs * dsk_ref[...]
    o_ref[0] = _group_norm_gate(y, z_ref[0], ng_ref[...], SSM_G).astype(o_ref.dtype)
    buf_ref[0] = tail

    @pl.when(t == pl.num_programs(1) - 1)
    def _():
        hpg = RP // SSM_P
        for g in range(SSM_G):
            st_ref[0, g * hpg:(g + 1) * hpg] = s_ref[g].T.reshape(hpg, SSM_P, SSM_N)


def ssd_seq(z, xbc, dt, conv_w, conv_b, dt_bias, a_log, d_lanes, norm_g, expand, *, q=128):
    B, L, DI = z.shape
    C = xbc.shape[2]
    H = DI // SSM_P
    q = min(q, L)
    assert L % q == 0
    vecd = pl.BlockSpec((1, DI), lambda b, t: (0, 0))
    vecl = pl.BlockSpec((1, LANES), lambda b, t: (0, 0))
    return pl.pallas_call(
        _ssd_seq_kernel,
        grid=(B, L // q),
        in_specs=[pl.BlockSpec((1, q, DI), lambda b, t: (b, t, 0)),
                  pl.BlockSpec((1, q, C), lambda b, t: (b, t, 0)),
                  pl.BlockSpec((1, q, LANES), lambda b, t: (b, t, 0)),
                  pl.BlockSpec((CONV_W, C), lambda b, t: (0, 0)),
                  pl.BlockSpec((1, C), lambda b, t: (0, 0)),
                  vecl, vecl, vecd, vecd,
                  pl.BlockSpec((LANES, DI), lambda b, t: (0, 0))],
        out_specs=(pl.BlockSpec((1, q, DI), lambda b, t: (b, t, 0)),
                   pl.BlockSpec((1, CONV_W - 1, C), lambda b, t: (b, 0, 0)),
                   pl.BlockSpec((1, H, SSM_P, SSM_N), lambda b, t: (b, 0, 0, 0))),
        out_shape=(jax.ShapeDtypeStruct((B, L, DI), BF16),
                   jax.ShapeDtypeStruct((B, CONV_W - 1, C), F32),
                   jax.ShapeDtypeStruct((B, H, SSM_P, SSM_N), F32)),
        scratch_shapes=[pltpu.VMEM((q + SUBLANES, C), F32),
                        pltpu.VMEM((SSM_G, SSM_N, DI // SSM_G), F32)],
        compiler_params=_cparams(("parallel", "arbitrary")),
        name="ssd_seq",
    )(z, xbc, dt, conv_w, conv_b.reshape(1, C), dt_bias, a_log, d_lanes, norm_g.reshape(1, DI), expand)


def _ssd_step_kernel(z_ref, xbc_ref, dt_ref, buf_ref, h0_ref, cw_ref, cb_ref, dtb_ref, alog_ref, dsk_ref, ng_ref,
                     ex_ref, o_ref, nbuf_ref, hn_ref):
    DI = z_ref.shape[2]
    GN = SSM_G * SSM_N
    hpg = DI // SSM_P // SSM_G
    x = xbc_ref[0]
    buf = buf_ref[0]
    xc = cb_ref[...] + x * cw_ref[CONV_W - 1:CONV_W, :]
    for k in range(CONV_W - 1):
        xc = xc + buf[k:k + 1, :] * cw_ref[k:k + 1, :]
    nbuf_ref[0, 0:CONV_W - 2, :] = buf[1:CONV_W - 1, :]
    nbuf_ref[0, CONV_W - 2:CONV_W - 1, :] = x
    xc = _silu(xc)
    xs = xc[:, :DI]
    dt = _softplus(dt_ref[0] + dtb_ref[...])
    dta = dt * (-jnp.exp(alog_ref[...]))
    ex = ex_ref[...]
    dec = jnp.exp(dta)
    dt_e = _dot_hi(jnp.broadcast_to(dt, (SUBLANES, LANES)), ex)[0:1, :]
    xdt = xs * dt_e
    RP = hpg * SSM_P
    eye = (lax.broadcasted_iota(jnp.int32, (RP, RP), 0) == lax.broadcasted_iota(jnp.int32, (RP, RP), 1))
    ys = []
    for g in range(SSM_G):
        brow = xc[:, DI + g * SSM_N:DI + (g + 1) * SSM_N]
        crow = xc[:, DI + GN + g * SSM_N:DI + GN + (g + 1) * SSM_N]
        xg = xdt[:, g * RP:(g + 1) * RP]
        xcol = jnp.sum(jnp.where(eye, jnp.broadcast_to(xg, (RP, RP)), 0.0), axis=-1, keepdims=True)
        news = []
        for r in range(hpg):
            hd = g * hpg + r
            new = h0_ref[0, hd] * dec[:, hd:hd + 1] + xcol[r * SSM_P:(r + 1) * SSM_P, :] * brow
            hn_ref[0, hd] = new
            news.append(new)
        new_g = jnp.concatenate(news, axis=0).astype(BF16)
        ys.append(_dot_nt(jnp.broadcast_to(crow, (SUBLANES, SSM_N)).astype(BF16), new_g)[0:1, :])
    y = jnp.concatenate(ys, axis=1) + xs * dsk_ref[...]
    o_ref[0] = _group_norm_gate(y, z_ref[0], ng_ref[...], SSM_G).astype(o_ref.dtype)


def ssd_step(z, xbc, dt, buf, h0, conv_w, conv_b, dt_bias, a_log, d_lanes, norm_g, expand):
    B, _, DI = z.shape
    C = xbc.shape[2]
    H = DI // SSM_P
    vecd = pl.BlockSpec((1, DI), lambda b: (0, 0))
    vecl = pl.BlockSpec((1, LANES), lambda b: (0, 0))
    st = pl.BlockSpec((1, H, SSM_P, SSM_N), lambda b: (b, 0, 0, 0))
    return pl.pallas_call(
        _ssd_step_kernel,
        grid=(B,),
        in_specs=[pl.BlockSpec((1, 1, DI), lambda b: (b, 0, 0)),
                  pl.BlockSpec((1, 1, C), lambda b: (b, 0, 0)),
                  pl.BlockSpec((1, 1, LANES), lambda b: (b, 0, 0)),
                  pl.BlockSpec((1, CONV_W - 1, C), lambda b: (b, 0, 0)), st,
                  pl.BlockSpec((CONV_W, C), lambda b: (0, 0)),
                  pl.BlockSpec((1, C), lambda b: (0, 0)),
                  vecl, vecl, vecd, vecd,
                  pl.BlockSpec((LANES, DI), lambda b: (0, 0))],
        out_specs=(pl.BlockSpec((1, 1, DI), lambda b: (b, 0, 0)),
                   pl.BlockSpec((1, CONV_W - 1, C), lambda b: (b, 0, 0)), st),
        out_shape=(jax.ShapeDtypeStruct((B, 1, DI), BF16),
                   jax.ShapeDtypeStruct((B, CONV_W - 1, C), F32),
                   jax.ShapeDtypeStruct((B, H, SSM_P, SSM_N), F32)),
        compiler_params=_cparams(("parallel",)),
        name="ssd_step",
    )(z, xbc, dt, buf, h0, conv_w, conv_b.reshape(1, C), dt_bias, a_log, d_lanes, norm_g.reshape(1, DI), expand)


def _split_dot(x, m):
    hi = x.astype(BF16)
    lo = (x - hi.astype(F32)).astype(BF16)
    return _dot(hi, m) + _dot(lo, m)


LOG2E = 1.4426950408889634
N_PIECES = 3
PAIR = LANES // ATT_HD


def _fox_prep_kernel(q_ref, k_ref, v_ref, fl_ref, bf_ref, qg_ref, kg_ref, hs_ref, he_ref, pl_ref,
                     qo_ref, ko_ref, lf_ref, kx_ref, vt_ref, carry_ref):
    t = pl.program_id(1)
    tl = q_ref.shape[1]
    D = q_ref.shape[2]

    def head_norm(x, g):
        ss = _split_dot(x * x, hs_ref[...])
        inv = lax.rsqrt(ss * (1.0 / ATT_HD) + EPS)
        return x * _split_dot(inv, he_ref[...]) * g

    qo_ref[0] = (head_norm(q_ref[0], qg_ref[...]) * (ATT_HD ** -0.5 * LOG2E)).astype(qo_ref.dtype)
    kn = head_norm(k_ref[0], kg_ref[...])
    ko_ref[0] = kn
    vt_ref[0] = v_ref[0].T.astype(BF16)
    z = fl_ref[0] + bf_ref[...]
    logf = jnp.minimum(z, 0.0) - jnp.log1p(jnp.exp(-jnp.abs(z)))
    lf_ref[0] = logf

    @pl.when(t == 0)
    def _():
        carry_ref[...] = jnp.zeros_like(carry_ref)

    ri = lax.broadcasted_iota(jnp.int32, (tl, tl), 0)
    ci = lax.broadcasted_iota(jnp.int32, (tl, tl), 1)
    c = _dot_hi((ci <= ri).astype(F32), logf) + carry_ref[0:1, :]
    carry_ref[...] = jnp.broadcast_to(c[tl - 1:tl, :], carry_ref.shape)
    rest = c * (-LOG2E)
    extra = jnp.zeros((tl, D), F32)
    for j in range(N_PIECES):
        piece = rest.astype(BF16)
        rest = rest - piece.astype(F32)
        extra = extra + _dot(piece, pl_ref[j])
    knb = kn.astype(BF16)
    extra = extra.astype(BF16)
    kx_ref[0] = jnp.concatenate(
        [x[:, p * LANES:(p + 1) * LANES] for p in range(D // LANES) for x in (knb, extra)], axis=1)


def fox_prep(q, k, v, fl, b_f, q_g, k_g, head_sum, head_expand, place, *, tl=256):
    B, L, D = q.shape
    tl = min(tl, L)
    assert L % tl == 0
    row = pl.BlockSpec((1, tl, D), lambda b, t: (b, t, 0))
    nar = pl.BlockSpec((1, tl, LANES), lambda b, t: (b, t, 0))
    vecd = pl.BlockSpec((1, D), lambda b, t: (0, 0))
    return pl.pallas_call(
        _fox_prep_kernel,
        grid=(B, L // tl),
        in_specs=[row, row, row, nar, pl.BlockSpec((1, LANES), lambda b, t: (0, 0)), vecd, vecd,
                  pl.BlockSpec((D, LANES), lambda b, t: (0, 0)),
                  pl.BlockSpec((LANES, D), lambda b, t: (0, 0)),
                  pl.BlockSpec((N_PIECES, LANES, D), lambda b, t: (0, 0, 0))],
        out_specs=(row, row, nar, pl.BlockSpec((1, tl, 2 * D), lambda b, t: (b, t, 0)),
                   pl.BlockSpec((1, D, tl), lambda b, t: (b, 0, t))),
        out_shape=(jax.ShapeDtypeStruct((B, L, D), BF16),
                   jax.ShapeDtypeStruct((B, L, D), F32),
                   jax.ShapeDtypeStruct((B, L, LANES), F32),
                   jax.ShapeDtypeStruct((B, L, 2 * D), BF16),
                   jax.ShapeDtypeStruct((B, D, L), BF16)),
        scratch_shapes=[pltpu.VMEM((SUBLANES, LANES), F32)],
        compiler_params=_cparams(("parallel", "arbitrary")),
        name="fox_prep",
    )(q, k, v, fl, b_f, q_g, k_g, head_sum, head_expand, place)


def _fox_attn_kernel(q_ref, kx_ref, vt_ref, g_ref, o_ref, qx_ref, m_ref, l_ref, acc_ref):
    qi = pl.program_id(2)
    tq = q_ref.shape[1]
    q = q_ref[0].astype(F32)
    lane = lax.broadcasted_iota(jnp.int32, (tq, LANES), 1)
    for k in range(PAIR):
        mine = (lane >= k * ATT_HD) & (lane < (k + 1) * ATT_HD)
        pick = (lane >= k * N_PIECES) & (lane < (k + 1) * N_PIECES)
        qx_ref[k] = jnp.concatenate([jnp.where(mine, q, 0.0), jnp.where(pick, 1.0, 0.0)], axis=1).astype(BF16)
    m_ref[...] = jnp.full_like(m_ref, NEG)
    l_ref[...] = jnp.zeros_like(l_ref)
    acc_ref[...] = jnp.zeros_like(acc_ref)

    def block(j, diagonal):
        r0 = pl.multiple_of(j * tq, tq)
        kb = kx_ref[0, pl.ds(r0, tq), :]
        vt = vt_ref[0, :, pl.ds(r0, tq)]
        for k in range(PAIR):
            s = _dot_nt(kb, qx_ref[k])
            if diagonal:
                rows = lax.broadcasted_iota(jnp.int32, (tq, tq), 0)
                cols = lax.broadcasted_iota(jnp.int32, (tq, tq), 1)
                s = jnp.where(rows <= cols, s, NEG)
            m_old = m_ref[k]
            m_new = jnp.maximum(m_old, jnp.max(s, axis=0, keepdims=True))
            alpha = jnp.exp2(m_old - m_new)
            p = jnp.exp2(s - m_new)
            l_ref[k] = l_ref[k] * alpha + jnp.sum(p, axis=0, keepdims=True)
            m_ref[k] = m_new
            acc_ref[k] = acc_ref[k] * alpha + _dot(vt[k * ATT_HD:(k + 1) * ATT_HD, :], p.astype(BF16))

    def body(j, carry):
        block(j, False)
        return carry

    lax.fori_loop(0, qi, body, 0)
    block(qi, True)
    o = jnp.concatenate([acc_ref[k] / l_ref[k] for k in range(PAIR)], axis=0)
    o_ref[0] = (o.T * _sigmoid(g_ref[0])).astype(o_ref.dtype)


def fox_attn(q, kx, vt, g, *, tq=512):
    B, L, D = q.shape
    tq = min(tq, L)
    assert L % tq == 0
    qs = pl.BlockSpec((1, tq, LANES), lambda b, p, i: (b, i, p))
    return pl.pallas_call(
        _fox_attn_kernel,
        grid=(B, D // LANES, L // tq),
        in_specs=[qs,
                  pl.BlockSpec((1, L, 2 * LANES), lambda b, p, i: (b, 0, p)),
                  pl.BlockSpec((1, LANES, L), lambda b, p, i: (b, p, 0)),
                  qs],
        out_specs=qs,
        out_shape=jax.ShapeDtypeStruct((B, L, D), BF16),
        scratch_shapes=[pltpu.VMEM((PAIR, tq, 2 * LANES), BF16), pltpu.VMEM((PAIR, 1, tq), F32),
                        pltpu.VMEM((PAIR, 1, tq), F32), pltpu.VMEM((PAIR, ATT_HD, tq), F32)],
        compiler_params=_cparams(("parallel", "parallel", "arbitrary")),
        name="fox_attn",
    )(q, kx, vt, g)


def _lanes_to_sublanes(row):
    n = row.shape[1]
    eye = lax.broadcasted_iota(jnp.int32, (n, n), 0) == lax.broadcasted_iota(jnp.int32, (n, n), 1)
    return jnp.sum(jnp.where(eye, jnp.broadcast_to(row, (n, n)), 0.0), axis=1, keepdims=True)


def _sublanes_to_lanes(col):
    n = col.shape[0]
    eye = lax.broadcasted_iota(jnp.int32, (n, n), 0) == lax.broadcasted_iota(jnp.int32, (n, n), 1)
    return jnp.sum(jnp.where(eye, jnp.broadcast_to(col, (n, n)), 0.0), axis=0, keepdims=True)


def _fox_decode_kernel(pt_ref, q_ref, kn_ref, vn_ref, lfn_ref, g_ref, *rest, npg):
    k_refs = rest[:npg]
    v_refs = rest[npg:2 * npg]
    lf_refs = rest[2 * npg:3 * npg]
    o_ref = rest[3 * npg]
    qp_ref, m_ref, l_ref, acc_ref, coff_ref = rest[3 * npg + 1:]
    s_id = pl.program_id(1)
    ps, H, hd = k_refs[0].shape[1:]

    @pl.when(s_id == 0)
    def _():
        qp_ref[...] = jnp.zeros_like(qp_ref)
        qp_ref[0:H, :] = q_ref[0].astype(BF16)
        m_ref[...] = jnp.full_like(m_ref, NEG)
        l_ref[...] = jnp.zeros_like(l_ref)
        acc_ref[...] = jnp.zeros_like(acc_ref)
        coff_ref[...] = jnp.zeros_like(coff_ref)

    ri = lax.broadcasted_iota(jnp.int32, (ps, ps), 0)
    ci = lax.broadcasted_iota(jnp.int32, (ps, ps), 1)
    tri = (ci <= ri).astype(F32)
    own = (lax.broadcasted_iota(jnp.int32, (ps, H, H), 1)
           == lax.broadcasted_iota(jnp.int32, (ps, H, H), 2)).reshape(ps * H, H)

    def absorb(s, p_rows, vals):
        m_old = m_ref[...]
        m_new = jnp.maximum(m_old, jnp.max(s, axis=0, keepdims=True))
        alpha = jnp.exp2(m_old - m_new)
        p = jnp.exp2(s - m_new)
        l_ref[...] = l_ref[...] * alpha + jnp.sum(p, axis=0, keepdims=True)
        m_ref[...] = m_new
        pv = p_rows(p) * vals
        acc_ref[...] = acc_ref[...] * _lanes_to_sublanes(alpha) + jnp.sum(pv.reshape(-1, H, hd), axis=0)

    for i in range(npg):
        k2 = k_refs[i][0].reshape(ps * H, hd).astype(BF16)
        qk = _dot_nt(k2, qp_ref[...])[:, 0:H]
        c = _dot_hi(tri, lf_refs[i][0]) + coff_ref[...]
        coff_ref[...] = c[ps - 1:ps, :]
        c_rows = jnp.broadcast_to((c * LOG2E)[:, None, :], (ps, H, H)).reshape(ps * H, H)
        s = jnp.where(own, qk - c_rows, NEG)
        absorb(s, lambda p: jnp.sum(p, axis=1, keepdims=True), v_refs[i][0].reshape(ps * H, hd))

    @pl.when(s_id == pl.num_programs(1) - 1)
    def _():
        qk_new = jnp.sum(q_ref[0] * kn_ref[0], axis=1, keepdims=True)
        c_q = (coff_ref[...] + lfn_ref[0][:, 0:H]) * LOG2E
        absorb(_sublanes_to_lanes(qk_new) - c_q, _lanes_to_sublanes, vn_ref[0])
        o = acc_ref[...] / _lanes_to_sublanes(l_ref[...])
        o_ref[0] = (o * _sigmoid(g_ref[0])).astype(o_ref.dtype)


def fox_decode(q, k_new, v_new, lf_new, g, cache_k, cache_v, cache_lf, page_table, *, npg=4):
    B, H, hd = q.shape
    n_pages = page_table.shape[1]
    ps = cache_k.shape[1]
    npg = min(npg, n_pages)
    assert n_pages % npg == 0
    steps = n_pages // npg
    row = pl.BlockSpec((1, H, hd), lambda b, s, pt: (b, 0, 0))

    def page(i, *tail):
        return pl.BlockSpec((1, ps) + tail, lambda b, s, pt: (pt[b * n_pages + s * npg + i],) + (0,) * (len(tail) + 1))

    in_specs = ([row, row, row, pl.BlockSpec((1, 1, LANES), lambda b, s, pt: (b, 0, 0)), row]
                + [page(i, H, hd) for i in range(npg)] + [page(i, H, hd) for i in range(npg)]
                + [page(i, H) for i in range(npg)])
    grid_spec = pltpu.PrefetchScalarGridSpec(
        num_scalar_prefetch=1, grid=(B, steps), in_specs=in_specs, out_specs=row,
        scratch_shapes=[pltpu.VMEM((LANES, hd), BF16), pltpu.VMEM((1, H), F32), pltpu.VMEM((1, H), F32),
                        pltpu.VMEM((H, hd), F32), pltpu.VMEM((1, H), F32)])
    return pl.pallas_call(
        functools.partial(_fox_decode_kernel, npg=npg),
        grid_spec=grid_spec,
        out_shape=jax.ShapeDtypeStruct((B, H, hd), BF16),
        compiler_params=_cparams(("parallel", "arbitrary")),
        name="fox_decode",
    )(page_table.reshape(-1), q, k_new, v_new, lf_new, g,
      *([cache_k] * npg), *([cache_v] * npg), *([cache_lf] * npg))


def _block_diag_chunks(w, per_chunk):
    nblk, bw, _ = w.shape
    w = w.reshape(nblk // per_chunk, per_chunk, bw, bw)
    eye = jnp.eye(per_chunk, dtype=w.dtype)
    out = jnp.einsum('cpij,pq->cpiqj', w, eye)
    return out.reshape(nblk // per_chunk, per_chunk * bw, per_chunk * bw)


def _head_expand(n_heads, width):
    r = jnp.arange(LANES)[:, None]
    c = jnp.arange(n_heads * width)[None, :] // width
    return (r == c).astype(F32)


def _pad_lanes(v):
    return jnp.pad(v.reshape(1, -1), ((0, 0), (0, LANES - v.size)))


def _moe_and_ple(h, p_i, i, prm, last):
    h = moe_dense(h, prm['norm_ffn'][i], prm['moe_w_router'][i], prm['moe_b_router'][i],
                  prm['moe_w_gate'][i], prm['moe_w_up'][i], prm['moe_w_down'][i])
    return ple(h, p_i, prm['norm_ple'][i], prm['ple_w_gate'][i], prm['ple_w_proj'][i],
               prm['norm_final'] if last else None)


def _lru_layer_prompt(h, B, L, j, g, prm):
    D = h.shape[1]
    xy = norm_matmul(h, g, prm['lru_w_in'][j])
    gated, buf, hl = lru_seq(xy.reshape(B, L, -1), prm['lru_conv_w'][j], prm['lru_conv_b'][j],
                             prm['lru_wa_bd'][j], prm['lru_b_a'][j], prm['lru_wi_bd'][j], prm['lru_b_i'][j],
                             prm['lru_lambda'][j])
    h = matmul_res(gated.reshape(B * L, -1), prm['lru_w_out'][j], h)
    return h, buf, hl.reshape(B, D)


def _lru_layer_sample(h, buf, h0, j, g, prm):
    xy = norm_matmul(h, g, prm['lru_w_in'][j])
    gated, nbuf, hl = lru_step(xy, jnp.swapaxes(buf, 0, 1), h0, prm['lru_conv_w'][j], prm['lru_conv_b'][j],
                               prm['lru_wa_bd'][j], prm['lru_b_a'][j], prm['lru_wi_bd'][j], prm['lru_b_i'][j],
                               prm['lru_lambda'][j])
    h = matmul_res(gated, prm['lru_w_out'][j], h)
    return h, jnp.swapaxes(nbuf, 0, 1), hl


def _ssd_proj(h, g, j, prm):
    z = norm_matmul(h, g, prm['ssm_w_z'][j])
    xbc = norm_matmul(h, g, prm['ssm_w_xbc'][j])
    dt = norm_matmul_hi(h, g, prm['ssm_w_dt'][j])
    return z, xbc, dt


def _ssd_args(j, prm):
    return (prm['ssm_conv_w'][j], prm['ssm_conv_b'][j], prm['ssm_dt_bias'][j], prm['ssm_a_log'][j],
            prm['ssm_d_lanes'][j], prm['ssm_norm'][j], prm['ssm_expand'])


def _ssd_layer_prompt(h, B, L, j, g, prm):
    z, xbc, dt = _ssd_proj(h, g, j, prm)
    y, buf, st = ssd_seq(z.reshape(B, L, -1), xbc.reshape(B, L, -1), dt.reshape(B, L, -1), *_ssd_args(j, prm))
    return matmul_res(y.reshape(B * L, -1), prm['ssm_w_out'][j], h), buf, st


def _ssd_layer_sample(h, buf, h0, j, g, prm):
    B = h.shape[0]
    z, xbc, dt = _ssd_proj(h, g, j, prm)
    y, nbuf, st = ssd_step(z.reshape(B, 1, -1), xbc.reshape(B, 1, -1), dt.reshape(B, 1, -1), buf, h0,
                           *_ssd_args(j, prm))
    return matmul_res(y.reshape(B, -1), prm['ssm_w_out'][j], h), nbuf, st


def _fox_proj(h, g, j, prm):
    q = norm_matmul(h, g, prm['fox_w_q'][j])
    k = norm_matmul(h, g, prm['fox_w_k'][j])
    v = norm_matmul(h, g, prm['fox_w_v'][j])
    og = norm_matmul(h, g, prm['fox_w_g'][j])
    fl = norm_matmul_hi(h, g, prm['fox_w_f'][j])
    return q, k, v, og, fl


def _fox_prep_args(j, prm):
    return (prm['fox_b_f'][j], prm['fox_q_norm'][j], prm['fox_k_norm'][j], prm['fox_head_sum'],
            prm['fox_head_expand'], prm['fox_place'])


def _fox_layer_prompt(h, B, L, j, g, prm, tq=512):
    q, k, v, og, fl = _fox_proj(h, g, j, prm)
    shp = (B, L, -1)
    qs, kn, logf, kx, vt = fox_prep(q.reshape(shp), k.reshape(shp), v.reshape(shp), fl.reshape(shp),
                                    *_fox_prep_args(j, prm))
    o = fox_attn(qs, kx, vt, og.reshape(shp), tq=tq)
    return matmul_res(o.reshape(B * L, -1), prm['fox_w_out'][j], h), kn, v.reshape(shp), logf


def _fox_layer_sample(h, cache, n_phys, page_table, j, g, prm):
    B, D = h.shape
    q, k, v, og, fl = _fox_proj(h, g, j, prm)
    one = (1, B, -1)
    qs, kn, logf, _, _ = fox_prep(q.reshape(one), k.reshape(one), v.reshape(one), fl.reshape(one),
                                  *_fox_prep_args(j, prm))
    ck, cv, clf = cache
    heads = (B, D // ATT_HD, ATT_HD)
    tok = (B, 1, -1)
    o = fox_decode(qs.astype(F32).reshape(heads), kn.reshape(heads), v.reshape(heads), logf.reshape(tok),
                   og.reshape(heads), ck, cv, clf, page_table + j * n_phys)
    return matmul_res(o.reshape(B, D), prm['fox_w_out'][j], h), kn.reshape(tok), v.reshape(tok), logf.reshape(tok)


def _prepare_params(raw):
    prm = dict(raw)
    D = raw['norm_final'].shape[0]
    for name in ('lru_w_in', 'lru_w_out', 'ssm_w_out', 'fox_w_out', 'moe_w_gate', 'moe_w_up', 'moe_w_down',
                 'ple_w_proj', 'ple_w_gate'):
        prm[name] = raw[name].astype(BF16)
    per = 2 * LANES // (D // LRU_BLOCKS)
    prm['lru_wa_bd'] = jax.vmap(lambda w: _block_diag_chunks(w, per))(raw['lru_w_a']).astype(BF16)
    prm['lru_wi_bd'] = jax.vmap(lambda w: _block_diag_chunks(w, per))(raw['lru_w_i']).astype(BF16)
    prm['lru_b_a'] = raw['lru_b_a'].reshape(raw['lru_b_a'].shape[0], -1)
    prm['lru_b_i'] = raw['lru_b_i'].reshape(raw['lru_b_i'].shape[0], -1)
    n_h = raw['ssm_a_log'].shape[1]
    di = n_h * SSM_P
    w = raw['ssm_w_in']
    conv_dim = raw['ssm_conv_w'].shape[2]
    prm['ssm_w_z'] = w[:, :, :di].astype(BF16)
    prm['ssm_w_xbc'] = w[:, :, di:di + conv_dim].astype(BF16)
    prm['ssm_w_dt'] = jax.vmap(lambda m: _pad_cols(m, LANES))(w[:, :, di + conv_dim:])
    prm['ssm_dt_bias'] = jax.vmap(_pad_lanes)(raw['ssm_dt_bias'])
    prm['ssm_a_log'] = jax.vmap(_pad_lanes)(raw['ssm_a_log'])
    prm['ssm_d_lanes'] = jnp.repeat(raw['ssm_d'], SSM_P, axis=1)[:, None, :]
    prm['ssm_expand'] = _head_expand(n_h, SSM_P)
    w = raw['fox_w_in']
    for n, name in enumerate(('fox_w_q', 'fox_w_k', 'fox_w_v', 'fox_w_g')):
        prm[name] = w[:, :, n * D:(n + 1) * D].astype(BF16)
    prm['fox_w_f'] = jax.vmap(lambda m: _pad_cols(m, LANES))(w[:, :, 4 * D:])
    prm['fox_b_f'] = jax.vmap(_pad_lanes)(raw['fox_b_f'])
    n_ah = D // ATT_HD
    prm['fox_q_norm'] = jnp.tile(raw['fox_q_norm'], (1, n_ah))[:, None, :]
    prm['fox_k_norm'] = jnp.tile(raw['fox_k_norm'], (1, n_ah))[:, None, :]
    prm['fox_head_expand'] = _head_expand(n_ah, ATT_HD)
    prm['fox_head_sum'] = prm['fox_head_expand'].T.astype(BF16)
    hh = jnp.arange(LANES)[None, :, None]
    jj = jnp.arange(N_PIECES)[:, None, None]
    col = jnp.arange(D)[None, None, :]
    prm['fox_place'] = ((hh < n_ah) & (col == LANES * (hh // PAIR) + N_PIECES * (hh % PAIR) + jj)).astype(BF16)
    prm['moe_w_router'] = jax.vmap(lambda we, wg: _pad_cols(jnp.concatenate([we, wg], axis=1), LANES))(
        raw['moe_w_expert'], raw['moe_w_group'])
    prm['moe_b_router'] = jax.vmap(lambda be, bg: _pad_lanes(jnp.concatenate([be, bg])))(
        raw['moe_b_expert'], raw['moe_b_group'])
    return prm


def kernel(x_prompt, x_sample, state_lru_h, state_lru_conv, state_ssm_h, state_ssm_conv, cache_k, cache_v, cache_logf, page_table, p_prompt, p_sample, lru_w_in, lru_conv_w, lru_conv_b, lru_w_a, lru_b_a, lru_w_i, lru_b_i, lru_lambda, lru_w_out, ssm_w_in, ssm_conv_w, ssm_conv_b, ssm_dt_bias, ssm_a_log, ssm_d, ssm_norm, ssm_w_out, fox_w_in, fox_b_f, fox_q_norm, fox_k_norm, fox_w_out, moe_w_group, moe_b_group, moe_w_expert, moe_b_expert, moe_w_gate, moe_w_up, moe_w_down, ple_w_proj, ple_w_gate, norm_mix, norm_ffn, norm_ple, norm_final):
    prm = _prepare_params(dict(
        lru_w_in=lru_w_in, lru_conv_w=lru_conv_w, lru_conv_b=lru_conv_b, lru_w_a=lru_w_a, lru_b_a=lru_b_a,
        lru_w_i=lru_w_i, lru_b_i=lru_b_i, lru_lambda=lru_lambda, lru_w_out=lru_w_out, ssm_w_in=ssm_w_in,
        ssm_conv_w=ssm_conv_w, ssm_conv_b=ssm_conv_b, ssm_dt_bias=ssm_dt_bias, ssm_a_log=ssm_a_log, ssm_d=ssm_d,
        ssm_norm=ssm_norm, ssm_w_out=ssm_w_out, fox_w_in=fox_w_in, fox_b_f=fox_b_f, fox_q_norm=fox_q_norm,
        fox_k_norm=fox_k_norm, fox_w_out=fox_w_out, moe_w_group=moe_w_group, moe_b_group=moe_b_group,
        moe_w_expert=moe_w_expert, moe_b_expert=moe_b_expert, moe_w_gate=moe_w_gate, moe_w_up=moe_w_up,
        moe_w_down=moe_w_down, ple_w_proj=ple_w_proj, ple_w_gate=ple_w_gate, norm_mix=norm_mix,
        norm_ffn=norm_ffn, norm_ple=norm_ple, norm_final=norm_final))
    depth = norm_mix.shape[0]
    B, L, D = x_prompt.shape
    Bs = x_sample.shape[0]
    n_mix = 3
    att_h = D // ATT_HD
    n_phys, page = cache_k.shape[1], cache_k.shape[2]
    cache = (cache_k.reshape(-1, page, att_h, ATT_HD), cache_v.reshape(-1, page, att_h, ATT_HD),
             cache_logf.reshape(-1, page, att_h))

    hp = x_prompt.reshape(B * L, D)
    hs = x_sample.reshape(Bs, D)
    outs = {k: [] for k in ('lru_h_p', 'lru_h_s', 'lru_c_p', 'lru_c_s', 'ssm_h_p', 'ssm_h_s', 'ssm_c_p', 'ssm_c_s',
                            'k_p', 'k_s', 'v_p', 'v_s', 'lf_p', 'lf_s')}
    yp = ys = None
    for i in range(depth):
        j = i // n_mix
        g = prm['norm_mix'][i]
        if i % n_mix == 0:
            hp, buf, hl = _lru_layer_prompt(hp, B, L, j, g, prm)
            outs['lru_c_p'].append(buf)
            outs['lru_h_p'].append(hl)
            hs, buf, hl = _lru_layer_sample(hs, state_lru_conv[j], state_lru_h[j], j, g, prm)
            outs['lru_c_s'].append(buf)
            outs['lru_h_s'].append(hl)
        elif i % n_mix == 1:
            hp, buf, st = _ssd_layer_prompt(hp, B, L, j, g, prm)
            outs['ssm_c_p'].append(buf)
            outs['ssm_h_p'].append(st)
            hs, buf, st = _ssd_layer_sample(hs, state_ssm_conv[j], state_ssm_h[j], j, g, prm)
            outs['ssm_c_s'].append(buf)
            outs['ssm_h_s'].append(st)
        else:
            hp, k, v, lf = _fox_layer_prompt(hp, B, L, j, g, prm)
            outs['k_p'].append(k.reshape(B, L, att_h, ATT_HD))
            outs['v_p'].append(v.reshape(B, L, att_h, ATT_HD))
            outs['lf_p'].append(lf[:, :, :att_h])
            hs, k, v, lf = _fox_layer_sample(hs, cache, n_phys, page_table, j, g, prm)
            outs['k_s'].append(k.reshape(Bs, 1, att_h, ATT_HD))
            outs['v_s'].append(v.reshape(Bs, 1, att_h, ATT_HD))
            outs['lf_s'].append(lf[:, :, :att_h])
        last = i == depth - 1
        hp = _moe_and_ple(hp, p_prompt[i].reshape(B * L, -1), i, prm, last)
        hs = _moe_and_ple(hs, p_sample[i].reshape(Bs, -1), i, prm, last)
        if last:
            hp, yp = hp
            hs, ys = hs
    st = {k: jnp.stack(v) for k, v in outs.items()}
    return (yp.reshape(B, L, D), ys.reshape(Bs, 1, D), st['lru_h_p'], st['lru_h_s'], st['lru_c_p'], st['lru_c_s'],
            st['ssm_h_p'], st['ssm_h_s'], st['ssm_c_p'], st['ssm_c_s'], st['k_p'], st['k_s'], st['v_p'], st['v_s'],
            st['lf_p'], st['lf_s'])
```

```python
import functools

import jax
import jax.numpy as jnp
from jax import lax
from jax.experimental import pallas as pl
from jax.experimental.pallas import tpu as pltpu

F32 = jnp.float32
BF16 = jnp.bfloat16
HI = lax.Precision.HIGHEST

EPS = 1e-6
CONV_W = 4
LANES = 128
SUBLANES = 8
LRU_C = 8.0
LRU_BLOCKS = 16
SSM_P = 64
SSM_G = 8
SSM_N = 128
ATT_HD = 64
MOE_GROUPS = 4
MOE_EPG = 4
MOE_E = MOE_GROUPS * MOE_EPG
NEG = -1e30
MIB = 1024 * 1024


def _cparams(sem, vmem_mib=48):
    return pltpu.CompilerParams(dimension_semantics=sem, vmem_limit_bytes=vmem_mib * MIB)


def _rms(x, g):
    ms = jnp.mean(x * x, axis=-1, keepdims=True)
    return x * lax.rsqrt(ms + EPS) * g


def _softplus(z):
    return jnp.maximum(z, 0.0) + jnp.log1p(jnp.exp(-jnp.abs(z)))


def _sigmoid(z):
    return 1.0 / (1.0 + jnp.exp(-z))


def _silu(z):
    return z * _sigmoid(z)


def _gelu_tanh(z):
    c = 0.7978845608028654
    return 0.5 * z * (1.0 + jnp.tanh(c * (z + 0.044715 * (z * z * z))))


def _dot(a, b):
    return jnp.dot(a, b, preferred_element_type=F32)


def _dot_hi(a, b):
    return jnp.dot(a, b, preferred_element_type=F32, precision=HI)


def _dot_nt(a, b):
    return lax.dot_general(a, b, (((1,), (1,)), ((), ())), preferred_element_type=F32)


def _dot_nt_hi(a, b):
    return lax.dot_general(a, b, (((1,), (1,)), ((), ())), preferred_element_type=F32, precision=HI)


def _pad_cols(w, n):
    return jnp.pad(w, ((0, 0), (0, n - w.shape[1])))


def _norm_matmul_kernel(x_ref, g_ref, w_ref, o_ref, xn_ref):
    @pl.when(pl.program_id(1) == 0)
    def _():
        xn_ref[...] = _rms(x_ref[...], g_ref[...]).astype(xn_ref.dtype)

    o_ref[...] = _dot(xn_ref[...], w_ref[...]).astype(o_ref.dtype)


def _norm_matmul_hi_kernel(x_ref, g_ref, w_ref, o_ref):
    o_ref[...] = _dot_hi(_rms(x_ref[...], g_ref[...]), w_ref[...])


def norm_matmul(x, g, w, *, tm=1024, tn=1024, out_dtype=F32):
    T, D = x.shape
    N = w.shape[1]
    tm, tn = min(tm, T), min(tn, N)
    assert T % tm == 0 and N % tn == 0
    return pl.pallas_call(
        _norm_matmul_kernel,
        grid=(T // tm, N // tn),
        in_specs=[pl.BlockSpec((tm, D), lambda i, j: (i, 0)),
                  pl.BlockSpec((1, D), lambda i, j: (0, 0)),
                  pl.BlockSpec((D, tn), lambda i, j: (0, j))],
        out_specs=pl.BlockSpec((tm, tn), lambda i, j: (i, j)),
        out_shape=jax.ShapeDtypeStruct((T, N), out_dtype),
        scratch_shapes=[pltpu.VMEM((tm, D), BF16)],
        compiler_params=_cparams(("parallel", "arbitrary")),
        name="norm_matmul",
    )(x, g.reshape(1, D), w)


def norm_matmul_hi(x, g, w, *, tm=512):
    T, D = x.shape
    N = w.shape[1]
    tm = min(tm, T)
    assert T % tm == 0
    return pl.pallas_call(
        _norm_matmul_hi_kernel,
        grid=(T // tm,),
        in_specs=[pl.BlockSpec((tm, D), lambda i: (i, 0)),
                  pl.BlockSpec((1, D), lambda i: (0, 0)),
                  pl.BlockSpec((D, N), lambda i: (0, 0))],
        out_specs=pl.BlockSpec((tm, N), lambda i: (i, 0)),
        out_shape=jax.ShapeDtypeStruct((T, N), F32),
        compiler_params=_cparams(("parallel",)),
        name="norm_matmul_hi",
    )(x, g.reshape(1, D), w)


def _matmul_res_kernel(a_ref, w_ref, r_ref, o_ref):
    o_ref[...] = r_ref[...] + _dot(a_ref[...], w_ref[...])


def matmul_res(a, w, res, *, tm=512):
    T, K = a.shape
    N = w.shape[1]
    tm = min(tm, T)
    assert T % tm == 0
    return pl.pallas_call(
        _matmul_res_kernel,
        grid=(T // tm,),
        in_specs=[pl.BlockSpec((tm, K), lambda i: (i, 0)),
                  pl.BlockSpec((K, N), lambda i: (0, 0)),
                  pl.BlockSpec((tm, N), lambda i: (i, 0))],
        out_specs=pl.BlockSpec((tm, N), lambda i: (i, 0)),
        out_shape=jax.ShapeDtypeStruct((T, N), F32),
        compiler_params=_cparams(("parallel",)),
        name="matmul_res",
    )(a, w, res)


def _ple_kernel(h_ref, p_ref, g_ref, wg_ref, wp_ref, gf_ref, o_ref, *maybe_final):
    h = h_ref[...]
    xn = _rms(h, g_ref[...]).astype(BF16)
    gate = _sigmoid(_dot(xn, wg_ref[...]))
    out = h + gate * _dot(p_ref[...].astype(BF16), wp_ref[...])
    o_ref[...] = out
    if maybe_final:
        maybe_final[0][...] = _rms(out, gf_ref[...])


def ple(h, p, g, w_gate, w_proj, g_final=None, *, tm=512):
    T, D = h.shape
    P = p.shape[1]
    tm = min(tm, T)
    assert T % tm == 0
    final = g_final is not None
    gf = (g_final if final else g).reshape(1, D)
    row = pl.BlockSpec((tm, D), lambda i: (i, 0))
    vec = pl.BlockSpec((1, D), lambda i: (0, 0))
    out_shape = jax.ShapeDtypeStruct((T, D), F32)
    return pl.pallas_call(
        _ple_kernel,
        grid=(T // tm,),
        in_specs=[row, pl.BlockSpec((tm, P), lambda i: (i, 0)), vec,
                  pl.BlockSpec((D, D), lambda i: (0, 0)),
                  pl.BlockSpec((P, D), lambda i: (0, 0)), vec],
        out_specs=(row, row) if final else row,
        out_shape=(out_shape, out_shape) if final else out_shape,
        compiler_params=_cparams(("parallel",)),
        name="ple",
    )(h, p, g.reshape(1, D), w_gate, w_proj, gf)


def _route(logits):
    lane = lax.broadcasted_iota(jnp.int32, logits.shape, 1)
    big = jnp.int32(1 << 20)
    is_g = (lane >= MOE_E) & (lane < MOE_E + MOE_GROUPS)
    glog = jnp.where(is_g, logits, NEG)
    gmax = jnp.max(glog, axis=-1, keepdims=True)
    gsel = jnp.min(jnp.where(is_g & (glog == gmax), lane, big), axis=-1, keepdims=True) - MOE_E
    gden = jnp.sum(jnp.where(is_g, jnp.exp(glog - gmax), 0.0), axis=-1, keepdims=True)
    gprob = 1.0 / gden
    in_g = (lane < MOE_E) & ((lane // MOE_EPG) == gsel)
    e1 = jnp.where(in_g, logits, NEG)
    v1 = jnp.max(e1, axis=-1, keepdims=True)
    i1 = jnp.min(jnp.where(in_g & (e1 == v1), lane, big), axis=-1, keepdims=True)
    in_g2 = in_g & (lane != i1)
    e2 = jnp.where(in_g2, logits, NEG)
    v2 = jnp.max(e2, axis=-1, keepdims=True)
    i2 = jnp.min(jnp.where(in_g2 & (e2 == v2), lane, big), axis=-1, keepdims=True)
    t = jnp.exp(v2 - v1)
    w1 = gprob / (1.0 + t)
    w2 = gprob * t / (1.0 + t)
    return jnp.where(lane == i1, w1, jnp.where(lane == i2, w2, 0.0))


def _moe_dense_kernel(h_ref, g_ref, wr_ref, br_ref, wg_ref, wu_ref, wd_ref, o_ref, xn_ref, comb_ref, acc_ref):
    e = pl.program_id(1)

    @pl.when(e == 0)
    def _():
        xn = _rms(h_ref[...], g_ref[...])
        comb_ref[...] = _route(_dot_hi(xn, wr_ref[...]) + br_ref[...])
        xn_ref[...] = xn.astype(BF16)
        acc_ref[...] = jnp.zeros_like(acc_ref)

    xn = xn_ref[...]
    hg = _dot(xn, wg_ref[0])
    hu = _dot(xn, wu_ref[0])
    comb = comb_ref[...]
    lane = lax.broadcasted_iota(jnp.int32, comb.shape, 1)
    c = jnp.sum(jnp.where(lane == e, comb, 0.0), axis=-1, keepdims=True)
    hid = (_silu(hg) * hu * c).astype(BF16)
    acc_ref[...] += _dot(hid, wd_ref[0])

    @pl.when(e == pl.num_programs(1) - 1)
    def _():
        o_ref[...] = h_ref[...] + acc_ref[...]


def moe_dense(h, g, w_router, b_router, w_gate, w_up, w_down, *, tm=1024):
    T, D = h.shape
    E, _, Fd = w_gate.shape
    tm = min(tm, T)
    assert T % tm == 0
    row = pl.BlockSpec((tm, D), lambda i, e: (i, 0))
    return pl.pallas_call(
        _moe_dense_kernel,
        grid=(T // tm, E),
        in_specs=[row, pl.BlockSpec((1, D), lambda i, e: (0, 0)),
                  pl.BlockSpec((D, LANES), lambda i, e: (0, 0)),
                  pl.BlockSpec((1, LANES), lambda i, e: (0, 0)),
                  pl.BlockSpec((1, D, Fd), lambda i, e: (e, 0, 0)),
                  pl.BlockSpec((1, D, Fd), lambda i, e: (e, 0, 0)),
                  pl.BlockSpec((1, Fd, D), lambda i, e: (e, 0, 0))],
        out_specs=row,
        out_shape=jax.ShapeDtypeStruct((T, D), F32),
        scratch_shapes=[pltpu.VMEM((tm, D), BF16), pltpu.VMEM((tm, LANES), F32), pltpu.VMEM((tm, D), F32)],
        compiler_params=_cparams(("parallel", "arbitrary")),
        name="moe_dense",
    )(h, g.reshape(1, D), w_router, b_router, w_gate, w_up, w_down)


def _conv_tile(xpad_ref, x, cw_ref, cb_ref, first):
    tl = x.shape[0]

    @pl.when(first)
    def _():
        xpad_ref[0:SUBLANES, :] = jnp.zeros((SUBLANES, x.shape[1]), F32)

    xpad_ref[SUBLANES:SUBLANES + tl, :] = x
    out = cb_ref[...] + x * cw_ref[CONV_W - 1:CONV_W, :]
    for k in range(CONV_W - 1):
        lo = SUBLANES - (CONV_W - 1) + k
        out = out + xpad_ref[lo:lo + tl, :] * cw_ref[k:k + 1, :]
    tail = xpad_ref[tl:tl + SUBLANES, :]
    xpad_ref[0:SUBLANES, :] = tail
    return out, tail[SUBLANES - (CONV_W - 1):, :]


def _lru_gates(xc, wa_ref, ba_ref, wi_ref, bi_ref, lam_ref):
    xcb = xc.astype(BF16)
    nb = wa_ref.shape[0]
    cw = wa_ref.shape[1]
    ra = jnp.concatenate([_dot(xcb[:, c * cw:(c + 1) * cw], wa_ref[c]) for c in range(nb)], axis=1)
    ia = jnp.concatenate([_dot(xcb[:, c * cw:(c + 1) * cw], wi_ref[c]) for c in range(nb)], axis=1)
    r = _sigmoid(ra + ba_ref[...])
    ig = _sigmoid(ia + bi_ref[...])
    log_a = (-LRU_C) * r * _softplus(-lam_ref[...])
    a = jnp.exp(log_a)
    b = jnp.sqrt(-jnp.tanh(log_a) * (a * a + 1.0)) * (ig * xc)
    return a, b


def _lru_seq_kernel(xb_ref, yb_ref, cw_ref, cb_ref, wa_ref, ba_ref, wi_ref, bi_ref, lam_ref,
                    o_ref, buf_ref, hl_ref, xpad_ref, a_ref, b_ref, hc_ref):
    t = pl.program_id(1)
    tl, W = xb_ref.shape[1], xb_ref.shape[2]
    xc, tail = _conv_tile(xpad_ref, xb_ref[0], cw_ref, cb_ref, t == 0)
    a, b = _lru_gates(xc, wa_ref, ba_ref, wi_ref, bi_ref, lam_ref)
    a_ref[...] = a
    b_ref[...] = b

    @pl.when(t == 0)
    def _():
        hc_ref[...] = jnp.zeros_like(hc_ref)

    row = lax.broadcasted_iota(jnp.int32, (SUBLANES, W), 0)

    def body(i, hc):
        r0 = pl.multiple_of(i * SUBLANES, SUBLANES)
        av = a_ref[pl.ds(r0, SUBLANES), :]
        bv = b_ref[pl.ds(r0, SUBLANES), :]
        for s in (1, 2, 4):
            keep = row >= s
            a_sh = pltpu.roll(av, s, 0)
            b_sh = pltpu.roll(bv, s, 0)
            bv = jnp.where(keep, av * b_sh + bv, bv)
            av = jnp.where(keep, av * a_sh, av)
        h = av * hc + bv
        b_ref[pl.ds(r0, SUBLANES), :] = h
        return jnp.broadcast_to(h[SUBLANES - 1:SUBLANES, :], (SUBLANES, W))

    hc = lax.fori_loop(0, tl // SUBLANES, body, hc_ref[...])
    hc_ref[...] = hc
    o_ref[0] = (b_ref[...] * _gelu_tanh(yb_ref[0])).astype(o_ref.dtype)
    buf_ref[0] = tail
    hl_ref[0] = hc[0:1, :]


def lru_seq(xy, conv_w, conv_b, wa_bd, b_a, wi_bd, b_i, lam, *, tl=256):
    B, L, W2 = xy.shape
    W = W2 // 2
    tl = min(tl, L)
    assert L % tl == 0 and tl % SUBLANES == 0
    vec = pl.BlockSpec((1, W), lambda b, t: (0, 0))
    wbd = pl.BlockSpec(wa_bd.shape, lambda b, t: (0, 0, 0))
    return pl.pallas_call(
        _lru_seq_kernel,
        grid=(B, L // tl),
        in_specs=[pl.BlockSpec((1, tl, W), lambda b, t: (b, t, 0)),
                  pl.BlockSpec((1, tl, W), lambda b, t: (b, t, 1)),
                  pl.BlockSpec((CONV_W, W), lambda b, t: (0, 0)), vec, wbd, vec, wbd, vec, vec],
        out_specs=(pl.BlockSpec((1, tl, W), lambda b, t: (b, t, 0)),
                   pl.BlockSpec((1, CONV_W - 1, W), lambda b, t: (b, 0, 0)),
                   pl.BlockSpec((1, 1, W), lambda b, t: (b, 0, 0))),
        out_shape=(jax.ShapeDtypeStruct((B, L, W), BF16),
                   jax.ShapeDtypeStruct((B, CONV_W - 1, W), F32),
                   jax.ShapeDtypeStruct((B, 1, W), F32)),
        scratch_shapes=[pltpu.VMEM((tl + SUBLANES, W), F32), pltpu.VMEM((tl, W), F32),
                        pltpu.VMEM((tl, W), F32), pltpu.VMEM((SUBLANES, W), F32)],
        compiler_params=_cparams(("parallel", "arbitrary")),
        name="lru_seq",
    )(xy, xy, conv_w, conv_b.reshape(1, W), wa_bd, b_a.reshape(1, W), wi_bd, b_i.reshape(1, W), lam.reshape(1, W))


def _lru_step_kernel(xb_ref, yb_ref, buf_ref, h0_ref, cw_ref, cb_ref, wa_ref, ba_ref, wi_ref, bi_ref, lam_ref,
                     o_ref, nbuf_ref, hl_ref):
    x = xb_ref[...]
    xc = cb_ref[...] + x * cw_ref[CONV_W - 1:CONV_W, :]
    for k in range(CONV_W - 1):
        xc = xc + buf_ref[k] * cw_ref[k:k + 1, :]
    a, b = _lru_gates(xc, wa_ref, ba_ref, wi_ref, bi_ref, lam_ref)
    h = a * h0_ref[...] + b
    o_ref[...] = (h * _gelu_tanh(yb_ref[...])).astype(o_ref.dtype)
    hl_ref[...] = h
    for k in range(CONV_W - 2):
        nbuf_ref[k] = buf_ref[k + 1]
    nbuf_ref[CONV_W - 2] = x


def lru_step(xy, buf_t, h0, conv_w, conv_b, wa_bd, b_a, wi_bd, b_i, lam):
    B, W2 = xy.shape
    W = W2 // 2
    vec = pl.BlockSpec((1, W), lambda i: (0, 0))
    wbd = pl.BlockSpec(wa_bd.shape, lambda i: (0, 0, 0))
    mat = pl.BlockSpec((B, W), lambda i: (0, 0))
    cube = pl.BlockSpec((CONV_W - 1, B, W), lambda i: (0, 0, 0))
    return pl.pallas_call(
        _lru_step_kernel,
        grid=(1,),
        in_specs=[mat, pl.BlockSpec((B, W), lambda i: (0, 1)), cube, mat,
                  pl.BlockSpec((CONV_W, W), lambda i: (0, 0)), vec, wbd, vec, wbd, vec, vec],
        out_specs=(mat, cube, mat),
        out_shape=(jax.ShapeDtypeStruct((B, W), BF16),
                   jax.ShapeDtypeStruct((CONV_W - 1, B, W), F32),
                   jax.ShapeDtypeStruct((B, W), F32)),
        compiler_params=_cparams(("arbitrary",)),
        name="lru_step",
    )(xy, xy, buf_t, h0, conv_w, conv_b.reshape(1, W), wa_bd, b_a.reshape(1, W), wi_bd, b_i.reshape(1, W),
      lam.reshape(1, W))


def _group_norm_gate(y, z, ng, n_groups):
    y = y * _silu(z)
    gw = y.shape[1] // n_groups
    outs = []
    for g in range(n_groups):
        yg = y[:, g * gw:(g + 1) * gw]
        ms = jnp.mean(yg * yg, axis=-1, keepdims=True)
        outs.append(yg * lax.rsqrt(ms + EPS))
    return jnp.concatenate(outs, axis=1) * ng


def _ssd_seq_kernel(z_ref, xbc_ref, dt_ref, cw_ref, cb_ref, dtb_ref, alog_ref, dsk_ref, ng_ref, ex_ref,
                    o_ref, buf_ref, st_ref, xpad_ref, s_ref):
    t = pl.program_id(1)
    Q = xbc_ref.shape[1]
    DI = z_ref.shape[2]
    GN = SSM_G * SSM_N
    RP = DI // SSM_G
    xc, tail = _conv_tile(xpad_ref, xbc_ref[0], cw_ref, cb_ref, t == 0)
    xc = _silu(xc)
    xs = xc[:, :DI]
    bm = xc[:, DI:DI + GN]
    cm = xc[:, DI + GN:]

    @pl.when(t == 0)
    def _():
        s_ref[...] = jnp.zeros_like(s_ref)

    dt = _softplus(dt_ref[0] + dtb_ref[...])
    a_neg = -jnp.exp(alog_ref[...])
    ri = lax.broadcasted_iota(jnp.int32, (Q, Q), 0)
    ci = lax.broadcasted_iota(jnp.int32, (Q, Q), 1)
    tri = ci <= ri
    acum = _dot_hi(tri.astype(F32), dt * a_neg)
    acum_t = acum.T
    ex = ex_ref[...]
    dt_e = _dot_hi(dt, ex)
    acum_e = _dot_hi(acum, ex)
    last_e = acum_e[Q - 1:Q, :]
    xdt = xs * dt_e
    xdtw = (xdt * jnp.exp(last_e - acum_e)).astype(BF16)
    xdt = xdt.astype(BF16)
    eacum = jnp.exp(acum_e)
    edec = jnp.exp(last_e)
    lane = lax.broadcasted_iota(jnp.int32, (Q, LANES), 1)
    lo = lane < SSM_P
    ys = []
    for g in range(SSM_G):
        cg = cm[:, g * SSM_N:(g + 1) * SSM_N].astype(BF16)
        bg32 = bm[:, g * SSM_N:(g + 1) * SSM_N]
        bg = bg32.astype(BF16)
        cb = _dot_nt(cg, bg)
        st = s_ref[g]
        yoff = _dot(cg, st.astype(BF16)) * eacum[:, g * RP:(g + 1) * RP]
        for pr in range(RP // LANES):
            ms = []
            for k in range(LANES // SSM_P):
                hd = (g * RP + pr * LANES) // SSM_P + k
                seg = acum[:, hd:hd + 1] - acum_t[hd:hd + 1, :]
                ms.append((cb * jnp.exp(jnp.where(tri, seg, NEG))).astype(BF16))
            c0 = g * RP + pr * LANES
            xp = xdt[:, c0:c0 + LANES]
            zero = jnp.zeros_like(xp)
            rhs = jnp.concatenate([jnp.where(lo, xp, zero), jnp.where(lo, zero, xp)], axis=0)
            ydiag = _dot(jnp.concatenate(ms, axis=1), rhs)
            ys.append(ydiag + yoff[:, pr * LANES:(pr + 1) * LANES])
        new = st * edec[:, g * RP:(g + 1) * RP] + _dot(bg32.T.astype(BF16), xdtw[:, g * RP:(g + 1) * RP])
        s_ref[g] = new
    y = jnp.concatenate(ys, axis=1) + xs * dsk_ref[...]
    o_ref[0] = _group_norm_gate(y, z_ref[0], ng_ref[...], SSM_G).astype(o_ref.dtype)
    buf_ref[0] = tail

    @pl.when(t == pl.num_programs(1) - 1)
    def _():
        hpg = RP // SSM_P
        for g in range(SSM_G):
            st_ref[0, g * hpg:(g + 1) * hpg] = s_ref[g].T.reshape(hpg, SSM_P, SSM_N)


def ssd_seq(z, xbc, dt, conv_w, conv_b, dt_bias, a_log, d_lanes, norm_g, expand, *, q=128):
    B, L, DI = z.shape
    C = xbc.shape[2]
    H = DI // SSM_P
    q = min(q, L)
    assert L % q == 0
    vecd = pl.BlockSpec((1, DI), lambda b, t: (0, 0))
    vecl = pl.BlockSpec((1, LANES), lambda b, t: (0, 0))
    return pl.pallas_call(
        _ssd_seq_kernel,
        grid=(B, L // q),
        in_specs=[pl.BlockSpec((1, q, DI), lambda b, t: (b, t, 0)),
                  pl.BlockSpec((1, q, C), lambda b, t: (b, t, 0)),
                  pl.BlockSpec((1, q, LANES), lambda b, t: (b, t, 0)),
                  pl.BlockSpec((CONV_W, C), lambda b, t: (0, 0)),
                  pl.BlockSpec((1, C), lambda b, t: (0, 0)),
                  vecl, vecl, vecd, vecd,
                  pl.BlockSpec((LANES, DI), lambda b, t: (0, 0))],
        out_specs=(pl.BlockSpec((1, q, DI), lambda b, t: (b, t, 0)),
                   pl.BlockSpec((1, CONV_W - 1, C), lambda b, t: (b, 0, 0)),
                   pl.BlockSpec((1, H, SSM_P, SSM_N), lambda b, t: (b, 0, 0, 0))),
        out_shape=(jax.ShapeDtypeStruct((B, L, DI), BF16),
                   jax.ShapeDtypeStruct((B, CONV_W - 1, C), F32),
                   jax.ShapeDtypeStruct((B, H, SSM_P, SSM_N), F32)),
        scratch_shapes=[pltpu.VMEM((q + SUBLANES, C), F32),
                        pltpu.VMEM((SSM_G, SSM_N, DI // SSM_G), F32)],
        compiler_params=_cparams(("parallel", "arbitrary")),
        name="ssd_seq",
    )(z, xbc, dt, conv_w, conv_b.reshape(1, C), dt_bias, a_log, d_lanes, norm_g.reshape(1, DI), expand)


def _ssd_step_kernel(z_ref, xbc_ref, dt_ref, buf_ref, h0_ref, cw_ref, cb_ref, dtb_ref, alog_ref, dsk_ref, ng_ref,
                     ex_ref, o_ref, nbuf_ref, hn_ref):
    DI = z_ref.shape[2]
    GN = SSM_G * SSM_N
    hpg = DI // SSM_P // SSM_G
    x = xbc_ref[0]
    buf = buf_ref[0]
    xc = cb_ref[...] + x * cw_ref[CONV_W - 1:CONV_W, :]
    for k in range(CONV_W - 1):
        xc = xc + buf[k:k + 1, :] * cw_ref[k:k + 1, :]
    nbuf_ref[0, 0:CONV_W - 2, :] = buf[1:CONV_W - 1, :]
    nbuf_ref[0, CONV_W - 2:CONV_W - 1, :] = x
    xc = _silu(xc)
    xs = xc[:, :DI]
    dt = _softplus(dt_ref[0] + dtb_ref[...])
    dta = dt * (-jnp.exp(alog_ref[...]))
    ex = ex_ref[...]
    dec = jnp.exp(dta)
    dt_e = _dot_hi(jnp.broadcast_to(dt, (SUBLANES, LANES)), ex)[0:1, :]
    xdt = xs * dt_e
    RP = hpg * SSM_P
    eye = (lax.broadcasted_iota(jnp.int32, (RP, RP), 0) == lax.broadcasted_iota(jnp.int32, (RP, RP), 1))
    ys = []
    for g in range(SSM_G):
        brow = xc[:, DI + g * SSM_N:DI + (g + 1) * SSM_N]
        crow = xc[:, DI + GN + g * SSM_N:DI + GN + (g + 1) * SSM_N]
        xg = xdt[:, g * RP:(g + 1) * RP]
        xcol = jnp.sum(jnp.where(eye, jnp.broadcast_to(xg, (RP, RP)), 0.0), axis=-1, keepdims=True)
        news = []
        for r in range(hpg):
            hd = g * hpg + r
            new = h0_ref[0, hd] * dec[:, hd:hd + 1] + xcol[r * SSM_P:(r + 1) * SSM_P, :] * brow
            hn_ref[0, hd] = new
            news.append(new)
        new_g = jnp.concatenate(news, axis=0).astype(BF16)
        ys.append(_dot_nt(jnp.broadcast_to(crow, (SUBLANES, SSM_N)).astype(BF16), new_g)[0:1, :])
    y = jnp.concatenate(ys, axis=1) + xs * dsk_ref[...]
    o_ref[0] = _group_norm_gate(y, z_ref[0], ng_ref[...], SSM_G).astype(o_ref.dtype)


def ssd_step(z, xbc, dt, buf, h0, conv_w, conv_b, dt_bias, a_log, d_lanes, norm_g, expand):
    B, _, DI = z.shape
    C = xbc.shape[2]
    H = DI // SSM_P
    vecd = pl.BlockSpec((1, DI), lambda b: (0, 0))
    vecl = pl.BlockSpec((1, LANES), lambda b: (0, 0))
    st = pl.BlockSpec((1, H, SSM_P, SSM_N), lambda b: (b, 0, 0, 0))
    return pl.pallas_call(
        _ssd_step_kernel,
        grid=(B,),
        in_specs=[pl.BlockSpec((1, 1, DI), lambda b: (b, 0, 0)),
                  pl.BlockSpec((1, 1, C), lambda b: (b, 0, 0)),
                  pl.BlockSpec((1, 1, LANES), lambda b: (b, 0, 0)),
                  pl.BlockSpec((1, CONV_W - 1, C), lambda b: (b, 0, 0)), st,
                  pl.BlockSpec((CONV_W, C), lambda b: (0, 0)),
                  pl.BlockSpec((1, C), lambda b: (0, 0)),
                  vecl, vecl, vecd, vecd,
                  pl.BlockSpec((LANES, DI), lambda b: (0, 0))],
        out_specs=(pl.BlockSpec((1, 1, DI), lambda b: (b, 0, 0)),
                   pl.BlockSpec((1, CONV_W - 1, C), lambda b: (b, 0, 0)), st),
        out_shape=(jax.ShapeDtypeStruct((B, 1, DI), BF16),
                   jax.ShapeDtypeStruct((B, CONV_W - 1, C), F32),
                   jax.ShapeDtypeStruct((B, H, SSM_P, SSM_N), F32)),
        compiler_params=_cparams(("parallel",)),
        name="ssd_step",
    )(z, xbc, dt, buf, h0, conv_w, conv_b.reshape(1, C), dt_bias, a_log, d_lanes, norm_g.reshape(1, DI), expand)


def _split_dot(x, m):
    hi = x.astype(BF16)
    lo = (x - hi.astype(F32)).astype(BF16)
    return _dot(hi, m) + _dot(lo, m)


LOG2E = 1.4426950408889634
N_PIECES = 3
PAIR = LANES // ATT_HD


def _fox_prep_kernel(q_ref, k_ref, v_ref, fl_ref, bf_ref, qg_ref, kg_ref, hs_ref, he_ref, pl_ref,
                     qo_ref, ko_ref, lf_ref, kx_ref, vt_ref, kt32_ref, vt32_ref, carry_ref):
    t = pl.program_id(1)
    tl = q_ref.shape[1]
    D = q_ref.shape[2]

    def head_norm(x, g):
        ss = _split_dot(x * x, hs_ref[...])
        inv = lax.rsqrt(ss * (1.0 / ATT_HD) + EPS)
        return x * _split_dot(inv, he_ref[...]) * g

    qo_ref[0] = (head_norm(q_ref[0], qg_ref[...]) * (ATT_HD ** -0.5 * LOG2E)).astype(qo_ref.dtype)
    kn = head_norm(k_ref[0], kg_ref[...])
    ko_ref[0] = kn
    kt32_ref[0] = kn.T
    vt = v_ref[0].T
    vt32_ref[0] = vt
    vt_ref[0] = vt.astype(BF16)
    z = fl_ref[0] + bf_ref[...]
    logf = jnp.minimum(z, 0.0) - jnp.log1p(jnp.exp(-jnp.abs(z)))
    lf_ref[0] = logf

    @pl.when(t == 0)
    def _():
        carry_ref[...] = jnp.zeros_like(carry_ref)

    ri = lax.broadcasted_iota(jnp.int32, (tl, tl), 0)
    ci = lax.broadcasted_iota(jnp.int32, (tl, tl), 1)
    c = _dot_hi((ci <= ri).astype(F32), logf) + carry_ref[0:1, :]
    carry_ref[...] = jnp.broadcast_to(c[tl - 1:tl, :], carry_ref.shape)
    rest = c * (-LOG2E)
    extra = jnp.zeros((tl, D), F32)
    for j in range(N_PIECES):
        piece = rest.astype(BF16)
        rest = rest - piece.astype(F32)
        extra = extra + _dot(piece, pl_ref[j])
    knb = kn.astype(BF16)
    extra = extra.astype(BF16)
    kx_ref[0] = jnp.concatenate(
        [x[:, p * LANES:(p + 1) * LANES] for p in range(D // LANES) for x in (knb, extra)], axis=1)


def fox_prep(q, k, v, fl, b_f, q_g, k_g, head_sum, head_expand, place, *, tl=256):
    B, L, D = q.shape
    tl = min(tl, L)
    assert L % tl == 0
    row = pl.BlockSpec((1, tl, D), lambda b, t: (b, t, 0))
    colt = pl.BlockSpec((1, D, tl), lambda b, t: (b, 0, t))
    nar = pl.BlockSpec((1, tl, LANES), lambda b, t: (b, t, 0))
    vecd = pl.BlockSpec((1, D), lambda b, t: (0, 0))
    return pl.pallas_call(
        _fox_prep_kernel,
        grid=(B, L // tl),
        in_specs=[row, row, row, nar, pl.BlockSpec((1, LANES), lambda b, t: (0, 0)), vecd, vecd,
                  pl.BlockSpec((D, LANES), lambda b, t: (0, 0)),
                  pl.BlockSpec((LANES, D), lambda b, t: (0, 0)),
                  pl.BlockSpec((N_PIECES, LANES, D), lambda b, t: (0, 0, 0))],
        out_specs=(row, row, nar, pl.BlockSpec((1, tl, 2 * D), lambda b, t: (b, t, 0)),
                   colt, colt, colt),
        out_shape=(jax.ShapeDtypeStruct((B, L, D), BF16),
                   jax.ShapeDtypeStruct((B, L, D), F32),
                   jax.ShapeDtypeStruct((B, L, LANES), F32),
                   jax.ShapeDtypeStruct((B, L, 2 * D), BF16),
                   jax.ShapeDtypeStruct((B, D, L), BF16),
                   jax.ShapeDtypeStruct((B, D, L), F32),
                   jax.ShapeDtypeStruct((B, D, L), F32)),
        scratch_shapes=[pltpu.VMEM((SUBLANES, LANES), F32)],
        compiler_params=_cparams(("parallel", "arbitrary")),
        name="fox_prep",
    )(q, k, v, fl, b_f, q_g, k_g, head_sum, head_expand, place)


def _fox_attn_kernel(q_ref, kx_ref, vt_ref, g_ref, o_ref, qx_ref, m_ref, l_ref, acc_ref):
    qi = pl.program_id(2)
    tq = q_ref.shape[1]
    q = q_ref[0].astype(F32)
    lane = lax.broadcasted_iota(jnp.int32, (tq, LANES), 1)
    for k in range(PAIR):
        mine = (lane >= k * ATT_HD) & (lane < (k + 1) * ATT_HD)
        pick = (lane >= k * N_PIECES) & (lane < (k + 1) * N_PIECES)
        qx_ref[k] = jnp.concatenate([jnp.where(mine, q, 0.0), jnp.where(pick, 1.0, 0.0)], axis=1).astype(BF16)
    m_ref[...] = jnp.full_like(m_ref, NEG)
    l_ref[...] = jnp.zeros_like(l_ref)
    acc_ref[...] = jnp.zeros_like(acc_ref)

    def block(j, diagonal):
        r0 = pl.multiple_of(j * tq, tq)
        kb = kx_ref[0, pl.ds(r0, tq), :]
        vt = vt_ref[0, :, pl.ds(r0, tq)]
        for k in range(PAIR):
            s = _dot_nt(kb, qx_ref[k])
            if diagonal:
                rows = lax.broadcasted_iota(jnp.int32, (tq, tq), 0)
                cols = lax.broadcasted_iota(jnp.int32, (tq, tq), 1)
                s = jnp.where(rows <= cols, s, NEG)
            m_old = m_ref[k]
            m_new = jnp.maximum(m_old, jnp.max(s, axis=0, keepdims=True))
            alpha = jnp.exp2(m_old - m_new)
            p = jnp.exp2(s - m_new)
            l_ref[k] = l_ref[k] * alpha + jnp.sum(p, axis=0, keepdims=True)
            m_ref[k] = m_new
            acc_ref[k] = acc_ref[k] * alpha + _dot(vt[k * ATT_HD:(k + 1) * ATT_HD, :], p.astype(BF16))

    def body(j, carry):
        block(j, False)
        return carry

    lax.fori_loop(0, qi, body, 0)
    block(qi, True)
    o = jnp.concatenate([acc_ref[k] / l_ref[k] for k in range(PAIR)], axis=0)
    o_ref[0] = (o.T * _sigmoid(g_ref[0])).astype(o_ref.dtype)


def fox_attn(q, kx, vt, g, *, tq=512):
    B, L, D = q.shape
    tq = min(tq, L)
    assert L % tq == 0
    qs = pl.BlockSpec((1, tq, LANES), lambda b, p, i: (b, i, p))
    return pl.pallas_call(
        _fox_attn_kernel,
        grid=(B, D // LANES, L // tq),
        in_specs=[qs,
                  pl.BlockSpec((1, L, 2 * LANES), lambda b, p, i: (b, 0, p)),
                  pl.BlockSpec((1, LANES, L), lambda b, p, i: (b, p, 0)),
                  qs],
        out_specs=qs,
        out_shape=jax.ShapeDtypeStruct((B, L, D), BF16),
        scratch_shapes=[pltpu.VMEM((PAIR, tq, 2 * LANES), BF16), pltpu.VMEM((PAIR, 1, tq), F32),
                        pltpu.VMEM((PAIR, 1, tq), F32), pltpu.VMEM((PAIR, ATT_HD, tq), F32)],
        compiler_params=_cparams(("parallel", "parallel", "arbitrary")),
        name="fox_attn",
    )(q, kx, vt, g)


def _lanes_to_sublanes(row):
    n = row.shape[1]
    eye = lax.broadcasted_iota(jnp.int32, (n, n), 0) == lax.broadcasted_iota(jnp.int32, (n, n), 1)
    return jnp.sum(jnp.where(eye, jnp.broadcast_to(row, (n, n)), 0.0), axis=1, keepdims=True)


def _sublanes_to_lanes(col):
    n = col.shape[0]
    eye = lax.broadcasted_iota(jnp.int32, (n, n), 0) == lax.broadcasted_iota(jnp.int32, (n, n), 1)
    return jnp.sum(jnp.where(eye, jnp.broadcast_to(col, (n, n)), 0.0), axis=0, keepdims=True)


def _fox_decode_kernel(pt_ref, q_ref, kn_ref, vn_ref, lfn_ref, g_ref, *rest, npg):
    k_refs = rest[:npg]
    v_refs = rest[npg:2 * npg]
    lf_refs = rest[2 * npg:3 * npg]
    o_ref = rest[3 * npg]
    qb_ref, m_ref, l_ref, acc_ref, coff_ref = rest[3 * npg + 1:]
    s_id = pl.program_id(1)
    H, hd, ps = k_refs[0].shape[1:]

    @pl.when(s_id == 0)
    def _():
        q = q_ref[0]
        for h in range(H):
            qb_ref[h] = jnp.broadcast_to(_lanes_to_sublanes(q[h:h + 1, :]), (hd, ps))
        m_ref[...] = jnp.full_like(m_ref, NEG)
        l_ref[...] = jnp.zeros_like(l_ref)
        acc_ref[...] = jnp.zeros_like(acc_ref)
        coff_ref[...] = jnp.zeros_like(coff_ref)

    ri = lax.broadcasted_iota(jnp.int32, (ps, ps), 0)
    ci = lax.broadcasted_iota(jnp.int32, (ps, ps), 1)
    upper = (ri <= ci).astype(F32)

    for i in range(npg):
        qk = jnp.concatenate([jnp.sum(k_refs[i][0, h] * qb_ref[h], axis=0, keepdims=True) for h in range(H)], axis=0)
        c = _dot_hi(lf_refs[i][0], upper) + coff_ref[...]
        coff_ref[...] = c[:, ps - 1:ps]
        s = qk - c * LOG2E
        m_old = m_ref[...]
        m_new = jnp.maximum(m_old, jnp.max(s, axis=1, keepdims=True))
        alpha = jnp.exp2(m_old - m_new)
        p = jnp.exp2(s - m_new)
        l_ref[...] = l_ref[...] * alpha + jnp.sum(p, axis=1, keepdims=True)
        m_ref[...] = m_new
        for h in range(H):
            acc_ref[h] = acc_ref[h] * alpha[h:h + 1, :] + p[h:h + 1, :] * v_refs[i][0, h]

    @pl.when(s_id == pl.num_programs(1) - 1)
    def _():
        s = (jnp.sum(q_ref[0] * kn_ref[0], axis=1, keepdims=True)
             - (coff_ref[...] + _lanes_to_sublanes(lfn_ref[0][:, 0:H])) * LOG2E)
        m_old = m_ref[...]
        m_new = jnp.maximum(m_old, s)
        alpha = jnp.exp2(m_old - m_new)
        p = jnp.exp2(s - m_new)
        l = l_ref[...] * alpha + p
        past = jnp.concatenate(
            [_sublanes_to_lanes(jnp.sum(acc_ref[h], axis=1, keepdims=True)) for h in range(H)], axis=0)
        o = (past * alpha + p * vn_ref[0]) / l
        o_ref[0] = (o * _sigmoid(g_ref[0])).astype(o_ref.dtype)


def fox_decode(q, k_new, v_new, lf_new, g, cache_kt, cache_vt, cache_lft, page_table, *, npg=4):
    B, H, hd = q.shape
    n_pages = page_table.shape[1]
    ps = cache_kt.shape[3]
    npg = min(npg, n_pages)
    assert n_pages % npg == 0
    steps = n_pages // npg
    row = pl.BlockSpec((1, H, hd), lambda b, s, pt: (b, 0, 0))

    def page(i, *tail):
        return pl.BlockSpec((1,) + tail, lambda b, s, pt: (pt[b * n_pages + s * npg + i],) + (0,) * len(tail))

    in_specs = ([row, row, row, pl.BlockSpec((1, 1, LANES), lambda b, s, pt: (b, 0, 0)), row]
                + [page(i, H, hd, ps) for i in range(npg)] + [page(i, H, hd, ps) for i in range(npg)]
                + [page(i, H, ps) for i in range(npg)])
    grid_spec = pltpu.PrefetchScalarGridSpec(
        num_scalar_prefetch=1, grid=(B, steps), in_specs=in_specs, out_specs=row,
        scratch_shapes=[pltpu.VMEM((H, hd, ps), F32), pltpu.VMEM((H, 1), F32), pltpu.VMEM((H, 1), F32),
                        pltpu.VMEM((H, hd, ps), F32), pltpu.VMEM((H, 1), F32)])
    return pl.pallas_call(
        functools.partial(_fox_decode_kernel, npg=npg),
        grid_spec=grid_spec,
        out_shape=jax.ShapeDtypeStruct((B, H, hd), BF16),
        compiler_params=_cparams(("parallel", "arbitrary")),
        name="fox_decode",
    )(page_table.reshape(-1), q, k_new, v_new, lf_new, g,
      *([cache_kt] * npg), *([cache_vt] * npg), *([cache_lft] * npg))


def _block_diag_chunks(w, per_chunk):
    nblk, bw, _ = w.shape
    w = w.reshape(nblk // per_chunk, per_chunk, bw, bw)
    eye = jnp.eye(per_chunk, dtype=w.dtype)
    out = jnp.einsum('cpij,pq->cpiqj', w, eye)
    return out.reshape(nblk // per_chunk, per_chunk * bw, per_chunk * bw)


def _head_expand(n_heads, width):
    r = jnp.arange(LANES)[:, None]
    c = jnp.arange(n_heads * width)[None, :] // width
    return (r == c).astype(F32)


def _pad_lanes(v):
    return jnp.pad(v.reshape(1, -1), ((0, 0), (0, LANES - v.size)))


def _moe_and_ple(h, p_i, i, prm, last):
    h = moe_dense(h, prm['norm_ffn'][i], prm['moe_w_router'][i], prm['moe_b_router'][i],
                  prm['moe_w_gate'][i], prm['moe_w_up'][i], prm['moe_w_down'][i])
    return ple(h, p_i, prm['norm_ple'][i], prm['ple_w_gate'][i], prm['ple_w_proj'][i],
               prm['norm_final'] if last else None)


def _lru_layer_prompt(h, B, L, j, g, prm):
    D = h.shape[1]
    xy = norm_matmul(h, g, prm['lru_w_in'][j])
    gated, buf, hl = lru_seq(xy.reshape(B, L, -1), prm['lru_conv_w'][j], prm['lru_conv_b'][j],
                             prm['lru_wa_bd'][j], prm['lru_b_a'][j], prm['lru_wi_bd'][j], prm['lru_b_i'][j],
                             prm['lru_lambda'][j])
    h = matmul_res(gated.reshape(B * L, -1), prm['lru_w_out'][j], h)
    return h, buf, hl.reshape(B, D)


def _lru_layer_sample(h, buf, h0, j, g, prm):
    xy = norm_matmul(h, g, prm['lru_w_in'][j])
    gated, nbuf, hl = lru_step(xy, jnp.swapaxes(buf, 0, 1), h0, prm['lru_conv_w'][j], prm['lru_conv_b'][j],
                               prm['lru_wa_bd'][j], prm['lru_b_a'][j], prm['lru_wi_bd'][j], prm['lru_b_i'][j],
                               prm['lru_lambda'][j])
    h = matmul_res(gated, prm['lru_w_out'][j], h)
    return h, jnp.swapaxes(nbuf, 0, 1), hl


def _ssd_proj(h, g, j, prm):
    z = norm_matmul(h, g, prm['ssm_w_z'][j])
    xbc = norm_matmul(h, g, prm['ssm_w_xbc'][j])
    dt = norm_matmul_hi(h, g, prm['ssm_w_dt'][j])
    return z, xbc, dt


def _ssd_args(j, prm):
    return (prm['ssm_conv_w'][j], prm['ssm_conv_b'][j], prm['ssm_dt_bias'][j], prm['ssm_a_log'][j],
            prm['ssm_d_lanes'][j], prm['ssm_norm'][j], prm['ssm_expand'])


def _ssd_layer_prompt(h, B, L, j, g, prm):
    z, xbc, dt = _ssd_proj(h, g, j, prm)
    y, buf, st = ssd_seq(z.reshape(B, L, -1), xbc.reshape(B, L, -1), dt.reshape(B, L, -1), *_ssd_args(j, prm))
    return matmul_res(y.reshape(B * L, -1), prm['ssm_w_out'][j], h), buf, st


def _ssd_layer_sample(h, buf, h0, j, g, prm):
    B = h.shape[0]
    z, xbc, dt = _ssd_proj(h, g, j, prm)
    y, nbuf, st = ssd_step(z.reshape(B, 1, -1), xbc.reshape(B, 1, -1), dt.reshape(B, 1, -1), buf, h0,
                           *_ssd_args(j, prm))
    return matmul_res(y.reshape(B, -1), prm['ssm_w_out'][j], h), nbuf, st


def _fox_proj(h, g, j, prm):
    q = norm_matmul(h, g, prm['fox_w_q'][j])
    k = norm_matmul(h, g, prm['fox_w_k'][j])
    v = norm_matmul(h, g, prm['fox_w_v'][j])
    og = norm_matmul(h, g, prm['fox_w_g'][j])
    fl = norm_matmul_hi(h, g, prm['fox_w_f'][j])
    return q, k, v, og, fl


def _fox_prep_args(j, prm):
    return (prm['fox_b_f'][j], prm['fox_q_norm'][j], prm['fox_k_norm'][j], prm['fox_head_sum'],
            prm['fox_head_expand'], prm['fox_place'])


def _fox_layer_prompt(h, B, L, j, g, prm, tq=512):
    q, k, v, og, fl = _fox_proj(h, g, j, prm)
    shp = (B, L, -1)
    qs, _, logf, kx, vt, kt32, vt32 = fox_prep(q.reshape(shp), k.reshape(shp), v.reshape(shp), fl.reshape(shp),
                                               *_fox_prep_args(j, prm))
    o = fox_attn(qs, kx, vt, og.reshape(shp), tq=tq)

    def per_head(xt):
        return jnp.transpose(xt.reshape(B, -1, ATT_HD, L), (0, 3, 1, 2))

    return matmul_res(o.reshape(B * L, -1), prm['fox_w_out'][j], h), per_head(kt32), per_head(vt32), logf


def _fox_layer_sample(h, cache, n_phys, page_table, j, g, prm):
    B, D = h.shape
    q, k, v, og, fl = _fox_proj(h, g, j, prm)
    one = (1, B, -1)
    qs, kn, logf = fox_prep(q.reshape(one), k.reshape(one), v.reshape(one), fl.reshape(one),
                            *_fox_prep_args(j, prm))[:3]
    ck, cv, clf = cache
    heads = (B, D // ATT_HD, ATT_HD)
    tok = (B, 1, -1)
    o = fox_decode(qs.astype(F32).reshape(heads), kn.reshape(heads), v.reshape(heads), logf.reshape(tok),
                   og.reshape(heads), ck, cv, clf, page_table + j * n_phys)
    return matmul_res(o.reshape(B, D), prm['fox_w_out'][j], h), kn.reshape(tok), v.reshape(tok), logf.reshape(tok)


def _prepare_params(raw):
    prm = dict(raw)
    D = raw['norm_final'].shape[0]
    for name in ('lru_w_in', 'lru_w_out', 'ssm_w_out', 'fox_w_out', 'moe_w_gate', 'moe_w_up', 'moe_w_down',
                 'ple_w_proj', 'ple_w_gate'):
        prm[name] = raw[name].astype(BF16)
    per = 2 * LANES // (D // LRU_BLOCKS)
    prm['lru_wa_bd'] = jax.vmap(lambda w: _block_diag_chunks(w, per))(raw['lru_w_a']).astype(BF16)
    prm['lru_wi_bd'] = jax.vmap(lambda w: _block_diag_chunks(w, per))(raw['lru_w_i']).astype(BF16)
    prm['lru_b_a'] = raw['lru_b_a'].reshape(raw['lru_b_a'].shape[0], -1)
    prm['lru_b_i'] = raw['lru_b_i'].reshape(raw['lru_b_i'].shape[0], -1)
    n_h = raw['ssm_a_log'].shape[1]
    di = n_h * SSM_P
    w = raw['ssm_w_in']
    conv_dim = raw['ssm_conv_w'].shape[2]
    prm['ssm_w_z'] = w[:, :, :di].astype(BF16)
    prm['ssm_w_xbc'] = w[:, :, di:di + conv_dim].astype(BF16)
    prm['ssm_w_dt'] = jax.vmap(lambda m: _pad_cols(m, LANES))(w[:, :, di + conv_dim:])
    prm['ssm_dt_bias'] = jax.vmap(_pad_lanes)(raw['ssm_dt_bias'])
    prm['ssm_a_log'] = jax.vmap(_pad_lanes)(raw['ssm_a_log'])
    prm['ssm_d_lanes'] = jnp.repeat(raw['ssm_d'], SSM_P, axis=1)[:, None, :]
    prm['ssm_expand'] = _head_expand(n_h, SSM_P)
    w = raw['fox_w_in']
    for n, name in enumerate(('fox_w_q', 'fox_w_k', 'fox_w_v', 'fox_w_g')):
        prm[name] = w[:, :, n * D:(n + 1) * D].astype(BF16)
    prm['fox_w_f'] = jax.vmap(lambda m: _pad_cols(m, LANES))(w[:, :, 4 * D:])
    prm['fox_b_f'] = jax.vmap(_pad_lanes)(raw['fox_b_f'])
    n_ah = D // ATT_HD
    prm['fox_q_norm'] = jnp.tile(raw['fox_q_norm'], (1, n_ah))[:, None, :]
    prm['fox_k_norm'] = jnp.tile(raw['fox_k_norm'], (1, n_ah))[:, None, :]
    prm['fox_head_expand'] = _head_expand(n_ah, ATT_HD)
    prm['fox_head_sum'] = prm['fox_head_expand'].T.astype(BF16)
    hh = jnp.arange(LANES)[None, :, None]
    jj = jnp.arange(N_PIECES)[:, None, None]
    col = jnp.arange(D)[None, None, :]
    prm['fox_place'] = ((hh < n_ah) & (col == LANES * (hh // PAIR) + N_PIECES * (hh % PAIR) + jj)).astype(BF16)
    prm['moe_w_router'] = jax.vmap(lambda we, wg: _pad_cols(jnp.concatenate([we, wg], axis=1), LANES))(
        raw['moe_w_expert'], raw['moe_w_group'])
    prm['moe_b_router'] = jax.vmap(lambda be, bg: _pad_lanes(jnp.concatenate([be, bg])))(
        raw['moe_b_expert'], raw['moe_b_group'])
    return prm


def kernel(x_prompt, x_sample, state_lru_h, state_lru_conv, state_ssm_h, state_ssm_conv, cache_k, cache_v, cache_logf, page_table, p_prompt, p_sample, lru_w_in, lru_conv_w, lru_conv_b, lru_w_a, lru_b_a, lru_w_i, lru_b_i, lru_lambda, lru_w_out, ssm_w_in, ssm_conv_w, ssm_conv_b, ssm_dt_bias, ssm_a_log, ssm_d, ssm_norm, ssm_w_out, fox_w_in, fox_b_f, fox_q_norm, fox_k_norm, fox_w_out, moe_w_group, moe_b_group, moe_w_expert, moe_b_expert, moe_w_gate, moe_w_up, moe_w_down, ple_w_proj, ple_w_gate, norm_mix, norm_ffn, norm_ple, norm_final):
    prm = _prepare_params(dict(
        lru_w_in=lru_w_in, lru_conv_w=lru_conv_w, lru_conv_b=lru_conv_b, lru_w_a=lru_w_a, lru_b_a=lru_b_a,
        lru_w_i=lru_w_i, lru_b_i=lru_b_i, lru_lambda=lru_lambda, lru_w_out=lru_w_out, ssm_w_in=ssm_w_in,
        ssm_conv_w=ssm_conv_w, ssm_conv_b=ssm_conv_b, ssm_dt_bias=ssm_dt_bias, ssm_a_log=ssm_a_log, ssm_d=ssm_d,
        ssm_norm=ssm_norm, ssm_w_out=ssm_w_out, fox_w_in=fox_w_in, fox_b_f=fox_b_f, fox_q_norm=fox_q_norm,
        fox_k_norm=fox_k_norm, fox_w_out=fox_w_out, moe_w_group=moe_w_group, moe_b_group=moe_b_group,
        moe_w_expert=moe_w_expert, moe_b_expert=moe_b_expert, moe_w_gate=moe_w_gate, moe_w_up=moe_w_up,
        moe_w_down=moe_w_down, ple_w_proj=ple_w_proj, ple_w_gate=ple_w_gate, norm_mix=norm_mix,
        norm_ffn=norm_ffn, norm_ple=norm_ple, norm_final=norm_final))
    depth = norm_mix.shape[0]
    B, L, D = x_prompt.shape
    Bs = x_sample.shape[0]
    n_mix = 3
    att_h = D // ATT_HD
    n_phys, page = cache_k.shape[1], cache_k.shape[2]
    cache = (jnp.transpose(cache_k, (0, 1, 3, 4, 2)).reshape(-1, att_h, ATT_HD, page),
             jnp.transpose(cache_v, (0, 1, 3, 4, 2)).reshape(-1, att_h, ATT_HD, page),
             jnp.transpose(cache_logf, (0, 1, 3, 2)).reshape(-1, att_h, page))

    hp = x_prompt.reshape(B * L, D)
    hs = x_sample.reshape(Bs, D)
    outs = {k: [] for k in ('lru_h_p', 'lru_h_s', 'lru_c_p', 'lru_c_s', 'ssm_h_p', 'ssm_h_s', 'ssm_c_p', 'ssm_c_s',
                            'k_p', 'k_s', 'v_p', 'v_s', 'lf_p', 'lf_s')}
    yp = ys = None
    for i in range(depth):
        j = i // n_mix
        g = prm['norm_mix'][i]
        if i % n_mix == 0:
            hp, buf, hl = _lru_layer_prompt(hp, B, L, j, g, prm)
            outs['lru_c_p'].append(buf)
            outs['lru_h_p'].append(hl)
            hs, buf, hl = _lru_layer_sample(hs, state_lru_conv[j], state_lru_h[j], j, g, prm)
            outs['lru_c_s'].append(buf)
            outs['lru_h_s'].append(hl)
        elif i % n_mix == 1:
            hp, buf, st = _ssd_layer_prompt(hp, B, L, j, g, prm)
            outs['ssm_c_p'].append(buf)
            outs['ssm_h_p'].append(st)
            hs, buf, st = _ssd_layer_sample(hs, state_ssm_conv[j], state_ssm_h[j], j, g, prm)
            outs['ssm_c_s'].append(buf)
            outs['ssm_h_s'].append(st)
        else:
            hp, k, v, lf = _fox_layer_prompt(hp, B, L, j, g, prm)
            outs['k_p'].append(k)
            outs['v_p'].append(v)
            outs['lf_p'].append(lf[:, :, :att_h])
            hs, k, v, lf = _fox_layer_sample(hs, cache, n_phys, page_table, j, g, prm)
            outs['k_s'].append(k.reshape(Bs, 1, att_h, ATT_HD))
            outs['v_s'].append(v.reshape(Bs, 1, att_h, ATT_HD))
            outs['lf_s'].append(lf[:, :, :att_h])
        last = i == depth - 1
        hp = _moe_and_ple(hp, p_prompt[i].reshape(B * L, -1), i, prm, last)
        hs = _moe_and_ple(hs, p_sample[i].reshape(Bs, -1), i, prm, last)
        if last:
            hp, yp = hp
            hs, ys = hs
    st = {k: jnp.stack(v) for k, v in outs.items()}
    return (yp.reshape(B, L, D), ys.reshape(Bs, 1, D), st['lru_h_p'], st['lru_h_s'], st['lru_c_p'], st['lru_c_s'],
            st['ssm_h_p'], st['ssm_h_s'], st['ssm_c_p'], st['ssm_c_s'], st['k_p'], st['k_s'], st['v_p'], st['v_s'],
            st['lf_p'], st['lf_s'])
```

```python
import functools

import jax
import jax.numpy as jnp
from jax import lax
from jax.experimental import pallas as pl
from jax.experimental.pallas import tpu as pltpu

F32 = jnp.float32
BF16 = jnp.bfloat16
HI = lax.Precision.HIGHEST

EPS = 1e-6
CONV_W = 4
LANES = 128
SUBLANES = 8
LRU_C = 8.0
LRU_BLOCKS = 16
SSM_P = 64
SSM_G = 8
SSM_N = 128
ATT_HD = 64
MOE_GROUPS = 4
MOE_EPG = 4
MOE_E = MOE_GROUPS * MOE_EPG
NEG = -1e30
MIB = 1024 * 1024


def _cparams(sem, vmem_mib=48):
    return pltpu.CompilerParams(dimension_semantics=sem, vmem_limit_bytes=vmem_mib * MIB)


def _rms(x, g):
    ms = jnp.mean(x * x, axis=-1, keepdims=True)
    return x * lax.rsqrt(ms + EPS) * g


def _softplus(z):
    return jnp.maximum(z, 0.0) + jnp.log1p(jnp.exp(-jnp.abs(z)))


def _sigmoid(z):
    return 1.0 / (1.0 + jnp.exp(-z))


def _silu(z):
    return z * _sigmoid(z)


def _gelu_tanh(z):
    c = 0.7978845608028654
    return 0.5 * z * (1.0 + jnp.tanh(c * (z + 0.044715 * (z * z * z))))


def _dot(a, b):
    return jnp.dot(a, b, preferred_element_type=F32)


def _dot_hi(a, b):
    return jnp.dot(a, b, preferred_element_type=F32, precision=HI)


def _dot_nt(a, b):
    return lax.dot_general(a, b, (((1,), (1,)), ((), ())), preferred_element_type=F32)


def _dot_nt_hi(a, b):
    return lax.dot_general(a, b, (((1,), (1,)), ((), ())), preferred_element_type=F32, precision=HI)


def _place_dot(x, m, pieces=3):
    out = None
    rest = x
    for _ in range(pieces):
        piece = rest.astype(BF16)
        rest = rest - piece.astype(F32)
        out = _dot(piece, m) if out is None else out + _dot(piece, m)
    return out


def _pad_cols(w, n):
    return jnp.pad(w, ((0, 0), (0, n - w.shape[1])))


def _norm_matmul_kernel(x_ref, g_ref, w_ref, o_ref, xn_ref):
    @pl.when(pl.program_id(1) == 0)
    def _():
        xn_ref[...] = _rms(x_ref[...], g_ref[...]).astype(xn_ref.dtype)

    o_ref[...] = _dot(xn_ref[...], w_ref[...]).astype(o_ref.dtype)


def _norm_matmul_hi_kernel(x_ref, g_ref, w_ref, o_ref):
    o_ref[...] = _dot_hi(_rms(x_ref[...], g_ref[...]), w_ref[...])


def norm_matmul(x, g, w, *, tm=1024, tn=1024, out_dtype=F32):
    T, D = x.shape
    N = w.shape[1]
    tm, tn = min(tm, T), min(tn, N)
    assert T % tm == 0 and N % tn == 0
    return pl.pallas_call(
        _norm_matmul_kernel,
        grid=(T // tm, N // tn),
        in_specs=[pl.BlockSpec((tm, D), lambda i, j: (i, 0)),
                  pl.BlockSpec((1, D), lambda i, j: (0, 0)),
                  pl.BlockSpec((D, tn), lambda i, j: (0, j))],
        out_specs=pl.BlockSpec((tm, tn), lambda i, j: (i, j)),
        out_shape=jax.ShapeDtypeStruct((T, N), out_dtype),
        scratch_shapes=[pltpu.VMEM((tm, D), BF16)],
        compiler_params=_cparams(("parallel", "arbitrary")),
        name="norm_matmul",
    )(x, g.reshape(1, D), w)


def norm_matmul_hi(x, g, w, *, tm=512):
    T, D = x.shape
    N = w.shape[1]
    tm = min(tm, T)
    assert T % tm == 0
    return pl.pallas_call(
        _norm_matmul_hi_kernel,
        grid=(T // tm,),
        in_specs=[pl.BlockSpec((tm, D), lambda i: (i, 0)),
                  pl.BlockSpec((1, D), lambda i: (0, 0)),
                  pl.BlockSpec((D, N), lambda i: (0, 0))],
        out_specs=pl.BlockSpec((tm, N), lambda i: (i, 0)),
        out_shape=jax.ShapeDtypeStruct((T, N), F32),
        compiler_params=_cparams(("parallel",)),
        name="norm_matmul_hi",
    )(x, g.reshape(1, D), w)


def _matmul_res_kernel(a_ref, w_ref, r_ref, o_ref):
    o_ref[...] = r_ref[...] + _dot(a_ref[...], w_ref[...])


def matmul_res(a, w, res, *, tm=512):
    T, K = a.shape
    N = w.shape[1]
    tm = min(tm, T)
    assert T % tm == 0
    return pl.pallas_call(
        _matmul_res_kernel,
        grid=(T // tm,),
        in_specs=[pl.BlockSpec((tm, K), lambda i: (i, 0)),
                  pl.BlockSpec((K, N), lambda i: (0, 0)),
                  pl.BlockSpec((tm, N), lambda i: (i, 0))],
        out_specs=pl.BlockSpec((tm, N), lambda i: (i, 0)),
        out_shape=jax.ShapeDtypeStruct((T, N), F32),
        compiler_params=_cparams(("parallel",)),
        name="matmul_res",
    )(a, w, res)


def _ple_kernel(h_ref, p_ref, g_ref, wg_ref, wp_ref, gf_ref, o_ref, *maybe_final):
    h = h_ref[...]
    xn = _rms(h, g_ref[...]).astype(BF16)
    gate = _sigmoid(_dot(xn, wg_ref[...]))
    out = h + gate * _dot(p_ref[...].astype(BF16), wp_ref[...])
    o_ref[...] = out
    if maybe_final:
        maybe_final[0][...] = _rms(out, gf_ref[...])


def ple(h, p, g, w_gate, w_proj, g_final=None, *, tm=512):
    T, D = h.shape
    P = p.shape[1]
    tm = min(tm, T)
    assert T % tm == 0
    final = g_final is not None
    gf = (g_final if final else g).reshape(1, D)
    row = pl.BlockSpec((tm, D), lambda i: (i, 0))
    vec = pl.BlockSpec((1, D), lambda i: (0, 0))
    out_shape = jax.ShapeDtypeStruct((T, D), F32)
    return pl.pallas_call(
        _ple_kernel,
        grid=(T // tm,),
        in_specs=[row, pl.BlockSpec((tm, P), lambda i: (i, 0)), vec,
                  pl.BlockSpec((D, D), lambda i: (0, 0)),
                  pl.BlockSpec((P, D), lambda i: (0, 0)), vec],
        out_specs=(row, row) if final else row,
        out_shape=(out_shape, out_shape) if final else out_shape,
        compiler_params=_cparams(("parallel",)),
        name="ple",
    )(h, p, g.reshape(1, D), w_gate, w_proj, gf)


def _route(logits):
    lane = lax.broadcasted_iota(jnp.int32, logits.shape, 1)
    big = jnp.int32(1 << 20)
    is_g = (lane >= MOE_E) & (lane < MOE_E + MOE_GROUPS)
    glog = jnp.where(is_g, logits, NEG)
    gmax = jnp.max(glog, axis=-1, keepdims=True)
    gsel = jnp.min(jnp.where(is_g & (glog == gmax), lane, big), axis=-1, keepdims=True) - MOE_E
    gden = jnp.sum(jnp.where(is_g, jnp.exp(glog - gmax), 0.0), axis=-1, keepdims=True)
    gprob = 1.0 / gden
    in_g = (lane < MOE_E) & ((lane // MOE_EPG) == gsel)
    e1 = jnp.where(in_g, logits, NEG)
    v1 = jnp.max(e1, axis=-1, keepdims=True)
    i1 = jnp.min(jnp.where(in_g & (e1 == v1), lane, big), axis=-1, keepdims=True)
    in_g2 = in_g & (lane != i1)
    e2 = jnp.where(in_g2, logits, NEG)
    v2 = jnp.max(e2, axis=-1, keepdims=True)
    i2 = jnp.min(jnp.where(in_g2 & (e2 == v2), lane, big), axis=-1, keepdims=True)
    t = jnp.exp(v2 - v1)
    w1 = gprob / (1.0 + t)
    w2 = gprob * t / (1.0 + t)
    return jnp.where(lane == i1, w1, jnp.where(lane == i2, w2, 0.0))


def _moe_dense_kernel(h_ref, g_ref, wr_ref, br_ref, wg_ref, wu_ref, wd_ref, o_ref, xn_ref, comb_ref, acc_ref):
    e = pl.program_id(1)

    @pl.when(e == 0)
    def _():
        xn = _rms(h_ref[...], g_ref[...])
        comb_ref[...] = _route(_dot_hi(xn, wr_ref[...]) + br_ref[...])
        xn_ref[...] = xn.astype(BF16)
        acc_ref[...] = jnp.zeros_like(acc_ref)

    xn = xn_ref[...]
    hg = _dot(xn, wg_ref[0])
    hu = _dot(xn, wu_ref[0])
    comb = comb_ref[...]
    lane = lax.broadcasted_iota(jnp.int32, comb.shape, 1)
    c = jnp.sum(jnp.where(lane == e, comb, 0.0), axis=-1, keepdims=True)
    hid = (_silu(hg) * hu * c).astype(BF16)
    acc_ref[...] += _dot(hid, wd_ref[0])

    @pl.when(e == pl.num_programs(1) - 1)
    def _():
        o_ref[...] = h_ref[...] + acc_ref[...]


def moe_dense(h, g, w_router, b_router, w_gate, w_up, w_down, *, tm=1024):
    T, D = h.shape
    E, _, Fd = w_gate.shape
    tm = min(tm, T)
    assert T % tm == 0
    row = pl.BlockSpec((tm, D), lambda i, e: (i, 0))
    return pl.pallas_call(
        _moe_dense_kernel,
        grid=(T // tm, E),
        in_specs=[row, pl.BlockSpec((1, D), lambda i, e: (0, 0)),
                  pl.BlockSpec((D, LANES), lambda i, e: (0, 0)),
                  pl.BlockSpec((1, LANES), lambda i, e: (0, 0)),
                  pl.BlockSpec((1, D, Fd), lambda i, e: (e, 0, 0)),
                  pl.BlockSpec((1, D, Fd), lambda i, e: (e, 0, 0)),
                  pl.BlockSpec((1, Fd, D), lambda i, e: (e, 0, 0))],
        out_specs=row,
        out_shape=jax.ShapeDtypeStruct((T, D), F32),
        scratch_shapes=[pltpu.VMEM((tm, D), BF16), pltpu.VMEM((tm, LANES), F32), pltpu.VMEM((tm, D), F32)],
        compiler_params=_cparams(("parallel", "arbitrary")),
        name="moe_dense",
    )(h, g.reshape(1, D), w_router, b_router, w_gate, w_up, w_down)


MOE_PAIRS = MOE_EPG * (MOE_EPG - 1) // 2
MOE_CLASSES = MOE_GROUPS * MOE_PAIRS


def _route_select(logits):
    lane = lax.broadcasted_iota(jnp.int32, logits.shape, 1)
    big = jnp.int32(1 << 20)
    is_g = (lane >= MOE_E) & (lane < MOE_E + MOE_GROUPS)
    glog = jnp.where(is_g, logits, NEG)
    gmax = jnp.max(glog, axis=-1, keepdims=True)
    gsel = jnp.min(jnp.where(is_g & (glog == gmax), lane, big), axis=-1, keepdims=True) - MOE_E
    in_g = (lane < MOE_E) & ((lane // MOE_EPG) == gsel)
    e1 = jnp.where(in_g, logits, NEG)
    v1 = jnp.max(e1, axis=-1, keepdims=True)
    i1 = jnp.min(jnp.where(in_g & (e1 == v1), lane, big), axis=-1, keepdims=True)
    in_g2 = in_g & (lane != i1)
    e2 = jnp.where(in_g2, logits, NEG)
    v2 = jnp.max(e2, axis=-1, keepdims=True)
    i2 = jnp.min(jnp.where(in_g2 & (e2 == v2), lane, big), axis=-1, keepdims=True)
    return gsel, i1, i2


def _moe_class_kernel(h_ref, g_ref, wr_ref, br_ref, o_ref):
    xn = _rms(h_ref[...], g_ref[...])
    gsel, i1, i2 = _route_select(_dot_hi(xn, wr_ref[...]) + br_ref[...])
    a = jnp.minimum(i1, i2) - gsel * MOE_EPG
    b = jnp.maximum(i1, i2) - gsel * MOE_EPG
    cls = gsel * MOE_PAIRS + (a * (2 * MOE_EPG - 1 - a)) // 2 + (b - a - 1)
    o_ref[...] = jnp.broadcast_to(cls, o_ref.shape)


def moe_class(h, g, w_router, b_router, *, tm=512):
    T, D = h.shape
    tm = min(tm, T)
    assert T % tm == 0
    return pl.pallas_call(
        _moe_class_kernel,
        grid=(T // tm,),
        in_specs=[pl.BlockSpec((tm, D), lambda i: (i, 0)), pl.BlockSpec((1, D), lambda i: (0, 0)),
                  pl.BlockSpec((D, LANES), lambda i: (0, 0)), pl.BlockSpec((1, LANES), lambda i: (0, 0))],
        out_specs=pl.BlockSpec((tm, LANES), lambda i: (i, 0)),
        out_shape=jax.ShapeDtypeStruct((T, LANES), jnp.int32),
        compiler_params=_cparams(("parallel",)),
        name="moe_class",
    )(h, g.reshape(1, D), w_router, b_router)


def _moe_plan(cls, tr):
    T = cls.shape[0]
    n_tiles = T // tr + MOE_CLASSES
    order = jnp.argsort(cls, stable=True).astype(jnp.int32)
    counts = jnp.sum(cls[:, None] == jnp.arange(MOE_CLASSES, dtype=jnp.int32)[None, :], axis=0, dtype=jnp.int32)
    tiles_per = (counts + tr - 1) // tr
    tile_end = jnp.cumsum(tiles_per)
    tile_ids = jnp.arange(n_tiles, dtype=jnp.int32)
    tcls = jnp.minimum(jnp.sum(tile_ids[:, None] >= tile_end[None, :], axis=1, dtype=jnp.int32), MOE_CLASSES - 1)
    k = tile_ids - (tile_end - tiles_per)[tcls]
    n_rows = jnp.where(tile_ids < tile_end[-1], jnp.clip(counts[tcls] - k * tr, 0, tr), 0).astype(jnp.int32)
    first = (jnp.cumsum(counts) - counts)[tcls] + k * tr
    rows = jnp.arange(tr, dtype=jnp.int32)[None, :]
    src = jnp.where(rows < n_rows[:, None], order[jnp.clip(first[:, None] + rows, 0, T - 1)], 0).astype(jnp.int32)
    pair = tcls % MOE_PAIRS
    grp = tcls // MOE_PAIRS
    pa = jnp.array([a for a in range(MOE_EPG) for b in range(a + 1, MOE_EPG)], jnp.int32)[pair]
    pb = jnp.array([b for a in range(MOE_EPG) for b in range(a + 1, MOE_EPG)], jnp.int32)[pair]
    return grp * MOE_EPG + pa, grp * MOE_EPG + pb, grp, n_rows, src


def _moe_routed_kernel(e1_ref, e2_ref, grp_ref, nrow_ref, src_ref, h_ref, g_ref, wr_ref, br_ref,
                       wg1_ref, wu1_ref, wd1_ref, wg2_ref, wu2_ref, wd2_ref, o_ref, x_ref, y_ref, sem_in, sem_out):
    i = pl.program_id(0)
    tr = x_ref.shape[0]
    n_rows = nrow_ref[i]

    def row_in(r):
        return pltpu.make_async_copy(h_ref.at[pl.ds(src_ref[0, 0, r], 1)], x_ref.at[pl.ds(r, 1)], sem_in)

    def row_out(r):
        return pltpu.make_async_copy(y_ref.at[pl.ds(r, 1)], o_ref.at[pl.ds(src_ref[0, 0, r], 1)], sem_out)

    def each(n, fn):
        def body8(j, carry):
            for u in range(SUBLANES):
                fn(j * SUBLANES + u)
            return carry

        def body1(r, carry):
            fn(r)
            return carry

        whole = n // SUBLANES
        lax.fori_loop(0, whole, body8, 0)
        if not isinstance(n, int) or n % SUBLANES:
            lax.fori_loop(whole * SUBLANES, n, body1, 0)

    @pl.when(n_rows > 0)
    def _():
        each(tr, lambda r: row_in(r).start())
        each(tr, lambda r: row_in(r).wait())
        x = x_ref[...]
        xn = _rms(x, g_ref[...])
        wr = wr_ref[...]
        xh, wh = xn.astype(BF16), wr.astype(BF16)
        xl, wl = (xn - xh.astype(F32)).astype(BF16), (wr - wh.astype(F32)).astype(BF16)
        logits = _dot(xh, wh) + _dot(xh, wl) + _dot(xl, wh) + br_ref[...]
        lane = lax.broadcasted_iota(jnp.int32, logits.shape, 1)

        def pick(idx):
            return jnp.sum(jnp.where(lane == idx, logits, 0.0), axis=-1, keepdims=True)

        la, lb, lg = pick(e1_ref[i]), pick(e2_ref[i]), pick(MOE_E + grp_ref[i])
        is_g = (lane >= MOE_E) & (lane < MOE_E + MOE_GROUPS)
        gprob = 1.0 / jnp.sum(jnp.where(is_g, jnp.exp(logits - lg), 0.0), axis=-1, keepdims=True)
        top = jnp.maximum(la, lb)
        pa, pb = jnp.exp(la - top), jnp.exp(lb - top)
        scale = gprob / (pa + pb)
        xb = xn.astype(BF16)
        hid1 = (_silu(_dot(xb, wg1_ref[0])) * _dot(xb, wu1_ref[0]) * (pa * scale)).astype(BF16)
        hid2 = (_silu(_dot(xb, wg2_ref[0])) * _dot(xb, wu2_ref[0]) * (pb * scale)).astype(BF16)
        y_ref[...] = x + _dot(hid1, wd1_ref[0]) + _dot(hid2, wd2_ref[0])
        each(n_rows, lambda r: row_out(r).start())
        each(n_rows, lambda r: row_out(r).wait())


def moe_routed(h, g, w_router, b_router, w_gate, w_up, w_down, *, tr=256):
    T, D = h.shape
    E, _, Fd = w_gate.shape
    assert T % tr == 0
    cls = moe_class(h, g, w_router, b_router)[:, 0]
    e1, e2, grp, n_rows, src = _moe_plan(cls, tr)
    n_tiles = src.shape[0]

    def wspec(shape, which):
        return pl.BlockSpec((1,) + shape, lambda i, e1, e2, grp, nr: ((e1, e2)[which][i], 0, 0))

    const = lambda shape: pl.BlockSpec(shape, lambda i, e1, e2, grp, nr: (0,) * len(shape))
    grid_spec = pltpu.PrefetchScalarGridSpec(
        num_scalar_prefetch=4, grid=(n_tiles,),
        in_specs=[pl.BlockSpec((1, 1, tr), lambda i, e1, e2, grp, nr: (i, 0, 0), memory_space=pltpu.SMEM),
                  pl.BlockSpec(memory_space=pl.ANY), const((1, D)), const((D, LANES)), const((1, LANES)),
                  wspec((D, Fd), 0), wspec((D, Fd), 0), wspec((Fd, D), 0),
                  wspec((D, Fd), 1), wspec((D, Fd), 1), wspec((Fd, D), 1)],
        out_specs=pl.BlockSpec(memory_space=pl.ANY),
        scratch_shapes=[pltpu.VMEM((tr, D), F32), pltpu.VMEM((tr, D), F32),
                        pltpu.SemaphoreType.DMA(()), pltpu.SemaphoreType.DMA(())])
    return pl.pallas_call(
        _moe_routed_kernel,
        grid_spec=grid_spec,
        out_shape=jax.ShapeDtypeStruct((T, D), F32),
        compiler_params=_cparams(("arbitrary",)),
        name="moe_routed",
    )(e1, e2, grp, n_rows, src.reshape(n_tiles, 1, tr), h, g.reshape(1, D), w_router, b_router,
      w_gate, w_up, w_down, w_gate, w_up, w_down)


def _conv_tile(xpad_ref, x, cw_ref, cb_ref, first):
    tl = x.shape[0]

    @pl.when(first)
    def _():
        xpad_ref[0:SUBLANES, :] = jnp.zeros((SUBLANES, x.shape[1]), F32)

    xpad_ref[SUBLANES:SUBLANES + tl, :] = x
    out = cb_ref[...] + x * cw_ref[CONV_W - 1:CONV_W, :]
    for k in range(CONV_W - 1):
        lo = SUBLANES - (CONV_W - 1) + k
        out = out + xpad_ref[lo:lo + tl, :] * cw_ref[k:k + 1, :]
    tail = xpad_ref[tl:tl + SUBLANES, :]
    xpad_ref[0:SUBLANES, :] = tail
    return out, tail[SUBLANES - (CONV_W - 1):, :]


def _lru_gates(xc, wa_ref, ba_ref, wi_ref, bi_ref, lam_ref):
    xcb = xc.astype(BF16)
    nb = wa_ref.shape[0]
    cw = wa_ref.shape[1]
    ra = jnp.concatenate([_dot(xcb[:, c * cw:(c + 1) * cw], wa_ref[c]) for c in range(nb)], axis=1)
    ia = jnp.concatenate([_dot(xcb[:, c * cw:(c + 1) * cw], wi_ref[c]) for c in range(nb)], axis=1)
    r = _sigmoid(ra + ba_ref[...])
    ig = _sigmoid(ia + bi_ref[...])
    log_a = (-LRU_C) * r * _softplus(-lam_ref[...])
    a = jnp.exp(log_a)
    b = jnp.sqrt(-jnp.tanh(log_a) * (a * a + 1.0)) * (ig * xc)
    return a, b


def _lru_seq_kernel(xb_ref, yb_ref, cw_ref, cb_ref, wa_ref, ba_ref, wi_ref, bi_ref, lam_ref,
                    o_ref, buf_ref, hl_ref, xpad_ref, a_ref, b_ref, hc_ref):
    t = pl.program_id(1)
    tl, W = xb_ref.shape[1], xb_ref.shape[2]
    xc, tail = _conv_tile(xpad_ref, xb_ref[0], cw_ref, cb_ref, t == 0)
    a, b = _lru_gates(xc, wa_ref, ba_ref, wi_ref, bi_ref, lam_ref)
    a_ref[...] = a
    b_ref[...] = b

    @pl.when(t == 0)
    def _():
        hc_ref[...] = jnp.zeros_like(hc_ref)

    row = lax.broadcasted_iota(jnp.int32, (SUBLANES, W), 0)

    def body(i, hc):
        r0 = pl.multiple_of(i * SUBLANES, SUBLANES)
        av = a_ref[pl.ds(r0, SUBLANES), :]
        bv = b_ref[pl.ds(r0, SUBLANES), :]
        for s in (1, 2, 4):
            keep = row >= s
            a_sh = pltpu.roll(av, s, 0)
            b_sh = pltpu.roll(bv, s, 0)
            bv = jnp.where(keep, av * b_sh + bv, bv)
            av = jnp.where(keep, av * a_sh, av)
        h = av * hc + bv
        b_ref[pl.ds(r0, SUBLANES), :] = h
        return jnp.broadcast_to(h[SUBLANES - 1:SUBLANES, :], (SUBLANES, W))

    hc = lax.fori_loop(0, tl // SUBLANES, body, hc_ref[...])
    hc_ref[...] = hc
    o_ref[0] = (b_ref[...] * _gelu_tanh(yb_ref[0])).astype(o_ref.dtype)
    buf_ref[0] = tail
    hl_ref[0] = hc[0:1, :]


def lru_seq(xy, conv_w, conv_b, wa_bd, b_a, wi_bd, b_i, lam, *, tl=256):
    B, L, W2 = xy.shape
    W = W2 // 2
    tl = min(tl, L)
    assert L % tl == 0 and tl % SUBLANES == 0
    vec = pl.BlockSpec((1, W), lambda b, t: (0, 0))
    wbd = pl.BlockSpec(wa_bd.shape, lambda b, t: (0, 0, 0))
    return pl.pallas_call(
        _lru_seq_kernel,
        grid=(B, L // tl),
        in_specs=[pl.BlockSpec((1, tl, W), lambda b, t: (b, t, 0)),
                  pl.BlockSpec((1, tl, W), lambda b, t: (b, t, 1)),
                  pl.BlockSpec((CONV_W, W), lambda b, t: (0, 0)), vec, wbd, vec, wbd, vec, vec],
        out_specs=(pl.BlockSpec((1, tl, W), lambda b, t: (b, t, 0)),
                   pl.BlockSpec((1, CONV_W - 1, W), lambda b, t: (b, 0, 0)),
                   pl.BlockSpec((1, 1, W), lambda b, t: (b, 0, 0))),
        out_shape=(jax.ShapeDtypeStruct((B, L, W), BF16),
                   jax.ShapeDtypeStruct((B, CONV_W - 1, W), F32),
                   jax.ShapeDtypeStruct((B, 1, W), F32)),
        scratch_shapes=[pltpu.VMEM((tl + SUBLANES, W), F32), pltpu.VMEM((tl, W), F32),
                        pltpu.VMEM((tl, W), F32), pltpu.VMEM((SUBLANES, W), F32)],
        compiler_params=_cparams(("parallel", "arbitrary")),
        name="lru_seq",
    )(xy, xy, conv_w, conv_b.reshape(1, W), wa_bd, b_a.reshape(1, W), wi_bd, b_i.reshape(1, W), lam.reshape(1, W))


def _lru_step_kernel(xb_ref, yb_ref, buf_ref, h0_ref, cw_ref, cb_ref, wa_ref, ba_ref, wi_ref, bi_ref, lam_ref,
                     o_ref, nbuf_ref, hl_ref):
    x = xb_ref[...]
    xc = cb_ref[...] + x * cw_ref[CONV_W - 1:CONV_W, :]
    for k in range(CONV_W - 1):
        xc = xc + buf_ref[k] * cw_ref[k:k + 1, :]
    a, b = _lru_gates(xc, wa_ref, ba_ref, wi_ref, bi_ref, lam_ref)
    h = a * h0_ref[...] + b
    o_ref[...] = (h * _gelu_tanh(yb_ref[...])).astype(o_ref.dtype)
    hl_ref[...] = h
    for k in range(CONV_W - 2):
        nbuf_ref[k] = buf_ref[k + 1]
    nbuf_ref[CONV_W - 2] = x


def lru_step(xy, buf_t, h0, conv_w, conv_b, wa_bd, b_a, wi_bd, b_i, lam):
    B, W2 = xy.shape
    W = W2 // 2
    vec = pl.BlockSpec((1, W), lambda i: (0, 0))
    wbd = pl.BlockSpec(wa_bd.shape, lambda i: (0, 0, 0))
    mat = pl.BlockSpec((B, W), lambda i: (0, 0))
    cube = pl.BlockSpec((CONV_W - 1, B, W), lambda i: (0, 0, 0))
    return pl.pallas_call(
        _lru_step_kernel,
        grid=(1,),
        in_specs=[mat, pl.BlockSpec((B, W), lambda i: (0, 1)), cube, mat,
                  pl.BlockSpec((CONV_W, W), lambda i: (0, 0)), vec, wbd, vec, wbd, vec, vec],
        out_specs=(mat, cube, mat),
        out_shape=(jax.ShapeDtypeStruct((B, W), BF16),
                   jax.ShapeDtypeStruct((CONV_W - 1, B, W), F32),
                   jax.ShapeDtypeStruct((B, W), F32)),
        compiler_params=_cparams(("arbitrary",)),
        name="lru_step",
    )(xy, xy, buf_t, h0, conv_w, conv_b.reshape(1, W), wa_bd, b_a.reshape(1, W), wi_bd, b_i.reshape(1, W),
      lam.reshape(1, W))


def _group_norm_gate(y, z, ng, n_groups):
    y = y * _silu(z)
    gw = y.shape[1] // n_groups
    outs = []
    for g in range(n_groups):
        yg = y[:, g * gw:(g + 1) * gw]
        ms = jnp.mean(yg * yg, axis=-1, keepdims=True)
        outs.append(yg * lax.rsqrt(ms + EPS))
    return jnp.concatenate(outs, axis=1) * ng


def _ssd_seq_kernel(z_ref, xbc_ref, dt_ref, cw_ref, cb_ref, dtb_ref, alog_ref, dsk_ref, ng_ref, ex_ref,
                    o_ref, buf_ref, st_ref, xpad_ref, s_ref):
    t = pl.program_id(1)
    Q = xbc_ref.shape[1]
    DI = z_ref.shape[2]
    GN = SSM_G * SSM_N
    RP = DI // SSM_G
    xc, tail = _conv_tile(xpad_ref, xbc_ref[0], cw_ref, cb_ref, t == 0)
    xc = _silu(xc)
    xs = xc[:, :DI]
    bm = xc[:, DI:DI + GN]
    cm = xc[:, DI + GN:]

    @pl.when(t == 0)
    def _():
        s_ref[...] = jnp.zeros_like(s_ref)

    dt = _softplus(dt_ref[0] + dtb_ref[...])
    a_neg = -jnp.exp(alog_ref[...])
    ri = lax.broadcasted_iota(jnp.int32, (Q, Q), 0)
    ci = lax.broadcasted_iota(jnp.int32, (Q, Q), 1)
    tri = ci <= ri
    acum = _dot_hi(tri.astype(F32), dt * a_neg)
    acum_t = acum.T
    ex = ex_ref[...]
    dt_e = _place_dot(dt, ex)
    acum_e = _place_dot(acum, ex)
    last_e = acum_e[Q - 1:Q, :]
    xdt = xs * dt_e
    xdtw = (xdt * jnp.exp(last_e - acum_e)).astype(BF16)
    xdt = xdt.astype(BF16)
    eacum = jnp.exp(acum_e)
    edec = jnp.exp(last_e)
    lane = lax.broadcasted_iota(jnp.int32, (Q, LANES), 1)
    lo = lane < SSM_P
    ys = []
    for g in range(SSM_G):
        cg = cm[:, g * SSM_N:(g + 1) * SSM_N].astype(BF16)
        bg32 = bm[:, g * SSM_N:(g + 1) * SSM_N]
        bg = bg32.astype(BF16)
        cb = _dot_nt(cg, bg)
        st = s_ref[g]
        yoff = _dot(cg, st.astype(BF16)) * eacum[:, g * RP:(g + 1) * RP]
        for pr in range(RP // LANES):
            ms = []
            for k in range(LANES // SSM_P):
                hd = (g * RP + pr * LANES) // SSM_P + k
                seg = acum[:, hd:hd + 1] - acum_t[hd:hd + 1, :]
                ms.append((cb * jnp.exp(jnp.where(tri, seg, NEG))).astype(BF16))
            c0 = g * RP + pr * LANES
            xp = xdt[:, c0:c0 + LANES]
            zero = jnp.zeros_like(xp)
            rhs = jnp.concatenate([jnp.where(lo, xp, zero), jnp.where(lo, zero, xp)], axis=0)
            ydiag = _dot(jnp.concatenate(ms, axis=1), rhs)
            ys.append(ydiag + yoff[:, pr * LANES:(pr + 1) * LANES])
        new = st * edec[:, g * RP:(g + 1) * RP] + _dot(bg32.T.astype(BF16), xdtw[:, g * RP:(g + 1) * RP])
        s_ref[g] = new
    y = jnp.concatenate(ys, axis=1) + xs * dsk_ref[...]
    o_ref[0] = _group_norm_gate(y, z_ref[0], ng_ref[...], SSM_G).astype(o_ref.dtype)
    buf_ref[0] = tail

    @pl.when(t == pl.num_programs(1) - 1)
    def _():
        hpg = RP // SSM_P
        for g in range(SSM_G):
            st_ref[0, g * hpg:(g + 1) * hpg] = s_ref[g].T.reshape(hpg, SSM_P, SSM_N)


def ssd_seq(z, xbc, dt, conv_w, conv_b, dt_bias, a_log, d_lanes, norm_g, expand, *, q=128):
    B, L, DI = z.shape
    C = xbc.shape[2]
    H = DI // SSM_P
    q = min(q, L)
    assert L % q == 0
    vecd = pl.BlockSpec((1, DI), lambda b, t: (0, 0))
    vecl = pl.BlockSpec((1, LANES), lambda b, t: (0, 0))
    return pl.pallas_call(
        _ssd_seq_kernel,
        grid=(B, L // q),
        in_specs=[pl.BlockSpec((1, q, DI), lambda b, t: (b, t, 0)),
                  pl.BlockSpec((1, q, C), lambda b, t: (b, t, 0)),
                  pl.BlockSpec((1, q, LANES), lambda b, t: (b, t, 0)),
                  pl.BlockSpec((CONV_W, C), lambda b, t: (0, 0)),
                  pl.BlockSpec((1, C), lambda b, t: (0, 0)),
                  vecl, vecl, vecd, vecd,
                  pl.BlockSpec((LANES, DI), lambda b, t: (0, 0))],
        out_specs=(pl.BlockSpec((1, q, DI), lambda b, t: (b, t, 0)),
                   pl.BlockSpec((1, CONV_W - 1, C), lambda b, t: (b, 0, 0)),
                   pl.BlockSpec((1, H, SSM_P, SSM_N), lambda b, t: (b, 0, 0, 0))),
        out_shape=(jax.ShapeDtypeStruct((B, L, DI), BF16),
                   jax.ShapeDtypeStruct((B, CONV_W - 1, C), F32),
                   jax.ShapeDtypeStruct((B, H, SSM_P, SSM_N), F32)),
        scratch_shapes=[pltpu.VMEM((q + SUBLANES, C), F32),
                        pltpu.VMEM((SSM_G, SSM_N, DI // SSM_G), F32)],
        compiler_params=_cparams(("parallel", "arbitrary")),
        name="ssd_seq",
    )(z, xbc, dt, conv_w, conv_b.reshape(1, C), dt_bias, a_log, d_lanes, norm_g.reshape(1, DI), expand)


def _ssd_step_kernel(z_ref, xbc_ref, dt_ref, buf_ref, h0_ref, cw_ref, cb_ref, dtb_ref, alog_ref, dsk_ref, ng_ref,
                     ex_ref, o_ref, nbuf_ref, hn_ref):
    DI = z_ref.shape[2]
    GN = SSM_G * SSM_N
    hpg = DI // SSM_P // SSM_G
    x = xbc_ref[0]
    buf = buf_ref[0]
    xc = cb_ref[...] + x * cw_ref[CONV_W - 1:CONV_W, :]
    for k in range(CONV_W - 1):
        xc = xc + buf[k:k + 1, :] * cw_ref[k:k + 1, :]
    nbuf_ref[0, 0:CONV_W - 2, :] = buf[1:CONV_W - 1, :]
    nbuf_ref[0, CONV_W - 2:CONV_W - 1, :] = x
    xc = _silu(xc)
    xs = xc[:, :DI]
    dt = _softplus(dt_ref[0] + dtb_ref[...])
    dta = dt * (-jnp.exp(alog_ref[...]))
    ex = ex_ref[...]
    dec = jnp.exp(dta)
    dt_e = _place_dot(jnp.broadcast_to(dt, (SUBLANES, LANES)), ex)[0:1, :]
    xdt = xs * dt_e
    RP = hpg * SSM_P
    eye = (lax.broadcasted_iota(jnp.int32, (RP, RP), 0) == lax.broadcasted_iota(jnp.int32, (RP, RP), 1))
    ys = []
    for g in range(SSM_G):
        brow = xc[:, DI + g * SSM_N:DI + (g + 1) * SSM_N]
        crow = xc[:, DI + GN + g * SSM_N:DI + GN + (g + 1) * SSM_N]
        xg = xdt[:, g * RP:(g + 1) * RP]
        xcol = jnp.sum(jnp.where(eye, jnp.broadcast_to(xg, (RP, RP)), 0.0), axis=-1, keepdims=True)
        news = []
        for r in range(hpg):
            hd = g * hpg + r
            new = h0_ref[0, hd] * dec[:, hd:hd + 1] + xcol[r * SSM_P:(r + 1) * SSM_P, :] * brow
            hn_ref[0, hd] = new
            news.append(new)
        new_g = jnp.concatenate(news, axis=0).astype(BF16)
        ys.append(_dot_nt(jnp.broadcast_to(crow, (SUBLANES, SSM_N)).astype(BF16), new_g)[0:1, :])
    y = jnp.concatenate(ys, axis=1) + xs * dsk_ref[...]
    o_ref[0] = _group_norm_gate(y, z_ref[0], ng_ref[...], SSM_G).astype(o_ref.dtype)


def ssd_step(z, xbc, dt, buf, h0, conv_w, conv_b, dt_bias, a_log, d_lanes, norm_g, expand):
    B, _, DI = z.shape
    C = xbc.shape[2]
    H = DI // SSM_P
    vecd = pl.BlockSpec((1, DI), lambda b: (0, 0))
    vecl = pl.BlockSpec((1, LANES), lambda b: (0, 0))
    st = pl.BlockSpec((1, H, SSM_P, SSM_N), lambda b: (b, 0, 0, 0))
    return pl.pallas_call(
        _ssd_step_kernel,
        grid=(B,),
        in_specs=[pl.BlockSpec((1, 1, DI), lambda b: (b, 0, 0)),
                  pl.BlockSpec((1, 1, C), lambda b: (b, 0, 0)),
                  pl.BlockSpec((1, 1, LANES), lambda b: (b, 0, 0)),
                  pl.BlockSpec((1, CONV_W - 1, C), lambda b: (b, 0, 0)), st,
                  pl.BlockSpec((CONV_W, C), lambda b: (0, 0)),
                  pl.BlockSpec((1, C), lambda b: (0, 0)),
                  vecl, vecl, vecd, vecd,
                  pl.BlockSpec((LANES, DI), lambda b: (0, 0))],
        out_specs=(pl.BlockSpec((1, 1, DI), lambda b: (b, 0, 0)),
                   pl.BlockSpec((1, CONV_W - 1, C), lambda b: (b, 0, 0)), st),
        out_shape=(jax.ShapeDtypeStruct((B, 1, DI), BF16),
                   jax.ShapeDtypeStruct((B, CONV_W - 1, C), F32),
                   jax.ShapeDtypeStruct((B, H, SSM_P, SSM_N), F32)),
        compiler_params=_cparams(("parallel",)),
        name="ssd_step",
    )(z, xbc, dt, buf, h0, conv_w, conv_b.reshape(1, C), dt_bias, a_log, d_lanes, norm_g.reshape(1, DI), expand)


def _split_dot(x, m):
    hi = x.astype(BF16)
    lo = (x - hi.astype(F32)).astype(BF16)
    return _dot(hi, m) + _dot(lo, m)


LOG2E = 1.4426950408889634
N_PIECES = 3
PAIR = LANES // ATT_HD


def _fox_prep_kernel(q_ref, k_ref, v_ref, fl_ref, bf_ref, qg_ref, kg_ref, hs_ref, he_ref, pl_ref,
                     qo_ref, ko_ref, lf_ref, kx_ref, vt_ref, kt32_ref, vt32_ref, carry_ref):
    t = pl.program_id(1)
    tl = q_ref.shape[1]
    D = q_ref.shape[2]

    def head_norm(x, g):
        ss = _split_dot(x * x, hs_ref[...])
        inv = lax.rsqrt(ss * (1.0 / ATT_HD) + EPS)
        return x * _split_dot(inv, he_ref[...]) * g

    qo_ref[0] = (head_norm(q_ref[0], qg_ref[...]) * (ATT_HD ** -0.5 * LOG2E)).astype(qo_ref.dtype)
    kn = head_norm(k_ref[0], kg_ref[...])
    ko_ref[0] = kn
    kt32_ref[0] = kn.T
    vt = v_ref[0].T
    vt32_ref[0] = vt
    vt_ref[0] = vt.astype(BF16)
    z = fl_ref[0] + bf_ref[...]
    logf = jnp.minimum(z, 0.0) - jnp.log1p(jnp.exp(-jnp.abs(z)))
    lf_ref[0] = logf

    @pl.when(t == 0)
    def _():
        carry_ref[...] = jnp.zeros_like(carry_ref)

    ri = lax.broadcasted_iota(jnp.int32, (tl, tl), 0)
    ci = lax.broadcasted_iota(jnp.int32, (tl, tl), 1)
    c = _dot_hi((ci <= ri).astype(F32), logf) + carry_ref[0:1, :]
    carry_ref[...] = jnp.broadcast_to(c[tl - 1:tl, :], carry_ref.shape)
    rest = c * (-LOG2E)
    extra = jnp.zeros((tl, D), F32)
    for j in range(N_PIECES):
        piece = rest.astype(BF16)
        rest = rest - piece.astype(F32)
        extra = extra + _dot(piece, pl_ref[j])
    knb = kn.astype(BF16)
    extra = extra.astype(BF16)
    kx_ref[0] = jnp.concatenate(
        [x[:, p * LANES:(p + 1) * LANES] for p in range(D // LANES) for x in (knb, extra)], axis=1)


def fox_prep(q, k, v, fl, b_f, q_g, k_g, head_sum, head_expand, place, *, tl=256):
    B, L, D = q.shape
    tl = min(tl, L)
    assert L % tl == 0
    row = pl.BlockSpec((1, tl, D), lambda b, t: (b, t, 0))
    colt = pl.BlockSpec((1, D, tl), lambda b, t: (b, 0, t))
    nar = pl.BlockSpec((1, tl, LANES), lambda b, t: (b, t, 0))
    vecd = pl.BlockSpec((1, D), lambda b, t: (0, 0))
    return pl.pallas_call(
        _fox_prep_kernel,
        grid=(B, L // tl),
        in_specs=[row, row, row, nar, pl.BlockSpec((1, LANES), lambda b, t: (0, 0)), vecd, vecd,
                  pl.BlockSpec((D, LANES), lambda b, t: (0, 0)),
                  pl.BlockSpec((LANES, D), lambda b, t: (0, 0)),
                  pl.BlockSpec((N_PIECES, LANES, D), lambda b, t: (0, 0, 0))],
        out_specs=(row, row, nar, pl.BlockSpec((1, tl, 2 * D), lambda b, t: (b, t, 0)),
                   colt, colt, colt),
        out_shape=(jax.ShapeDtypeStruct((B, L, D), BF16),
                   jax.ShapeDtypeStruct((B, L, D), F32),
                   jax.ShapeDtypeStruct((B, L, LANES), F32),
                   jax.ShapeDtypeStruct((B, L, 2 * D), BF16),
                   jax.ShapeDtypeStruct((B, D, L), BF16),
                   jax.ShapeDtypeStruct((B, D, L), F32),
                   jax.ShapeDtypeStruct((B, D, L), F32)),
        scratch_shapes=[pltpu.VMEM((SUBLANES, LANES), F32)],
        compiler_params=_cparams(("parallel", "arbitrary")),
        name="fox_prep",
    )(q, k, v, fl, b_f, q_g, k_g, head_sum, head_expand, place)


def _fox_attn_kernel(q_ref, kx_ref, vt_ref, g_ref, o_ref, qx_ref, m_ref, l_ref, acc_ref):
    qi = pl.program_id(2)
    tq = q_ref.shape[1]
    q = q_ref[0].astype(F32)
    lane = lax.broadcasted_iota(jnp.int32, (tq, LANES), 1)
    for k in range(PAIR):
        mine = (lane >= k * ATT_HD) & (lane < (k + 1) * ATT_HD)
        pick = (lane >= k * N_PIECES) & (lane < (k + 1) * N_PIECES)
        qx_ref[k] = jnp.concatenate([jnp.where(mine, q, 0.0), jnp.where(pick, 1.0, 0.0)], axis=1).astype(BF16)
    m_ref[...] = jnp.full_like(m_ref, NEG)
    l_ref[...] = jnp.zeros_like(l_ref)
    acc_ref[...] = jnp.zeros_like(acc_ref)

    def block(j, diagonal):
        r0 = pl.multiple_of(j * tq, tq)
        kb = kx_ref[0, pl.ds(r0, tq), :]
        vt = vt_ref[0, :, pl.ds(r0, tq)]
        for k in range(PAIR):
            s = _dot_nt(kb, qx_ref[k])
            if diagonal:
                rows = lax.broadcasted_iota(jnp.int32, (tq, tq), 0)
                cols = lax.broadcasted_iota(jnp.int32, (tq, tq), 1)
                s = jnp.where(rows <= cols, s, NEG)
            m_old = m_ref[k]
            m_new = jnp.maximum(m_old, jnp.max(s, axis=0, keepdims=True))
            alpha = jnp.exp2(m_old - m_new)
            p = jnp.exp2(s - m_new)
            l_ref[k] = l_ref[k] * alpha + jnp.sum(p, axis=0, keepdims=True)
            m_ref[k] = m_new
            acc_ref[k] = acc_ref[k] * alpha + _dot(vt[k * ATT_HD:(k + 1) * ATT_HD, :], p.astype(BF16))

    def body(j, carry):
        block(j, False)
        return carry

    lax.fori_loop(0, qi, body, 0)
    block(qi, True)
    o = jnp.concatenate([acc_ref[k] / l_ref[k] for k in range(PAIR)], axis=0)
    o_ref[0] = (o.T * _sigmoid(g_ref[0])).astype(o_ref.dtype)


def fox_attn(q, kx, vt, g, *, tq=512):
    B, L, D = q.shape
    tq = min(tq, L)
    assert L % tq == 0
    qs = pl.BlockSpec((1, tq, LANES), lambda b, p, i: (b, i, p))
    return pl.pallas_call(
        _fox_attn_kernel,
        grid=(B, D // LANES, L // tq),
        in_specs=[qs,
                  pl.BlockSpec((1, L, 2 * LANES), lambda b, p, i: (b, 0, p)),
                  pl.BlockSpec((1, LANES, L), lambda b, p, i: (b, p, 0)),
                  qs],
        out_specs=qs,
        out_shape=jax.ShapeDtypeStruct((B, L, D), BF16),
        scratch_shapes=[pltpu.VMEM((PAIR, tq, 2 * LANES), BF16), pltpu.VMEM((PAIR, 1, tq), F32),
                        pltpu.VMEM((PAIR, 1, tq), F32), pltpu.VMEM((PAIR, ATT_HD, tq), F32)],
        compiler_params=_cparams(("parallel", "parallel", "arbitrary")),
        name="fox_attn",
    )(q, kx, vt, g)


def _lanes_to_sublanes(row):
    n = row.shape[1]
    eye = lax.broadcasted_iota(jnp.int32, (n, n), 0) == lax.broadcasted_iota(jnp.int32, (n, n), 1)
    return jnp.sum(jnp.where(eye, jnp.broadcast_to(row, (n, n)), 0.0), axis=1, keepdims=True)


def _sublanes_to_lanes(col):
    n = col.shape[0]
    eye = lax.broadcasted_iota(jnp.int32, (n, n), 0) == lax.broadcasted_iota(jnp.int32, (n, n), 1)
    return jnp.sum(jnp.where(eye, jnp.broadcast_to(col, (n, n)), 0.0), axis=0, keepdims=True)


def _fox_decode_kernel(pt_ref, q_ref, kn_ref, vn_ref, lfn_ref, g_ref, *rest, npg):
    k_refs = rest[:npg]
    v_refs = rest[npg:2 * npg]
    lf_refs = rest[2 * npg:3 * npg]
    o_ref = rest[3 * npg]
    qb_ref, m_ref, l_ref, acc_ref, coff_ref = rest[3 * npg + 1:]
    s_id = pl.program_id(1)
    H, hd, ps = k_refs[0].shape[1:]

    @pl.when(s_id == 0)
    def _():
        q = q_ref[0]
        for h in range(H):
            qb_ref[h] = jnp.broadcast_to(_lanes_to_sublanes(q[h:h + 1, :]), (hd, ps))
        m_ref[...] = jnp.full_like(m_ref, NEG)
        l_ref[...] = jnp.zeros_like(l_ref)
        acc_ref[...] = jnp.zeros_like(acc_ref)
        coff_ref[...] = jnp.zeros_like(coff_ref)

    ri = lax.broadcasted_iota(jnp.int32, (ps, ps), 0)
    ci = lax.broadcasted_iota(jnp.int32, (ps, ps), 1)
    upper = (ri <= ci).astype(F32)

    for i in range(npg):
        qk = jnp.concatenate([jnp.sum(k_refs[i][0, h] * qb_ref[h], axis=0, keepdims=True) for h in range(H)], axis=0)
        c = _dot_hi(lf_refs[i][0], upper) + coff_ref[...]
        coff_ref[...] = c[:, ps - 1:ps]
        s = qk - c * LOG2E
        m_old = m_ref[...]
        m_new = jnp.maximum(m_old, jnp.max(s, axis=1, keepdims=True))
        alpha = jnp.exp2(m_old - m_new)
        p = jnp.exp2(s - m_new)
        l_ref[...] = l_ref[...] * alpha + jnp.sum(p, axis=1, keepdims=True)
        m_ref[...] = m_new
        for h in range(H):
            acc_ref[h] = acc_ref[h] * alpha[h:h + 1, :] + p[h:h + 1, :] * v_refs[i][0, h]

    @pl.when(s_id == pl.num_programs(1) - 1)
    def _():
        s = (jnp.sum(q_ref[0] * kn_ref[0], axis=1, keepdims=True)
             - (coff_ref[...] + _lanes_to_sublanes(lfn_ref[0][:, 0:H])) * LOG2E)
        m_old = m_ref[...]
        m_new = jnp.maximum(m_old, s)
        alpha = jnp.exp2(m_old - m_new)
        p = jnp.exp2(s - m_new)
        l = l_ref[...] * alpha + p
        past = jnp.concatenate(
            [_sublanes_to_lanes(jnp.sum(acc_ref[h], axis=1, keepdims=True)) for h in range(H)], axis=0)
        o = (past * alpha + p * vn_ref[0]) / l
        o_ref[0] = (o * _sigmoid(g_ref[0])).astype(o_ref.dtype)


def fox_decode(q, k_new, v_new, lf_new, g, cache_kt, cache_vt, cache_lft, page_table, *, npg=4):
    B, H, hd = q.shape
    n_pages = page_table.shape[1]
    ps = cache_kt.shape[3]
    npg = min(npg, n_pages)
    assert n_pages % npg == 0
    steps = n_pages // npg
    row = pl.BlockSpec((1, H, hd), lambda b, s, pt: (b, 0, 0))

    def page(i, *tail):
        return pl.BlockSpec((1,) + tail, lambda b, s, pt: (pt[b * n_pages + s * npg + i],) + (0,) * len(tail))

    in_specs = ([row, row, row, pl.BlockSpec((1, 1, LANES), lambda b, s, pt: (b, 0, 0)), row]
                + [page(i, H, hd, ps) for i in range(npg)] + [page(i, H, hd, ps) for i in range(npg)]
                + [page(i, H, ps) for i in range(npg)])
    grid_spec = pltpu.PrefetchScalarGridSpec(
        num_scalar_prefetch=1, grid=(B, steps), in_specs=in_specs, out_specs=row,
        scratch_shapes=[pltpu.VMEM((H, hd, ps), F32), pltpu.VMEM((H, 1), F32), pltpu.VMEM((H, 1), F32),
                        pltpu.VMEM((H, hd, ps), F32), pltpu.VMEM((H, 1), F32)])
    return pl.pallas_call(
        functools.partial(_fox_decode_kernel, npg=npg),
        grid_spec=grid_spec,
        out_shape=jax.ShapeDtypeStruct((B, H, hd), BF16),
        compiler_params=_cparams(("parallel", "arbitrary")),
        name="fox_decode",
    )(page_table.reshape(-1), q, k_new, v_new, lf_new, g,
      *([cache_kt] * npg), *([cache_vt] * npg), *([cache_lft] * npg))


def _block_diag_chunks(w, per_chunk):
    nblk, bw, _ = w.shape
    w = w.reshape(nblk // per_chunk, per_chunk, bw, bw)
    eye = jnp.eye(per_chunk, dtype=w.dtype)
    out = jnp.einsum('cpij,pq->cpiqj', w, eye)
    return out.reshape(nblk // per_chunk, per_chunk * bw, per_chunk * bw)


def _head_expand(n_heads, width):
    r = jnp.arange(LANES)[:, None]
    c = jnp.arange(n_heads * width)[None, :] // width
    return (r == c).astype(F32)


def _pad_lanes(v):
    return jnp.pad(v.reshape(1, -1), ((0, 0), (0, LANES - v.size)))


MOE_ROUTED_MIN_TOKENS = 2048


def _moe_and_ple(h, p_i, i, prm, last):
    moe = moe_routed if h.shape[0] >= MOE_ROUTED_MIN_TOKENS else moe_dense
    h = moe(h, prm['norm_ffn'][i], prm['moe_w_router'][i], prm['moe_b_router'][i],
            prm['moe_w_gate'][i], prm['moe_w_up'][i], prm['moe_w_down'][i])
    return ple(h, p_i, prm['norm_ple'][i], prm['ple_w_gate'][i], prm['ple_w_proj'][i],
               prm['norm_final'] if last else None)


def _lru_layer_prompt(h, B, L, j, g, prm):
    D = h.shape[1]
    xy = norm_matmul(h, g, prm['lru_w_in'][j])
    gated, buf, hl = lru_seq(xy.reshape(B, L, -1), prm['lru_conv_w'][j], prm['lru_conv_b'][j],
                             prm['lru_wa_bd'][j], prm['lru_b_a'][j], prm['lru_wi_bd'][j], prm['lru_b_i'][j],
                             prm['lru_lambda'][j])
    h = matmul_res(gated.reshape(B * L, -1), prm['lru_w_out'][j], h)
    return h, buf, hl.reshape(B, D)


def _lru_layer_sample(h, buf, h0, j, g, prm):
    xy = norm_matmul(h, g, prm['lru_w_in'][j])
    gated, nbuf, hl = lru_step(xy, jnp.swapaxes(buf, 0, 1), h0, prm['lru_conv_w'][j], prm['lru_conv_b'][j],
                               prm['lru_wa_bd'][j], prm['lru_b_a'][j], prm['lru_wi_bd'][j], prm['lru_b_i'][j],
                               prm['lru_lambda'][j])
    h = matmul_res(gated, prm['lru_w_out'][j], h)
    return h, jnp.swapaxes(nbuf, 0, 1), hl


def _ssd_proj(h, g, j, prm):
    z = norm_matmul(h, g, prm['ssm_w_z'][j])
    xbc = norm_matmul(h, g, prm['ssm_w_xbc'][j])
    dt = norm_matmul_hi(h, g, prm['ssm_w_dt'][j])
    return z, xbc, dt


def _ssd_args(j, prm):
    return (prm['ssm_conv_w'][j], prm['ssm_conv_b'][j], prm['ssm_dt_bias'][j], prm['ssm_a_log'][j],
            prm['ssm_d_lanes'][j], prm['ssm_norm'][j], prm['ssm_expand'])


def _ssd_layer_prompt(h, B, L, j, g, prm):
    z, xbc, dt = _ssd_proj(h, g, j, prm)
    y, buf, st = ssd_seq(z.reshape(B, L, -1), xbc.reshape(B, L, -1), dt.reshape(B, L, -1), *_ssd_args(j, prm))
    return matmul_res(y.reshape(B * L, -1), prm['ssm_w_out'][j], h), buf, st


def _ssd_layer_sample(h, buf, h0, j, g, prm):
    B = h.shape[0]
    z, xbc, dt = _ssd_proj(h, g, j, prm)
    y, nbuf, st = ssd_step(z.reshape(B, 1, -1), xbc.reshape(B, 1, -1), dt.reshape(B, 1, -1), buf, h0,
                           *_ssd_args(j, prm))
    return matmul_res(y.reshape(B, -1), prm['ssm_w_out'][j], h), nbuf, st


def _fox_proj(h, g, j, prm):
    q = norm_matmul(h, g, prm['fox_w_q'][j])
    k = norm_matmul(h, g, prm['fox_w_k'][j])
    v = norm_matmul(h, g, prm['fox_w_v'][j])
    og = norm_matmul(h, g, prm['fox_w_g'][j])
    fl = norm_matmul_hi(h, g, prm['fox_w_f'][j])
    return q, k, v, og, fl


def _fox_prep_args(j, prm):
    return (prm['fox_b_f'][j], prm['fox_q_norm'][j], prm['fox_k_norm'][j], prm['fox_head_sum'],
            prm['fox_head_expand'], prm['fox_place'])


def _fox_layer_prompt(h, B, L, j, g, prm, tq=512):
    q, k, v, og, fl = _fox_proj(h, g, j, prm)
    shp = (B, L, -1)
    qs, _, logf, kx, vt, kt32, vt32 = fox_prep(q.reshape(shp), k.reshape(shp), v.reshape(shp), fl.reshape(shp),
                                               *_fox_prep_args(j, prm))
    o = fox_attn(qs, kx, vt, og.reshape(shp), tq=tq)

    def per_head(xt):
        return jnp.transpose(xt.reshape(B, -1, ATT_HD, L), (0, 3, 1, 2))

    return matmul_res(o.reshape(B * L, -1), prm['fox_w_out'][j], h), per_head(kt32), per_head(vt32), logf


def _fox_layer_sample(h, cache, n_phys, page_table, j, g, prm):
    B, D = h.shape
    q, k, v, og, fl = _fox_proj(h, g, j, prm)
    one = (1, B, -1)
    qs, kn, logf = fox_prep(q.reshape(one), k.reshape(one), v.reshape(one), fl.reshape(one),
                            *_fox_prep_args(j, prm))[:3]
    ck, cv, clf = cache
    heads = (B, D // ATT_HD, ATT_HD)
    tok = (B, 1, -1)
    o = fox_decode(qs.astype(F32).reshape(heads), kn.reshape(heads), v.reshape(heads), logf.reshape(tok),
                   og.reshape(heads), ck, cv, clf, page_table + j * n_phys)
    return matmul_res(o.reshape(B, D), prm['fox_w_out'][j], h), kn.reshape(tok), v.reshape(tok), logf.reshape(tok)


def _prepare_params(raw):
    prm = dict(raw)
    D = raw['norm_final'].shape[0]
    for name in ('lru_w_in', 'lru_w_out', 'ssm_w_out', 'fox_w_out', 'moe_w_gate', 'moe_w_up', 'moe_w_down',
                 'ple_w_proj', 'ple_w_gate'):
        prm[name] = raw[name].astype(BF16)
    per = 2 * LANES // (D // LRU_BLOCKS)
    prm['lru_wa_bd'] = jax.vmap(lambda w: _block_diag_chunks(w, per))(raw['lru_w_a']).astype(BF16)
    prm['lru_wi_bd'] = jax.vmap(lambda w: _block_diag_chunks(w, per))(raw['lru_w_i']).astype(BF16)
    prm['lru_b_a'] = raw['lru_b_a'].reshape(raw['lru_b_a'].shape[0], -1)
    prm['lru_b_i'] = raw['lru_b_i'].reshape(raw['lru_b_i'].shape[0], -1)
    n_h = raw['ssm_a_log'].shape[1]
    di = n_h * SSM_P
    w = raw['ssm_w_in']
    conv_dim = raw['ssm_conv_w'].shape[2]
    prm['ssm_w_z'] = w[:, :, :di].astype(BF16)
    prm['ssm_w_xbc'] = w[:, :, di:di + conv_dim].astype(BF16)
    prm['ssm_w_dt'] = jax.vmap(lambda m: _pad_cols(m, LANES))(w[:, :, di + conv_dim:])
    prm['ssm_dt_bias'] = jax.vmap(_pad_lanes)(raw['ssm_dt_bias'])
    prm['ssm_a_log'] = jax.vmap(_pad_lanes)(raw['ssm_a_log'])
    prm['ssm_d_lanes'] = jnp.repeat(raw['ssm_d'], SSM_P, axis=1)[:, None, :]
    prm['ssm_expand'] = _head_expand(n_h, SSM_P).astype(BF16)
    w = raw['fox_w_in']
    for n, name in enumerate(('fox_w_q', 'fox_w_k', 'fox_w_v', 'fox_w_g')):
        prm[name] = w[:, :, n * D:(n + 1) * D].astype(BF16)
    prm['fox_w_f'] = jax.vmap(lambda m: _pad_cols(m, LANES))(w[:, :, 4 * D:])
    prm['fox_b_f'] = jax.vmap(_pad_lanes)(raw['fox_b_f'])
    n_ah = D // ATT_HD
    prm['fox_q_norm'] = jnp.tile(raw['fox_q_norm'], (1, n_ah))[:, None, :]
    prm['fox_k_norm'] = jnp.tile(raw['fox_k_norm'], (1, n_ah))[:, None, :]
    prm['fox_head_expand'] = _head_expand(n_ah, ATT_HD)
    prm['fox_head_sum'] = prm['fox_head_expand'].T.astype(BF16)
    hh = jnp.arange(LANES)[None, :, None]
    jj = jnp.arange(N_PIECES)[:, None, None]
    col = jnp.arange(D)[None, None, :]
    prm['fox_place'] = ((hh < n_ah) & (col == LANES * (hh // PAIR) + N_PIECES * (hh % PAIR) + jj)).astype(BF16)
    prm['moe_w_router'] = jax.vmap(lambda we, wg: _pad_cols(jnp.concatenate([we, wg], axis=1), LANES))(
        raw['moe_w_expert'], raw['moe_w_group'])
    prm['moe_b_router'] = jax.vmap(lambda be, bg: _pad_lanes(jnp.concatenate([be, bg])))(
        raw['moe_b_expert'], raw['moe_b_group'])
    return prm


def kernel(x_prompt, x_sample, state_lru_h, state_lru_conv, state_ssm_h, state_ssm_conv, cache_k, cache_v, cache_logf, page_table, p_prompt, p_sample, lru_w_in, lru_conv_w, lru_conv_b, lru_w_a, lru_b_a, lru_w_i, lru_b_i, lru_lambda, lru_w_out, ssm_w_in, ssm_conv_w, ssm_conv_b, ssm_dt_bias, ssm_a_log, ssm_d, ssm_norm, ssm_w_out, fox_w_in, fox_b_f, fox_q_norm, fox_k_norm, fox_w_out, moe_w_group, moe_b_group, moe_w_expert, moe_b_expert, moe_w_gate, moe_w_up, moe_w_down, ple_w_proj, ple_w_gate, norm_mix, norm_ffn, norm_ple, norm_final):
    prm = _prepare_params(dict(
        lru_w_in=lru_w_in, lru_conv_w=lru_conv_w, lru_conv_b=lru_conv_b, lru_w_a=lru_w_a, lru_b_a=lru_b_a,
        lru_w_i=lru_w_i, lru_b_i=lru_b_i, lru_lambda=lru_lambda, lru_w_out=lru_w_out, ssm_w_in=ssm_w_in,
        ssm_conv_w=ssm_conv_w, ssm_conv_b=ssm_conv_b, ssm_dt_bias=ssm_dt_bias, ssm_a_log=ssm_a_log, ssm_d=ssm_d,
        ssm_norm=ssm_norm, ssm_w_out=ssm_w_out, fox_w_in=fox_w_in, fox_b_f=fox_b_f, fox_q_norm=fox_q_norm,
        fox_k_norm=fox_k_norm, fox_w_out=fox_w_out, moe_w_group=moe_w_group, moe_b_group=moe_b_group,
        moe_w_expert=moe_w_expert, moe_b_expert=moe_b_expert, moe_w_gate=moe_w_gate, moe_w_up=moe_w_up,
        moe_w_down=moe_w_down, ple_w_proj=ple_w_proj, ple_w_gate=ple_w_gate, norm_mix=norm_mix,
        norm_ffn=norm_ffn, norm_ple=norm_ple, norm_final=norm_final))
    depth = norm_mix.shape[0]
    B, L, D = x_prompt.shape
    Bs = x_sample.shape[0]
    n_mix = 3
    att_h = D // ATT_HD
    n_phys, page = cache_k.shape[1], cache_k.shape[2]
    cache = (jnp.transpose(cache_k, (0, 1, 3, 4, 2)).reshape(-1, att_h, ATT_HD, page),
             jnp.transpose(cache_v, (0, 1, 3, 4, 2)).reshape(-1, att_h, ATT_HD, page),
             jnp.transpose(cache_logf, (0, 1, 3, 2)).reshape(-1, att_h, page))

    hp = x_prompt.reshape(B * L, D)
    hs = x_sample.reshape(Bs, D)
    outs = {k: [] for k in ('lru_h_p', 'lru_h_s', 'lru_c_p', 'lru_c_s', 'ssm_h_p', 'ssm_h_s', 'ssm_c_p', 'ssm_c_s',
                            'k_p', 'k_s', 'v_p', 'v_s', 'lf_p', 'lf_s')}
    yp = ys = None
    for i in range(depth):
        j = i // n_mix
        g = prm['norm_mix'][i]
        if i % n_mix == 0:
            hp, buf, hl = _lru_layer_prompt(hp, B, L, j, g, prm)
            outs['lru_c_p'].append(buf)
            outs['lru_h_p'].append(hl)
            hs, buf, hl = _lru_layer_sample(hs, state_lru_conv[j], state_lru_h[j], j, g, prm)
            outs['lru_c_s'].append(buf)
            outs['lru_h_s'].append(hl)
        elif i % n_mix == 1:
            hp, buf, st = _ssd_layer_prompt(hp, B, L, j, g, prm)
            outs['ssm_c_p'].append(buf)
            outs['ssm_h_p'].append(st)
            hs, buf, st = _ssd_layer_sample(hs, state_ssm_conv[j], state_ssm_h[j], j, g, prm)
            outs['ssm_c_s'].append(buf)
            outs['ssm_h_s'].append(st)
        else:
            hp, k, v, lf = _fox_layer_prompt(hp, B, L, j, g, prm)
            outs['k_p'].append(k)
            outs['v_p'].append(v)
            outs['lf_p'].append(lf[:, :, :att_h])
            hs, k, v, lf = _fox_layer_sample(hs, cache, n_phys, page_table, j, g, prm)
            outs['k_s'].append(k.reshape(Bs, 1, att_h, ATT_HD))
            outs['v_s'].append(v.reshape(Bs, 1, att_h, ATT_HD))
            outs['lf_s'].append(lf[:, :, :att_h])
        last = i == depth - 1
        hp = _moe_and_ple(hp, p_prompt[i].reshape(B * L, -1), i, prm, last)
        hs = _moe_and_ple(hs, p_sample[i].reshape(Bs, -1), i, prm, last)
        if last:
            hp, yp = hp
            hs, ys = hs
    st = {k: jnp.stack(v) for k, v in outs.items()}
    return (yp.reshape(B, L, D), ys.reshape(Bs, 1, D), st['lru_h_p'], st['lru_h_s'], st['lru_c_p'], st['lru_c_s'],
            st['ssm_h_p'], st['ssm_h_s'], st['ssm_c_p'], st['ssm_c_s'], st['k_p'], st['k_s'], st['v_p'], st['v_s'],
            st['lf_p'], st['lf_s'])
```

```python
import functools

import jax
import jax.numpy as jnp
from jax import lax
from jax.experimental import pallas as pl
from jax.experimental.pallas import tpu as pltpu

F32 = jnp.float32
BF16 = jnp.bfloat16
HI = lax.Precision.HIGHEST

EPS = 1e-6
CONV_W = 4
LANES = 128
SUBLANES = 8
LRU_C = 8.0
LRU_BLOCKS = 16
SSM_P = 64
SSM_G = 8
SSM_N = 128
ATT_HD = 64
MOE_GROUPS = 4
MOE_EPG = 4
MOE_E = MOE_GROUPS * MOE_EPG
NEG = -1e30
MIB = 1024 * 1024


def _cparams(sem, vmem_mib=48):
    return pltpu.CompilerParams(dimension_semantics=sem, vmem_limit_bytes=vmem_mib * MIB)


def _rms(x, g):
    ms = jnp.mean(x * x, axis=-1, keepdims=True)
    return x * lax.rsqrt(ms + EPS) * g


def _softplus(z):
    return jnp.maximum(z, 0.0) + jnp.log1p(jnp.exp(-jnp.abs(z)))


def _sigmoid(z):
    return 1.0 / (1.0 + jnp.exp(-z))


def _silu(z):
    return z * _sigmoid(z)


def _gelu_tanh(z):
    c = 0.7978845608028654
    return 0.5 * z * (1.0 + jnp.tanh(c * (z + 0.044715 * (z * z * z))))


def _dot(a, b):
    return jnp.dot(a, b, preferred_element_type=F32)


def _dot_hi(a, b):
    return jnp.dot(a, b, preferred_element_type=F32, precision=HI)


def _dot_nt(a, b):
    return lax.dot_general(a, b, (((1,), (1,)), ((), ())), preferred_element_type=F32)


def _dot_nt_hi(a, b):
    return lax.dot_general(a, b, (((1,), (1,)), ((), ())), preferred_element_type=F32, precision=HI)


def _place_dot(x, m, pieces=3):
    out = None
    rest = x
    for _ in range(pieces):
        piece = rest.astype(BF16)
        rest = rest - piece.astype(F32)
        out = _dot(piece, m) if out is None else out + _dot(piece, m)
    return out


def _pad_cols(w, n):
    return jnp.pad(w, ((0, 0), (0, n - w.shape[1])))


def _norm_matmul_kernel(x_ref, g_ref, w_ref, o_ref, xn_ref):
    @pl.when(pl.program_id(1) == 0)
    def _():
        xn_ref[...] = _rms(x_ref[...], g_ref[...]).astype(xn_ref.dtype)

    o_ref[...] = _dot(xn_ref[...], w_ref[...]).astype(o_ref.dtype)


def _norm_matmul_hi_kernel(x_ref, g_ref, w_ref, o_ref):
    o_ref[...] = _dot_hi(_rms(x_ref[...], g_ref[...]), w_ref[...])


def norm_matmul(x, g, w, *, tm=1024, tn=1024, out_dtype=F32):
    T, D = x.shape
    N = w.shape[1]
    tm, tn = min(tm, T), min(tn, N)
    assert T % tm == 0 and N % tn == 0
    return pl.pallas_call(
        _norm_matmul_kernel,
        grid=(T // tm, N // tn),
        in_specs=[pl.BlockSpec((tm, D), lambda i, j: (i, 0)),
                  pl.BlockSpec((1, D), lambda i, j: (0, 0)),
                  pl.BlockSpec((D, tn), lambda i, j: (0, j))],
        out_specs=pl.BlockSpec((tm, tn), lambda i, j: (i, j)),
        out_shape=jax.ShapeDtypeStruct((T, N), out_dtype),
        scratch_shapes=[pltpu.VMEM((tm, D), BF16)],
        compiler_params=_cparams(("parallel", "arbitrary")),
        name="norm_matmul",
    )(x, g.reshape(1, D), w)


def norm_matmul_hi(x, g, w, *, tm=512):
    T, D = x.shape
    N = w.shape[1]
    tm = min(tm, T)
    assert T % tm == 0
    return pl.pallas_call(
        _norm_matmul_hi_kernel,
        grid=(T // tm,),
        in_specs=[pl.BlockSpec((tm, D), lambda i: (i, 0)),
                  pl.BlockSpec((1, D), lambda i: (0, 0)),
                  pl.BlockSpec((D, N), lambda i: (0, 0))],
        out_specs=pl.BlockSpec((tm, N), lambda i: (i, 0)),
        out_shape=jax.ShapeDtypeStruct((T, N), F32),
        compiler_params=_cparams(("parallel",)),
        name="norm_matmul_hi",
    )(x, g.reshape(1, D), w)


def _matmul_res_kernel(a_ref, w_ref, r_ref, o_ref):
    o_ref[...] = r_ref[...] + _dot(a_ref[...], w_ref[...])


def matmul_res(a, w, res, *, tm=512):
    T, K = a.shape
    N = w.shape[1]
    tm = min(tm, T)
    assert T % tm == 0
    return pl.pallas_call(
        _matmul_res_kernel,
        grid=(T // tm,),
        in_specs=[pl.BlockSpec((tm, K), lambda i: (i, 0)),
                  pl.BlockSpec((K, N), lambda i: (0, 0)),
                  pl.BlockSpec((tm, N), lambda i: (i, 0))],
        out_specs=pl.BlockSpec((tm, N), lambda i: (i, 0)),
        out_shape=jax.ShapeDtypeStruct((T, N), F32),
        compiler_params=_cparams(("parallel",)),
        name="matmul_res",
    )(a, w, res)


def _ple_kernel(h_ref, p_ref, g_ref, wg_ref, wp_ref, gf_ref, o_ref, *maybe_final):
    h = h_ref[...]
    xn = _rms(h, g_ref[...]).astype(BF16)
    gate = _sigmoid(_dot(xn, wg_ref[...]))
    out = h + gate * _dot(p_ref[...].astype(BF16), wp_ref[...])
    o_ref[...] = out
    if maybe_final:
        maybe_final[0][...] = _rms(out, gf_ref[...])


def ple(h, p, g, w_gate, w_proj, g_final=None, *, tm=512):
    T, D = h.shape
    P = p.shape[1]
    tm = min(tm, T)
    assert T % tm == 0
    final = g_final is not None
    gf = (g_final if final else g).reshape(1, D)
    row = pl.BlockSpec((tm, D), lambda i: (i, 0))
    vec = pl.BlockSpec((1, D), lambda i: (0, 0))
    out_shape = jax.ShapeDtypeStruct((T, D), F32)
    return pl.pallas_call(
        _ple_kernel,
        grid=(T // tm,),
        in_specs=[row, pl.BlockSpec((tm, P), lambda i: (i, 0)), vec,
                  pl.BlockSpec((D, D), lambda i: (0, 0)),
                  pl.BlockSpec((P, D), lambda i: (0, 0)), vec],
        out_specs=(row, row) if final else row,
        out_shape=(out_shape, out_shape) if final else out_shape,
        compiler_params=_cparams(("parallel",)),
        name="ple",
    )(h, p, g.reshape(1, D), w_gate, w_proj, gf)


def _route(logits):
    lane = lax.broadcasted_iota(jnp.int32, logits.shape, 1)
    big = jnp.int32(1 << 20)
    is_g = (lane >= MOE_E) & (lane < MOE_E + MOE_GROUPS)
    glog = jnp.where(is_g, logits, NEG)
    gmax = jnp.max(glog, axis=-1, keepdims=True)
    gsel = jnp.min(jnp.where(is_g & (glog == gmax), lane, big), axis=-1, keepdims=True) - MOE_E
    gden = jnp.sum(jnp.where(is_g, jnp.exp(glog - gmax), 0.0), axis=-1, keepdims=True)
    gprob = 1.0 / gden
    in_g = (lane < MOE_E) & ((lane // MOE_EPG) == gsel)
    e1 = jnp.where(in_g, logits, NEG)
    v1 = jnp.max(e1, axis=-1, keepdims=True)
    i1 = jnp.min(jnp.where(in_g & (e1 == v1), lane, big), axis=-1, keepdims=True)
    in_g2 = in_g & (lane != i1)
    e2 = jnp.where(in_g2, logits, NEG)
    v2 = jnp.max(e2, axis=-1, keepdims=True)
    i2 = jnp.min(jnp.where(in_g2 & (e2 == v2), lane, big), axis=-1, keepdims=True)
    t = jnp.exp(v2 - v1)
    w1 = gprob / (1.0 + t)
    w2 = gprob * t / (1.0 + t)
    return jnp.where(lane == i1, w1, jnp.where(lane == i2, w2, 0.0))


def _moe_dense_kernel(h_ref, g_ref, wr_ref, br_ref, wg_ref, wu_ref, wd_ref, o_ref, xn_ref, comb_ref, acc_ref):
    e = pl.program_id(1)

    @pl.when(e == 0)
    def _():
        xn = _rms(h_ref[...], g_ref[...])
        comb_ref[...] = _route(_dot_hi(xn, wr_ref[...]) + br_ref[...])
        xn_ref[...] = xn.astype(BF16)
        acc_ref[...] = jnp.zeros_like(acc_ref)

    xn = xn_ref[...]
    hg = _dot(xn, wg_ref[0])
    hu = _dot(xn, wu_ref[0])
    comb = comb_ref[...]
    lane = lax.broadcasted_iota(jnp.int32, comb.shape, 1)
    c = jnp.sum(jnp.where(lane == e, comb, 0.0), axis=-1, keepdims=True)
    hid = (_silu(hg) * hu * c).astype(BF16)
    acc_ref[...] += _dot(hid, wd_ref[0])

    @pl.when(e == pl.num_programs(1) - 1)
    def _():
        o_ref[...] = h_ref[...] + acc_ref[...]


def moe_dense(h, g, w_router, b_router, w_gate, w_up, w_down, *, tm=1024):
    T, D = h.shape
    E, _, Fd = w_gate.shape
    tm = min(tm, T)
    assert T % tm == 0
    row = pl.BlockSpec((tm, D), lambda i, e: (i, 0))
    return pl.pallas_call(
        _moe_dense_kernel,
        grid=(T // tm, E),
        in_specs=[row, pl.BlockSpec((1, D), lambda i, e: (0, 0)),
                  pl.BlockSpec((D, LANES), lambda i, e: (0, 0)),
                  pl.BlockSpec((1, LANES), lambda i, e: (0, 0)),
                  pl.BlockSpec((1, D, Fd), lambda i, e: (e, 0, 0)),
                  pl.BlockSpec((1, D, Fd), lambda i, e: (e, 0, 0)),
                  pl.BlockSpec((1, Fd, D), lambda i, e: (e, 0, 0))],
        out_specs=row,
        out_shape=jax.ShapeDtypeStruct((T, D), F32),
        scratch_shapes=[pltpu.VMEM((tm, D), BF16), pltpu.VMEM((tm, LANES), F32), pltpu.VMEM((tm, D), F32)],
        compiler_params=_cparams(("parallel", "arbitrary")),
        name="moe_dense",
    )(h, g.reshape(1, D), w_router, b_router, w_gate, w_up, w_down)


MOE_PAIRS = MOE_EPG * (MOE_EPG - 1) // 2
MOE_CLASSES = MOE_GROUPS * MOE_PAIRS


def _route_select(logits):
    lane = lax.broadcasted_iota(jnp.int32, logits.shape, 1)
    big = jnp.int32(1 << 20)
    is_g = (lane >= MOE_E) & (lane < MOE_E + MOE_GROUPS)
    glog = jnp.where(is_g, logits, NEG)
    gmax = jnp.max(glog, axis=-1, keepdims=True)
    gsel = jnp.min(jnp.where(is_g & (glog == gmax), lane, big), axis=-1, keepdims=True) - MOE_E
    in_g = (lane < MOE_E) & ((lane // MOE_EPG) == gsel)
    e1 = jnp.where(in_g, logits, NEG)
    v1 = jnp.max(e1, axis=-1, keepdims=True)
    i1 = jnp.min(jnp.where(in_g & (e1 == v1), lane, big), axis=-1, keepdims=True)
    in_g2 = in_g & (lane != i1)
    e2 = jnp.where(in_g2, logits, NEG)
    v2 = jnp.max(e2, axis=-1, keepdims=True)
    i2 = jnp.min(jnp.where(in_g2 & (e2 == v2), lane, big), axis=-1, keepdims=True)
    return gsel, i1, i2


def _moe_class_kernel(h_ref, g_ref, wr_ref, br_ref, o_ref):
    xn = _rms(h_ref[...], g_ref[...])
    gsel, i1, i2 = _route_select(_dot_hi(xn, wr_ref[...]) + br_ref[...])
    a = jnp.minimum(i1, i2) - gsel * MOE_EPG
    b = jnp.maximum(i1, i2) - gsel * MOE_EPG
    cls = gsel * MOE_PAIRS + (a * (2 * MOE_EPG - 1 - a)) // 2 + (b - a - 1)
    o_ref[...] = jnp.broadcast_to(cls, o_ref.shape)


def moe_class(h, g, w_router, b_router, *, tm=512):
    T, D = h.shape
    tm = min(tm, T)
    assert T % tm == 0
    return pl.pallas_call(
        _moe_class_kernel,
        grid=(T // tm,),
        in_specs=[pl.BlockSpec((tm, D), lambda i: (i, 0)), pl.BlockSpec((1, D), lambda i: (0, 0)),
                  pl.BlockSpec((D, LANES), lambda i: (0, 0)), pl.BlockSpec((1, LANES), lambda i: (0, 0))],
        out_specs=pl.BlockSpec((tm, LANES), lambda i: (i, 0)),
        out_shape=jax.ShapeDtypeStruct((T, LANES), jnp.int32),
        compiler_params=_cparams(("parallel",)),
        name="moe_class",
    )(h, g.reshape(1, D), w_router, b_router)


def _moe_plan(cls, tr):
    T = cls.shape[0]
    n_tiles = T // tr + MOE_CLASSES
    order = jnp.argsort(cls, stable=True).astype(jnp.int32)
    counts = jnp.sum(cls[:, None] == jnp.arange(MOE_CLASSES, dtype=jnp.int32)[None, :], axis=0, dtype=jnp.int32)
    tiles_per = (counts + tr - 1) // tr
    tile_end = jnp.cumsum(tiles_per)
    tile_ids = jnp.arange(n_tiles, dtype=jnp.int32)
    tcls = jnp.minimum(jnp.sum(tile_ids[:, None] >= tile_end[None, :], axis=1, dtype=jnp.int32), MOE_CLASSES - 1)
    k = tile_ids - (tile_end - tiles_per)[tcls]
    n_rows = jnp.where(tile_ids < tile_end[-1], jnp.clip(counts[tcls] - k * tr, 0, tr), 0).astype(jnp.int32)
    first = (jnp.cumsum(counts) - counts)[tcls] + k * tr
    rows = jnp.arange(tr, dtype=jnp.int32)[None, :]
    src = jnp.where(rows < n_rows[:, None], order[jnp.clip(first[:, None] + rows, 0, T - 1)], 0).astype(jnp.int32)
    pair = tcls % MOE_PAIRS
    grp = tcls // MOE_PAIRS
    pa = jnp.array([a for a in range(MOE_EPG) for b in range(a + 1, MOE_EPG)], jnp.int32)[pair]
    pb = jnp.array([b for a in range(MOE_EPG) for b in range(a + 1, MOE_EPG)], jnp.int32)[pair]
    return grp * MOE_EPG + pa, grp * MOE_EPG + pb, grp, n_rows, src


def _moe_routed_kernel(e1_ref, e2_ref, grp_ref, nrow_ref, prev_ref, src_ref, next_ref, h_ref, g_ref, wr_ref, br_ref,
                       wg1_ref, wu1_ref, wd1_ref, wg2_ref, wu2_ref, wd2_ref, o_ref, x_ref, y_ref, sem_in, sem_out):
    i = pl.program_id(0)
    last = pl.num_programs(0) - 1
    tr = x_ref.shape[1]
    n_rows = nrow_ref[i]
    slot = i % 2
    n_prev = jnp.where(i > 0, nrow_ref[jnp.maximum(i - 1, 0)], 0)
    n_next = jnp.where(i < last, nrow_ref[jnp.minimum(i + 1, last)], 0)

    def row_in(idx_ref, s, r):
        return pltpu.make_async_copy(h_ref.at[pl.ds(idx_ref[0, 0, r], 1)], x_ref.at[s, pl.ds(r, 1)], sem_in.at[s])

    def row_out(idx_ref, s, r):
        return pltpu.make_async_copy(y_ref.at[s, pl.ds(r, 1)], o_ref.at[pl.ds(idx_ref[0, 0, r], 1)], sem_out.at[s])

    def each(n, fn):
        def body8(j, carry):
            for u in range(SUBLANES):
                fn(j * SUBLANES + u)
            return carry

        def body1(r, carry):
            fn(r)
            return carry

        whole = n // SUBLANES
        lax.fori_loop(0, whole, body8, 0)
        if not isinstance(n, int) or n % SUBLANES:
            lax.fori_loop(whole * SUBLANES, n, body1, 0)

    @pl.when((i == 0) & (n_rows > 0))
    def _():
        each(tr, lambda r: row_in(src_ref, slot, r).start())

    @pl.when(n_next > 0)
    def _():
        each(tr, lambda r: row_in(next_ref, 1 - slot, r).start())

    @pl.when(n_rows > 0)
    def _():
        each(tr, lambda r: row_in(src_ref, slot, r).wait())
        x = x_ref[slot]
        xn = _rms(x, g_ref[...])
        wr = wr_ref[...]
        xh, wh = xn.astype(BF16), wr.astype(BF16)
        xl, wl = (xn - xh.astype(F32)).astype(BF16), (wr - wh.astype(F32)).astype(BF16)
        logits = _dot(xh, wh) + _dot(xh, wl) + _dot(xl, wh) + br_ref[...]
        lane = lax.broadcasted_iota(jnp.int32, logits.shape, 1)

        def pick(idx):
            return jnp.sum(jnp.where(lane == idx, logits, 0.0), axis=-1, keepdims=True)

        la, lb, lg = pick(e1_ref[i]), pick(e2_ref[i]), pick(MOE_E + grp_ref[i])
        is_g = (lane >= MOE_E) & (lane < MOE_E + MOE_GROUPS)
        gprob = 1.0 / jnp.sum(jnp.where(is_g, jnp.exp(logits - lg), 0.0), axis=-1, keepdims=True)
        top = jnp.maximum(la, lb)
        pa, pb = jnp.exp(la - top), jnp.exp(lb - top)
        scale = gprob / (pa + pb)
        xb = xn.astype(BF16)
        hid1 = (_silu(_dot(xb, wg1_ref[0])) * _dot(xb, wu1_ref[0]) * (pa * scale)).astype(BF16)
        hid2 = (_silu(_dot(xb, wg2_ref[0])) * _dot(xb, wu2_ref[0]) * (pb * scale)).astype(BF16)
        y_ref[slot] = x + _dot(hid1, wd1_ref[0]) + _dot(hid2, wd2_ref[0])

    each(n_prev, lambda r: row_out(prev_ref, 1 - slot, r).wait())
    each(n_rows, lambda r: row_out(src_ref, slot, r).start())

    @pl.when(i == last)
    def _():
        each(n_rows, lambda r: row_out(src_ref, slot, r).wait())


def moe_routed(h, g, w_router, b_router, w_gate, w_up, w_down, *, tr=256):
    T, D = h.shape
    E, _, Fd = w_gate.shape
    assert T % tr == 0
    cls = moe_class(h, g, w_router, b_router)[:, 0]
    e1, e2, grp, n_rows, src = _moe_plan(cls, tr)
    n_tiles = src.shape[0]

    def wspec(shape, which):
        return pl.BlockSpec((1,) + shape, lambda i, e1, e2, grp, nr: ((e1, e2)[which][i], 0, 0))

    const = lambda shape: pl.BlockSpec(shape, lambda i, e1, e2, grp, nr: (0,) * len(shape))

    def rows_of(shift):
        return pl.BlockSpec((1, 1, tr), lambda i, e1, e2, grp, nr: (jnp.clip(i + shift, 0, n_tiles - 1), 0, 0),
                            memory_space=pltpu.SMEM)

    grid_spec = pltpu.PrefetchScalarGridSpec(
        num_scalar_prefetch=4, grid=(n_tiles,),
        in_specs=[rows_of(-1), rows_of(0), rows_of(1),
                  pl.BlockSpec(memory_space=pl.ANY), const((1, D)), const((D, LANES)), const((1, LANES)),
                  wspec((D, Fd), 0), wspec((D, Fd), 0), wspec((Fd, D), 0),
                  wspec((D, Fd), 1), wspec((D, Fd), 1), wspec((Fd, D), 1)],
        out_specs=pl.BlockSpec(memory_space=pl.ANY),
        scratch_shapes=[pltpu.VMEM((2, tr, D), F32), pltpu.VMEM((2, tr, D), F32),
                        pltpu.SemaphoreType.DMA((2,)), pltpu.SemaphoreType.DMA((2,))])
    src3 = src.reshape(n_tiles, 1, tr)
    return pl.pallas_call(
        _moe_routed_kernel,
        grid_spec=grid_spec,
        out_shape=jax.ShapeDtypeStruct((T, D), F32),
        compiler_params=_cparams(("arbitrary",)),
        name="moe_routed",
    )(e1, e2, grp, n_rows, src3, src3, src3, h, g.reshape(1, D), w_router, b_router,
      w_gate, w_up, w_down, w_gate, w_up, w_down)


def _conv_tile(xpad_ref, x, cw_ref, cb_ref, first):
    tl = x.shape[0]

    @pl.when(first)
    def _():
        xpad_ref[0:SUBLANES, :] = jnp.zeros((SUBLANES, x.shape[1]), F32)

    xpad_ref[SUBLANES:SUBLANES + tl, :] = x
    out = cb_ref[...] + x * cw_ref[CONV_W - 1:CONV_W, :]
    for k in range(CONV_W - 1):
        lo = SUBLANES - (CONV_W - 1) + k
        out = out + xpad_ref[lo:lo + tl, :] * cw_ref[k:k + 1, :]
    tail = xpad_ref[tl:tl + SUBLANES, :]
    xpad_ref[0:SUBLANES, :] = tail
    return out, tail[SUBLANES - (CONV_W - 1):, :]


def _lru_gates(xc, wa_ref, ba_ref, wi_ref, bi_ref, lam_ref):
    xcb = xc.astype(BF16)
    nb = wa_ref.shape[0]
    cw = wa_ref.shape[1]
    ra = jnp.concatenate([_dot(xcb[:, c * cw:(c + 1) * cw], wa_ref[c]) for c in range(nb)], axis=1)
    ia = jnp.concatenate([_dot(xcb[:, c * cw:(c + 1) * cw], wi_ref[c]) for c in range(nb)], axis=1)
    r = _sigmoid(ra + ba_ref[...])
    ig = _sigmoid(ia + bi_ref[...])
    log_a = (-LRU_C) * r * _softplus(-lam_ref[...])
    a = jnp.exp(log_a)
    b = jnp.sqrt(-jnp.tanh(log_a) * (a * a + 1.0)) * (ig * xc)
    return a, b


def _lru_seq_kernel(xb_ref, yb_ref, cw_ref, cb_ref, wa_ref, ba_ref, wi_ref, bi_ref, lam_ref,
                    o_ref, buf_ref, hl_ref, xpad_ref, a_ref, b_ref, hc_ref):
    t = pl.program_id(1)
    tl, W = xb_ref.shape[1], xb_ref.shape[2]
    xc, tail = _conv_tile(xpad_ref, xb_ref[0], cw_ref, cb_ref, t == 0)
    a, b = _lru_gates(xc, wa_ref, ba_ref, wi_ref, bi_ref, lam_ref)
    a_ref[...] = a
    b_ref[...] = b

    @pl.when(t == 0)
    def _():
        hc_ref[...] = jnp.zeros_like(hc_ref)

    row = lax.broadcasted_iota(jnp.int32, (SUBLANES, W), 0)

    def body(i, hc):
        r0 = pl.multiple_of(i * SUBLANES, SUBLANES)
        av = a_ref[pl.ds(r0, SUBLANES), :]
        bv = b_ref[pl.ds(r0, SUBLANES), :]
        for s in (1, 2, 4):
            keep = row >= s
            a_sh = pltpu.roll(av, s, 0)
            b_sh = pltpu.roll(bv, s, 0)
            bv = jnp.where(keep, av * b_sh + bv, bv)
            av = jnp.where(keep, av * a_sh, av)
        h = av * hc + bv
        b_ref[pl.ds(r0, SUBLANES), :] = h
        return jnp.broadcast_to(h[SUBLANES - 1:SUBLANES, :], (SUBLANES, W))

    hc = lax.fori_loop(0, tl // SUBLANES, body, hc_ref[...])
    hc_ref[...] = hc
    o_ref[0] = (b_ref[...] * _gelu_tanh(yb_ref[0])).astype(o_ref.dtype)
    buf_ref[0] = tail
    hl_ref[0] = hc[0:1, :]


def lru_seq(xy, conv_w, conv_b, wa_bd, b_a, wi_bd, b_i, lam, *, tl=256):
    B, L, W2 = xy.shape
    W = W2 // 2
    tl = min(tl, L)
    assert L % tl == 0 and tl % SUBLANES == 0
    vec = pl.BlockSpec((1, W), lambda b, t: (0, 0))
    wbd = pl.BlockSpec(wa_bd.shape, lambda b, t: (0, 0, 0))
    return pl.pallas_call(
        _lru_seq_kernel,
        grid=(B, L // tl),
        in_specs=[pl.BlockSpec((1, tl, W), lambda b, t: (b, t, 0)),
                  pl.BlockSpec((1, tl, W), lambda b, t: (b, t, 1)),
                  pl.BlockSpec((CONV_W, W), lambda b, t: (0, 0)), vec, wbd, vec, wbd, vec, vec],
        out_specs=(pl.BlockSpec((1, tl, W), lambda b, t: (b, t, 0)),
                   pl.BlockSpec((1, CONV_W - 1, W), lambda b, t: (b, 0, 0)),
                   pl.BlockSpec((1, 1, W), lambda b, t: (b, 0, 0))),
        out_shape=(jax.ShapeDtypeStruct((B, L, W), BF16),
                   jax.ShapeDtypeStruct((B, CONV_W - 1, W), F32),
                   jax.ShapeDtypeStruct((B, 1, W), F32)),
        scratch_shapes=[pltpu.VMEM((tl + SUBLANES, W), F32), pltpu.VMEM((tl, W), F32),
                        pltpu.VMEM((tl, W), F32), pltpu.VMEM((SUBLANES, W), F32)],
        compiler_params=_cparams(("parallel", "arbitrary")),
        name="lru_seq",
    )(xy, xy, conv_w, conv_b.reshape(1, W), wa_bd, b_a.reshape(1, W), wi_bd, b_i.reshape(1, W), lam.reshape(1, W))


def _lru_step_kernel(xb_ref, yb_ref, buf_ref, h0_ref, cw_ref, cb_ref, wa_ref, ba_ref, wi_ref, bi_ref, lam_ref,
                     o_ref, nbuf_ref, hl_ref):
    x = xb_ref[...]
    xc = cb_ref[...] + x * cw_ref[CONV_W - 1:CONV_W, :]
    for k in range(CONV_W - 1):
        xc = xc + buf_ref[k] * cw_ref[k:k + 1, :]
    a, b = _lru_gates(xc, wa_ref, ba_ref, wi_ref, bi_ref, lam_ref)
    h = a * h0_ref[...] + b
    o_ref[...] = (h * _gelu_tanh(yb_ref[...])).astype(o_ref.dtype)
    hl_ref[...] = h
    for k in range(CONV_W - 2):
        nbuf_ref[k] = buf_ref[k + 1]
    nbuf_ref[CONV_W - 2] = x


def lru_step(xy, buf_t, h0, conv_w, conv_b, wa_bd, b_a, wi_bd, b_i, lam):
    B, W2 = xy.shape
    W = W2 // 2
    vec = pl.BlockSpec((1, W), lambda i: (0, 0))
    wbd = pl.BlockSpec(wa_bd.shape, lambda i: (0, 0, 0))
    mat = pl.BlockSpec((B, W), lambda i: (0, 0))
    cube = pl.BlockSpec((CONV_W - 1, B, W), lambda i: (0, 0, 0))
    return pl.pallas_call(
        _lru_step_kernel,
        grid=(1,),
        in_specs=[mat, pl.BlockSpec((B, W), lambda i: (0, 1)), cube, mat,
                  pl.BlockSpec((CONV_W, W), lambda i: (0, 0)), vec, wbd, vec, wbd, vec, vec],
        out_specs=(mat, cube, mat),
        out_shape=(jax.ShapeDtypeStruct((B, W), BF16),
                   jax.ShapeDtypeStruct((CONV_W - 1, B, W), F32),
                   jax.ShapeDtypeStruct((B, W), F32)),
        compiler_params=_cparams(("arbitrary",)),
        name="lru_step",
    )(xy, xy, buf_t, h0, conv_w, conv_b.reshape(1, W), wa_bd, b_a.reshape(1, W), wi_bd, b_i.reshape(1, W),
      lam.reshape(1, W))


def _group_norm_gate(y, z, ng, n_groups):
    y = y * _silu(z)
    gw = y.shape[1] // n_groups
    outs = []
    for g in range(n_groups):
        yg = y[:, g * gw:(g + 1) * gw]
        ms = jnp.mean(yg * yg, axis=-1, keepdims=True)
        outs.append(yg * lax.rsqrt(ms + EPS))
    return jnp.concatenate(outs, axis=1) * ng


def _ssd_seq_kernel(z_ref, xbc_ref, dt_ref, cw_ref, cb_ref, dtb_ref, alog_ref, dsk_ref, ng_ref, ex_ref,
                    o_ref, buf_ref, st_ref, xpad_ref, s_ref):
    t = pl.program_id(1)
    Q = xbc_ref.shape[1]
    DI = z_ref.shape[2]
    GN = SSM_G * SSM_N
    RP = DI // SSM_G
    xc, tail = _conv_tile(xpad_ref, xbc_ref[0], cw_ref, cb_ref, t == 0)
    xc = _silu(xc)
    xs = xc[:, :DI]
    bm = xc[:, DI:DI + GN]
    cm = xc[:, DI + GN:]

    @pl.when(t == 0)
    def _():
        s_ref[...] = jnp.zeros_like(s_ref)

    dt = _softplus(dt_ref[0] + dtb_ref[...])
    a_neg = -jnp.exp(alog_ref[...])
    ri = lax.broadcasted_iota(jnp.int32, (Q, Q), 0)
    ci = lax.broadcasted_iota(jnp.int32, (Q, Q), 1)
    tri = ci <= ri
    acum = _dot_hi(tri.astype(F32), dt * a_neg)
    acum_t = acum.T
    ex = ex_ref[...]
    dt_e = _place_dot(dt, ex)
    acum_e = _place_dot(acum, ex)
    last_e = acum_e[Q - 1:Q, :]
    xdt = xs * dt_e
    xdtw = (xdt * jnp.exp(last_e - acum_e)).astype(BF16)
    xdt = xdt.astype(BF16)
    eacum = jnp.exp(acum_e)
    edec = jnp.exp(last_e)
    lane = lax.broadcasted_iota(jnp.int32, (Q, LANES), 1)
    lo = lane < SSM_P
    ys = []
    for g in range(SSM_G):
        cg = cm[:, g * SSM_N:(g + 1) * SSM_N].astype(BF16)
        bg32 = bm[:, g * SSM_N:(g + 1) * SSM_N]
        bg = bg32.astype(BF16)
        cb = _dot_nt(cg, bg)
        st = s_ref[g]
        yoff = _dot(cg, st.astype(BF16)) * eacum[:, g * RP:(g + 1) * RP]
        for pr in range(RP // LANES):
            ms = []
            for k in range(LANES // SSM_P):
                hd = (g * RP + pr * LANES) // SSM_P + k
                seg = acum[:, hd:hd + 1] - acum_t[hd:hd + 1, :]
                ms.append((cb * jnp.exp(jnp.where(tri, seg, NEG))).astype(BF16))
            c0 = g * RP + pr * LANES
            xp = xdt[:, c0:c0 + LANES]
            zero = jnp.zeros_like(xp)
            rhs = jnp.concatenate([jnp.where(lo, xp, zero), jnp.where(lo, zero, xp)], axis=0)
            ydiag = _dot(jnp.concatenate(ms, axis=1), rhs)
            ys.append(ydiag + yoff[:, pr * LANES:(pr + 1) * LANES])
        new = st * edec[:, g * RP:(g + 1) * RP] + _dot(bg32.T.astype(BF16), xdtw[:, g * RP:(g + 1) * RP])
        s_ref[g] = new
    y = jnp.concatenate(ys, axis=1) + xs * dsk_ref[...]
    o_ref[0] = _group_norm_gate(y, z_ref[0], ng_ref[...], SSM_G).astype(o_ref.dtype)
    buf_ref[0] = tail

    @pl.when(t == pl.num_programs(1) - 1)
    def _():
        hpg = RP // SSM_P
        for g in range(SSM_G):
            st_ref[0, g * hpg:(g + 1) * hpg] = s_ref[g].T.reshape(hpg, SSM_P, SSM_N)


def ssd_seq(z, xbc, dt, conv_w, conv_b, dt_bias, a_log, d_lanes, norm_g, expand, *, q=128):
    B, L, DI = z.shape
    C = xbc.shape[2]
    H = DI // SSM_P
    q = min(q, L)
    assert L % q == 0
    vecd = pl.BlockSpec((1, DI), lambda b, t: (0, 0))
    vecl = pl.BlockSpec((1, LANES), lambda b, t: (0, 0))
    return pl.pallas_call(
        _ssd_seq_kernel,
        grid=(B, L // q),
        in_specs=[pl.BlockSpec((1, q, DI), lambda b, t: (b, t, 0)),
                  pl.BlockSpec((1, q, C), lambda b, t: (b, t, 0)),
                  pl.BlockSpec((1, q, LANES), lambda b, t: (b, t, 0)),
                  pl.BlockSpec((CONV_W, C), lambda b, t: (0, 0)),
                  pl.BlockSpec((1, C), lambda b, t: (0, 0)),
                  vecl, vecl, vecd, vecd,
                  pl.BlockSpec((LANES, DI), lambda b, t: (0, 0))],
        out_specs=(pl.BlockSpec((1, q, DI), lambda b, t: (b, t, 0)),
                   pl.BlockSpec((1, CONV_W - 1, C), lambda b, t: (b, 0, 0)),
                   pl.BlockSpec((1, H, SSM_P, SSM_N), lambda b, t: (b, 0, 0, 0))),
        out_shape=(jax.ShapeDtypeStruct((B, L, DI), BF16),
                   jax.ShapeDtypeStruct((B, CONV_W - 1, C), F32),
                   jax.ShapeDtypeStruct((B, H, SSM_P, SSM_N), F32)),
        scratch_shapes=[pltpu.VMEM((q + SUBLANES, C), F32),
                        pltpu.VMEM((SSM_G, SSM_N, DI // SSM_G), F32)],
        compiler_params=_cparams(("parallel", "arbitrary")),
        name="ssd_seq",
    )(z, xbc, dt, conv_w, conv_b.reshape(1, C), dt_bias, a_log, d_lanes, norm_g.reshape(1, DI), expand)


def _ssd_step_kernel(z_ref, xbc_ref, dt_ref, buf_ref, h0_ref, cw_ref, cb_ref, dtb_ref, alog_ref, dsk_ref, ng_ref,
                     ex_ref, o_ref, nbuf_ref, hn_ref):
    DI = z_ref.shape[2]
    GN = SSM_G * SSM_N
    hpg = DI // SSM_P // SSM_G
    x = xbc_ref[0]
    buf = buf_ref[0]
    xc = cb_ref[...] + x * cw_ref[CONV_W - 1:CONV_W, :]
    for k in range(CONV_W - 1):
        xc = xc + buf[k:k + 1, :] * cw_ref[k:k + 1, :]
    nbuf_ref[0, 0:CONV_W - 2, :] = buf[1:CONV_W - 1, :]
    nbuf_ref[0, CONV_W - 2:CONV_W - 1, :] = x
    xc = _silu(xc)
    xs = xc[:, :DI]
    dt = _softplus(dt_ref[0] + dtb_ref[...])
    dta = dt * (-jnp.exp(alog_ref[...]))
    ex = ex_ref[...]
    dec = jnp.exp(dta)
    dt_e = _place_dot(jnp.broadcast_to(dt, (SUBLANES, LANES)), ex)[0:1, :]
    xdt = xs * dt_e
    RP = hpg * SSM_P
    eye = (lax.broadcasted_iota(jnp.int32, (RP, RP), 0) == lax.broadcasted_iota(jnp.int32, (RP, RP), 1))
    ys = []
    for g in range(SSM_G):
        brow = xc[:, DI + g * SSM_N:DI + (g + 1) * SSM_N]
        crow = xc[:, DI + GN + g * SSM_N:DI + GN + (g + 1) * SSM_N]
        xg = xdt[:, g * RP:(g + 1) * RP]
        xcol = jnp.sum(jnp.where(eye, jnp.broadcast_to(xg, (RP, RP)), 0.0), axis=-1, keepdims=True)
        news = []
        for r in range(hpg):
            hd = g * hpg + r
            new = h0_ref[0, hd] * dec[:, hd:hd + 1] + xcol[r * SSM_P:(r + 1) * SSM_P, :] * brow
            hn_ref[0, hd] = new
            news.append(new)
        new_g = jnp.concatenate(news, axis=0).astype(BF16)
        ys.append(_dot_nt(jnp.broadcast_to(crow, (SUBLANES, SSM_N)).astype(BF16), new_g)[0:1, :])
    y = jnp.concatenate(ys, axis=1) + xs * dsk_ref[...]
    o_ref[0] = _group_norm_gate(y, z_ref[0], ng_ref[...], SSM_G).astype(o_ref.dtype)


def ssd_step(z, xbc, dt, buf, h0, conv_w, conv_b, dt_bias, a_log, d_lanes, norm_g, expand):
    B, _, DI = z.shape
    C = xbc.shape[2]
    H = DI // SSM_P
    vecd = pl.BlockSpec((1, DI), lambda b: (0, 0))
    vecl = pl.BlockSpec((1, LANES), lambda b: (0, 0))
    st = pl.BlockSpec((1, H, SSM_P, SSM_N), lambda b: (b, 0, 0, 0))
    return pl.pallas_call(
        _ssd_step_kernel,
        grid=(B,),
        in_specs=[pl.BlockSpec((1, 1, DI), lambda b: (b, 0, 0)),
                  pl.BlockSpec((1, 1, C), lambda b: (b, 0, 0)),
                  pl.BlockSpec((1, 1, LANES), lambda b: (b, 0, 0)),
                  pl.BlockSpec((1, CONV_W - 1, C), lambda b: (b, 0, 0)), st,
                  pl.BlockSpec((CONV_W, C), lambda b: (0, 0)),
                  pl.BlockSpec((1, C), lambda b: (0, 0)),
                  vecl, vecl, vecd, vecd,
                  pl.BlockSpec((LANES, DI), lambda b: (0, 0))],
        out_specs=(pl.BlockSpec((1, 1, DI), lambda b: (b, 0, 0)),
                   pl.BlockSpec((1, CONV_W - 1, C), lambda b: (b, 0, 0)), st),
        out_shape=(jax.ShapeDtypeStruct((B, 1, DI), BF16),
                   jax.ShapeDtypeStruct((B, CONV_W - 1, C), F32),
                   jax.ShapeDtypeStruct((B, H, SSM_P, SSM_N), F32)),
        compiler_params=_cparams(("parallel",)),
        name="ssd_step",
    )(z, xbc, dt, buf, h0, conv_w, conv_b.reshape(1, C), dt_bias, a_log, d_lanes, norm_g.reshape(1, DI), expand)


def _split_dot(x, m):
    hi = x.astype(BF16)
    lo = (x - hi.astype(F32)).astype(BF16)
    return _dot(hi, m) + _dot(lo, m)


LOG2E = 1.4426950408889634
N_PIECES = 3
PAIR = LANES // ATT_HD


def _fox_prep_kernel(q_ref, k_ref, v_ref, fl_ref, bf_ref, qg_ref, kg_ref, hs_ref, he_ref, pl_ref,
                     qo_ref, ko_ref, lf_ref, kx_ref, vt_ref, kt32_ref, vt32_ref, carry_ref):
    t = pl.program_id(1)
    tl = q_ref.shape[1]
    D = q_ref.shape[2]

    def head_norm(x, g):
        ss = _split_dot(x * x, hs_ref[...])
        inv = lax.rsqrt(ss * (1.0 / ATT_HD) + EPS)
        return x * _split_dot(inv, he_ref[...]) * g

    qo_ref[0] = (head_norm(q_ref[0], qg_ref[...]) * (ATT_HD ** -0.5 * LOG2E)).astype(qo_ref.dtype)
    kn = head_norm(k_ref[0], kg_ref[...])
    ko_ref[0] = kn
    kt32_ref[0] = kn.T
    vt = v_ref[0].T
    vt32_ref[0] = vt
    vt_ref[0] = vt.astype(BF16)
    z = fl_ref[0] + bf_ref[...]
    logf = jnp.minimum(z, 0.0) - jnp.log1p(jnp.exp(-jnp.abs(z)))
    lf_ref[0] = logf

    @pl.when(t == 0)
    def _():
        carry_ref[...] = jnp.zeros_like(carry_ref)

    ri = lax.broadcasted_iota(jnp.int32, (tl, tl), 0)
    ci = lax.broadcasted_iota(jnp.int32, (tl, tl), 1)
    c = _dot_hi((ci <= ri).astype(F32), logf) + carry_ref[0:1, :]
    carry_ref[...] = jnp.broadcast_to(c[tl - 1:tl, :], carry_ref.shape)
    rest = c * (-LOG2E)
    extra = jnp.zeros((tl, D), F32)
    for j in range(N_PIECES):
        piece = rest.astype(BF16)
        rest = rest - piece.astype(F32)
        extra = extra + _dot(piece, pl_ref[j])
    knb = kn.astype(BF16)
    extra = extra.astype(BF16)
    kx_ref[0] = jnp.concatenate(
        [x[:, p * LANES:(p + 1) * LANES] for p in range(D // LANES) for x in (knb, extra)], axis=1)


def fox_prep(q, k, v, fl, b_f, q_g, k_g, head_sum, head_expand, place, *, tl=256):
    B, L, D = q.shape
    tl = min(tl, L)
    assert L % tl == 0
    row = pl.BlockSpec((1, tl, D), lambda b, t: (b, t, 0))
    colt = pl.BlockSpec((1, D, tl), lambda b, t: (b, 0, t))
    nar = pl.BlockSpec((1, tl, LANES), lambda b, t: (b, t, 0))
    vecd = pl.BlockSpec((1, D), lambda b, t: (0, 0))
    return pl.pallas_call(
        _fox_prep_kernel,
        grid=(B, L // tl),
        in_specs=[row, row, row, nar, pl.BlockSpec((1, LANES), lambda b, t: (0, 0)), vecd, vecd,
                  pl.BlockSpec((D, LANES), lambda b, t: (0, 0)),
                  pl.BlockSpec((LANES, D), lambda b, t: (0, 0)),
                  pl.BlockSpec((N_PIECES, LANES, D), lambda b, t: (0, 0, 0))],
        out_specs=(row, row, nar, pl.BlockSpec((1, tl, 2 * D), lambda b, t: (b, t, 0)),
                   colt, colt, colt),
        out_shape=(jax.ShapeDtypeStruct((B, L, D), BF16),
                   jax.ShapeDtypeStruct((B, L, D), F32),
                   jax.ShapeDtypeStruct((B, L, LANES), F32),
                   jax.ShapeDtypeStruct((B, L, 2 * D), BF16),
                   jax.ShapeDtypeStruct((B, D, L), BF16),
                   jax.ShapeDtypeStruct((B, D, L), F32),
                   jax.ShapeDtypeStruct((B, D, L), F32)),
        scratch_shapes=[pltpu.VMEM((SUBLANES, LANES), F32)],
        compiler_params=_cparams(("parallel", "arbitrary")),
        name="fox_prep",
    )(q, k, v, fl, b_f, q_g, k_g, head_sum, head_expand, place)


def _fox_attn_kernel(q_ref, kx_ref, vt_ref, g_ref, o_ref, qx_ref, m_ref, l_ref, acc_ref, s_ref):
    qi = pl.program_id(2)
    tq = q_ref.shape[1]
    q = q_ref[0].astype(F32)
    lane = lax.broadcasted_iota(jnp.int32, (tq, LANES), 1)
    for k in range(PAIR):
        mine = (lane >= k * ATT_HD) & (lane < (k + 1) * ATT_HD)
        pick = (lane >= k * N_PIECES) & (lane < (k + 1) * N_PIECES)
        qx_ref[k] = jnp.concatenate([jnp.where(mine, q, 0.0), jnp.where(pick, 1.0, 0.0)], axis=1).astype(BF16)
    m_ref[...] = jnp.full_like(m_ref, NEG)
    l_ref[...] = jnp.zeros_like(l_ref)
    acc_ref[...] = jnp.zeros_like(acc_ref)

    def scores(j):
        r0 = pl.multiple_of(j * tq, tq)
        kb = kx_ref[0, pl.ds(r0, tq), :]
        return [_dot_nt(kb, qx_ref[k]) for k in range(PAIR)]

    def consume(j, diagonal):
        r0 = pl.multiple_of(j * tq, tq)
        vt = vt_ref[0, :, pl.ds(r0, tq)]
        for k in range(PAIR):
            s = s_ref[k]
            if diagonal:
                rows = lax.broadcasted_iota(jnp.int32, (tq, tq), 0)
                cols = lax.broadcasted_iota(jnp.int32, (tq, tq), 1)
                s = jnp.where(rows <= cols, s, NEG)
            m_old = m_ref[k]
            m_new = jnp.maximum(m_old, jnp.max(s, axis=0, keepdims=True))
            alpha = jnp.exp2(m_old - m_new)
            p = jnp.exp2(s - m_new)
            l_ref[k] = l_ref[k] * alpha + jnp.sum(p, axis=0, keepdims=True)
            m_ref[k] = m_new
            acc_ref[k] = acc_ref[k] * alpha + _dot(vt[k * ATT_HD:(k + 1) * ATT_HD, :], p.astype(BF16))

    def stash(s_list):
        for k in range(PAIR):
            s_ref[k] = s_list[k]

    stash(scores(0))

    def body(j, carry):
        nxt = scores(j + 1)
        consume(j, False)
        stash(nxt)
        return carry

    lax.fori_loop(0, qi, body, 0)
    consume(qi, True)
    o = jnp.concatenate([acc_ref[k] / l_ref[k] for k in range(PAIR)], axis=0)
    o_ref[0] = (o.T * _sigmoid(g_ref[0])).astype(o_ref.dtype)


def fox_attn(q, kx, vt, g, *, tq=512):
    B, L, D = q.shape
    tq = min(tq, L)
    assert L % tq == 0
    qs = pl.BlockSpec((1, tq, LANES), lambda b, p, i: (b, i, p))
    return pl.pallas_call(
        _fox_attn_kernel,
        grid=(B, D // LANES, L // tq),
        in_specs=[qs,
                  pl.BlockSpec((1, L, 2 * LANES), lambda b, p, i: (b, 0, p)),
                  pl.BlockSpec((1, LANES, L), lambda b, p, i: (b, p, 0)),
                  qs],
        out_specs=qs,
        out_shape=jax.ShapeDtypeStruct((B, L, D), BF16),
        scratch_shapes=[pltpu.VMEM((PAIR, tq, 2 * LANES), BF16), pltpu.VMEM((PAIR, 1, tq), F32),
                        pltpu.VMEM((PAIR, 1, tq), F32), pltpu.VMEM((PAIR, ATT_HD, tq), F32),
                        pltpu.VMEM((PAIR, tq, tq), F32)],
        compiler_params=_cparams(("parallel", "parallel", "arbitrary")),
        name="fox_attn",
    )(q, kx, vt, g)


def _lanes_to_sublanes(row):
    n = row.shape[1]
    eye = lax.broadcasted_iota(jnp.int32, (n, n), 0) == lax.broadcasted_iota(jnp.int32, (n, n), 1)
    return jnp.sum(jnp.where(eye, jnp.broadcast_to(row, (n, n)), 0.0), axis=1, keepdims=True)


def _sublanes_to_lanes(col):
    n = col.shape[0]
    eye = lax.broadcasted_iota(jnp.int32, (n, n), 0) == lax.broadcasted_iota(jnp.int32, (n, n), 1)
    return jnp.sum(jnp.where(eye, jnp.broadcast_to(col, (n, n)), 0.0), axis=0, keepdims=True)


def _fox_decode_kernel(pt_ref, q_ref, kn_ref, vn_ref, lfn_ref, g_ref, *rest, npg):
    k_refs = rest[:npg]
    v_refs = rest[npg:2 * npg]
    lf_refs = rest[2 * npg:3 * npg]
    o_ref = rest[3 * npg]
    qb_ref, m_ref, l_ref, acc_ref, coff_ref = rest[3 * npg + 1:]
    s_id = pl.program_id(1)
    H, hd, ps = k_refs[0].shape[1:]

    @pl.when(s_id == 0)
    def _():
        q = q_ref[0]
        for h in range(H):
            qb_ref[h] = jnp.broadcast_to(_lanes_to_sublanes(q[h:h + 1, :]), (hd, ps))
        m_ref[...] = jnp.full_like(m_ref, NEG)
        l_ref[...] = jnp.zeros_like(l_ref)
        acc_ref[...] = jnp.zeros_like(acc_ref)
        coff_ref[...] = jnp.zeros_like(coff_ref)

    ri = lax.broadcasted_iota(jnp.int32, (ps, ps), 0)
    ci = lax.broadcasted_iota(jnp.int32, (ps, ps), 1)
    upper = (ri <= ci).astype(F32)

    for i in range(npg):
        qk = jnp.concatenate([jnp.sum(k_refs[i][0, h] * qb_ref[h], axis=0, keepdims=True) for h in range(H)], axis=0)
        c = _dot_hi(lf_refs[i][0], upper) + coff_ref[...]
        coff_ref[...] = c[:, ps - 1:ps]
        s = qk - c * LOG2E
        m_old = m_ref[...]
        m_new = jnp.maximum(m_old, jnp.max(s, axis=1, keepdims=True))
        alpha = jnp.exp2(m_old - m_new)
        p = jnp.exp2(s - m_new)
        l_ref[...] = l_ref[...] * alpha + jnp.sum(p, axis=1, keepdims=True)
        m_ref[...] = m_new
        for h in range(H):
            acc_ref[h] = acc_ref[h] * alpha[h:h + 1, :] + p[h:h + 1, :] * v_refs[i][0, h]

    @pl.when(s_id == pl.num_programs(1) - 1)
    def _():
        s = (jnp.sum(q_ref[0] * kn_ref[0], axis=1, keepdims=True)
             - (coff_ref[...] + _lanes_to_sublanes(lfn_ref[0][:, 0:H])) * LOG2E)
        m_old = m_ref[...]
        m_new = jnp.maximum(m_old, s)
        alpha = jnp.exp2(m_old - m_new)
        p = jnp.exp2(s - m_new)
        l = l_ref[...] * alpha + p
        past = jnp.concatenate(
            [_sublanes_to_lanes(jnp.sum(acc_ref[h], axis=1, keepdims=True)) for h in range(H)], axis=0)
        o = (past * alpha + p * vn_ref[0]) / l
        o_ref[0] = (o * _sigmoid(g_ref[0])).astype(o_ref.dtype)


def fox_decode(q, k_new, v_new, lf_new, g, cache_kt, cache_vt, cache_lft, page_table, *, npg=8):
    B, H, hd = q.shape
    n_pages = page_table.shape[1]
    ps = cache_kt.shape[3]
    npg = min(npg, n_pages)
    assert n_pages % npg == 0
    steps = n_pages // npg
    row = pl.BlockSpec((1, H, hd), lambda b, s, pt: (b, 0, 0))

    def page(i, *tail):
        return pl.BlockSpec((1,) + tail, lambda b, s, pt: (pt[b * n_pages + s * npg + i],) + (0,) * len(tail))

    in_specs = ([row, row, row, pl.BlockSpec((1, 1, LANES), lambda b, s, pt: (b, 0, 0)), row]
                + [page(i, H, hd, ps) for i in range(npg)] + [page(i, H, hd, ps) for i in range(npg)]
                + [page(i, H, ps) for i in range(npg)])
    grid_spec = pltpu.PrefetchScalarGridSpec(
        num_scalar_prefetch=1, grid=(B, steps), in_specs=in_specs, out_specs=row,
        scratch_shapes=[pltpu.VMEM((H, hd, ps), F32), pltpu.VMEM((H, 1), F32), pltpu.VMEM((H, 1), F32),
                        pltpu.VMEM((H, hd, ps), F32), pltpu.VMEM((H, 1), F32)])
    return pl.pallas_call(
        functools.partial(_fox_decode_kernel, npg=npg),
        grid_spec=grid_spec,
        out_shape=jax.ShapeDtypeStruct((B, H, hd), BF16),
        compiler_params=_cparams(("parallel", "arbitrary")),
        name="fox_decode",
    )(page_table.reshape(-1), q, k_new, v_new, lf_new, g,
      *([cache_kt] * npg), *([cache_vt] * npg), *([cache_lft] * npg))


def _block_diag_chunks(w, per_chunk):
    nblk, bw, _ = w.shape
    w = w.reshape(nblk // per_chunk, per_chunk, bw, bw)
    eye = jnp.eye(per_chunk, dtype=w.dtype)
    out = jnp.einsum('cpij,pq->cpiqj', w, eye)
    return out.reshape(nblk // per_chunk, per_chunk * bw, per_chunk * bw)


def _head_expand(n_heads, width):
    r = jnp.arange(LANES)[:, None]
    c = jnp.arange(n_heads * width)[None, :] // width
    return (r == c).astype(F32)


def _pad_lanes(v):
    return jnp.pad(v.reshape(1, -1), ((0, 0), (0, LANES - v.size)))


MOE_ROUTED_MIN_TOKENS = 2048


def _moe_and_ple(h, p_i, i, prm, last):
    moe = moe_routed if h.shape[0] >= MOE_ROUTED_MIN_TOKENS else moe_dense
    h = moe(h, prm['norm_ffn'][i], prm['moe_w_router'][i], prm['moe_b_router'][i],
            prm['moe_w_gate'][i], prm['moe_w_up'][i], prm['moe_w_down'][i])
    return ple(h, p_i, prm['norm_ple'][i], prm['ple_w_gate'][i], prm['ple_w_proj'][i],
               prm['norm_final'] if last else None)


def _lru_layer_prompt(h, B, L, j, g, prm):
    D = h.shape[1]
    xy = norm_matmul(h, g, prm['lru_w_in'][j])
    gated, buf, hl = lru_seq(xy.reshape(B, L, -1), prm['lru_conv_w'][j], prm['lru_conv_b'][j],
                             prm['lru_wa_bd'][j], prm['lru_b_a'][j], prm['lru_wi_bd'][j], prm['lru_b_i'][j],
                             prm['lru_lambda'][j])
    h = matmul_res(gated.reshape(B * L, -1), prm['lru_w_out'][j], h)
    return h, buf, hl.reshape(B, D)


def _lru_layer_sample(h, buf, h0, j, g, prm):
    xy = norm_matmul(h, g, prm['lru_w_in'][j])
    gated, nbuf, hl = lru_step(xy, jnp.swapaxes(buf, 0, 1), h0, prm['lru_conv_w'][j], prm['lru_conv_b'][j],
                               prm['lru_wa_bd'][j], prm['lru_b_a'][j], prm['lru_wi_bd'][j], prm['lru_b_i'][j],
                               prm['lru_lambda'][j])
    h = matmul_res(gated, prm['lru_w_out'][j], h)
    return h, jnp.swapaxes(nbuf, 0, 1), hl


def _ssd_proj(h, g, j, prm):
    z = norm_matmul(h, g, prm['ssm_w_z'][j])
    xbc = norm_matmul(h, g, prm['ssm_w_xbc'][j])
    dt = norm_matmul_hi(h, g, prm['ssm_w_dt'][j])
    return z, xbc, dt


def _ssd_args(j, prm):
    return (prm['ssm_conv_w'][j], prm['ssm_conv_b'][j], prm['ssm_dt_bias'][j], prm['ssm_a_log'][j],
            prm['ssm_d_lanes'][j], prm['ssm_norm'][j], prm['ssm_expand'])


def _ssd_layer_prompt(h, B, L, j, g, prm):
    z, xbc, dt = _ssd_proj(h, g, j, prm)
    y, buf, st = ssd_seq(z.reshape(B, L, -1), xbc.reshape(B, L, -1), dt.reshape(B, L, -1), *_ssd_args(j, prm))
    return matmul_res(y.reshape(B * L, -1), prm['ssm_w_out'][j], h), buf, st


def _ssd_layer_sample(h, buf, h0, j, g, prm):
    B = h.shape[0]
    z, xbc, dt = _ssd_proj(h, g, j, prm)
    y, nbuf, st = ssd_step(z.reshape(B, 1, -1), xbc.reshape(B, 1, -1), dt.reshape(B, 1, -1), buf, h0,
                           *_ssd_args(j, prm))
    return matmul_res(y.reshape(B, -1), prm['ssm_w_out'][j], h), nbuf, st


def _fox_proj(h, g, j, prm):
    q = norm_matmul(h, g, prm['fox_w_q'][j])
    k = norm_matmul(h, g, prm['fox_w_k'][j])
    v = norm_matmul(h, g, prm['fox_w_v'][j])
    og = norm_matmul(h, g, prm['fox_w_g'][j])
    fl = norm_matmul_hi(h, g, prm['fox_w_f'][j])
    return q, k, v, og, fl


def _fox_prep_args(j, prm):
    return (prm['fox_b_f'][j], prm['fox_q_norm'][j], prm['fox_k_norm'][j], prm['fox_head_sum'],
            prm['fox_head_expand'], prm['fox_place'])


def _fox_layer_prompt(h, B, L, j, g, prm, tq=512):
    q, k, v, og, fl = _fox_proj(h, g, j, prm)
    shp = (B, L, -1)
    qs, _, logf, kx, vt, kt32, vt32 = fox_prep(q.reshape(shp), k.reshape(shp), v.reshape(shp), fl.reshape(shp),
                                               *_fox_prep_args(j, prm))
    o = fox_attn(qs, kx, vt, og.reshape(shp), tq=tq)

    def per_head(xt):
        return jnp.transpose(xt.reshape(B, -1, ATT_HD, L), (0, 3, 1, 2))

    return matmul_res(o.reshape(B * L, -1), prm['fox_w_out'][j], h), per_head(kt32), per_head(vt32), logf


def _fox_layer_sample(h, cache, n_phys, page_table, j, g, prm):
    B, D = h.shape
    q, k, v, og, fl = _fox_proj(h, g, j, prm)
    one = (1, B, -1)
    qs, kn, logf = fox_prep(q.reshape(one), k.reshape(one), v.reshape(one), fl.reshape(one),
                            *_fox_prep_args(j, prm))[:3]
    ck, cv, clf = cache
    heads = (B, D // ATT_HD, ATT_HD)
    tok = (B, 1, -1)
    o = fox_decode(qs.astype(F32).reshape(heads), kn.reshape(heads), v.reshape(heads), logf.reshape(tok),
                   og.reshape(heads), ck, cv, clf, page_table + j * n_phys)
    return matmul_res(o.reshape(B, D), prm['fox_w_out'][j], h), kn.reshape(tok), v.reshape(tok), logf.reshape(tok)


def _prepare_params(raw):
    prm = dict(raw)
    D = raw['norm_final'].shape[0]
    for name in ('lru_w_in', 'lru_w_out', 'ssm_w_out', 'fox_w_out', 'moe_w_gate', 'moe_w_up', 'moe_w_down',
                 'ple_w_proj', 'ple_w_gate'):
        prm[name] = raw[name].astype(BF16)
    per = 2 * LANES // (D // LRU_BLOCKS)
    prm['lru_wa_bd'] = jax.vmap(lambda w: _block_diag_chunks(w, per))(raw['lru_w_a']).astype(BF16)
    prm['lru_wi_bd'] = jax.vmap(lambda w: _block_diag_chunks(w, per))(raw['lru_w_i']).astype(BF16)
    prm['lru_b_a'] = raw['lru_b_a'].reshape(raw['lru_b_a'].shape[0], -1)
    prm['lru_b_i'] = raw['lru_b_i'].reshape(raw['lru_b_i'].shape[0], -1)
    n_h = raw['ssm_a_log'].shape[1]
    di = n_h * SSM_P
    w = raw['ssm_w_in']
    conv_dim = raw['ssm_conv_w'].shape[2]
    prm['ssm_w_z'] = w[:, :, :di].astype(BF16)
    prm['ssm_w_xbc'] = w[:, :, di:di + conv_dim].astype(BF16)
    prm['ssm_w_dt'] = jax.vmap(lambda m: _pad_cols(m, LANES))(w[:, :, di + conv_dim:])
    prm['ssm_dt_bias'] = jax.vmap(_pad_lanes)(raw['ssm_dt_bias'])
    prm['ssm_a_log'] = jax.vmap(_pad_lanes)(raw['ssm_a_log'])
    prm['ssm_d_lanes'] = jnp.repeat(raw['ssm_d'], SSM_P, axis=1)[:, None, :]
    prm['ssm_expand'] = _head_expand(n_h, SSM_P).astype(BF16)
    w = raw['fox_w_in']
    for n, name in enumerate(('fox_w_q', 'fox_w_k', 'fox_w_v', 'fox_w_g')):
        prm[name] = w[:, :, n * D:(n + 1) * D].astype(BF16)
    prm['fox_w_f'] = jax.vmap(lambda m: _pad_cols(m, LANES))(w[:, :, 4 * D:])
    prm['fox_b_f'] = jax.vmap(_pad_lanes)(raw['fox_b_f'])
    n_ah = D // ATT_HD
    prm['fox_q_norm'] = jnp.tile(raw['fox_q_norm'], (1, n_ah))[:, None, :]
    prm['fox_k_norm'] = jnp.tile(raw['fox_k_norm'], (1, n_ah))[:, None, :]
    prm['fox_head_expand'] = _head_expand(n_ah, ATT_HD)
    prm['fox_head_sum'] = prm['fox_head_expand'].T.astype(BF16)
    hh = jnp.arange(LANES)[None, :, None]
    jj = jnp.arange(N_PIECES)[:, None, None]
    col = jnp.arange(D)[None, None, :]
    prm['fox_place'] = ((hh < n_ah) & (col == LANES * (hh // PAIR) + N_PIECES * (hh % PAIR) + jj)).astype(BF16)
    prm['moe_w_router'] = jax.vmap(lambda we, wg: _pad_cols(jnp.concatenate([we, wg], axis=1), LANES))(
        raw['moe_w_expert'], raw['moe_w_group'])
    prm['moe_b_router'] = jax.vmap(lambda be, bg: _pad_lanes(jnp.concatenate([be, bg])))(
        raw['moe_b_expert'], raw['moe_b_group'])
    return prm


def kernel(x_prompt, x_sample, state_lru_h, state_lru_conv, state_ssm_h, state_ssm_conv, cache_k, cache_v, cache_logf, page_table, p_prompt, p_sample, lru_w_in, lru_conv_w, lru_conv_b, lru_w_a, lru_b_a, lru_w_i, lru_b_i, lru_lambda, lru_w_out, ssm_w_in, ssm_conv_w, ssm_conv_b, ssm_dt_bias, ssm_a_log, ssm_d, ssm_norm, ssm_w_out, fox_w_in, fox_b_f, fox_q_norm, fox_k_norm, fox_w_out, moe_w_group, moe_b_group, moe_w_expert, moe_b_expert, moe_w_gate, moe_w_up, moe_w_down, ple_w_proj, ple_w_gate, norm_mix, norm_ffn, norm_ple, norm_final):
    prm = _prepare_params(dict(
        lru_w_in=lru_w_in, lru_conv_w=lru_conv_w, lru_conv_b=lru_conv_b, lru_w_a=lru_w_a, lru_b_a=lru_b_a,
        lru_w_i=lru_w_i, lru_b_i=lru_b_i, lru_lambda=lru_lambda, lru_w_out=lru_w_out, ssm_w_in=ssm_w_in,
        ssm_conv_w=ssm_conv_w, ssm_conv_b=ssm_conv_b, ssm_dt_bias=ssm_dt_bias, ssm_a_log=ssm_a_log, ssm_d=ssm_d,
        ssm_norm=ssm_norm, ssm_w_out=ssm_w_out, fox_w_in=fox_w_in, fox_b_f=fox_b_f, fox_q_norm=fox_q_norm,
        fox_k_norm=fox_k_norm, fox_w_out=fox_w_out, moe_w_group=moe_w_group, moe_b_group=moe_b_group,
        moe_w_expert=moe_w_expert, moe_b_expert=moe_b_expert, moe_w_gate=moe_w_gate, moe_w_up=moe_w_up,
        moe_w_down=moe_w_down, ple_w_proj=ple_w_proj, ple_w_gate=ple_w_gate, norm_mix=norm_mix,
        norm_ffn=norm_ffn, norm_ple=norm_ple, norm_final=norm_final))
    depth = norm_mix.shape[0]
    B, L, D = x_prompt.shape
    Bs = x_sample.shape[0]
    n_mix = 3
    att_h = D // ATT_HD
    n_phys, page = cache_k.shape[1], cache_k.shape[2]
    cache = (jnp.transpose(cache_k, (0, 1, 3, 4, 2)).reshape(-1, att_h, ATT_HD, page),
             jnp.transpose(cache_v, (0, 1, 3, 4, 2)).reshape(-1, att_h, ATT_HD, page),
             jnp.transpose(cache_logf, (0, 1, 3, 2)).reshape(-1, att_h, page))

    hp = x_prompt.reshape(B * L, D)
    hs = x_sample.reshape(Bs, D)
    outs = {k: [] for k in ('lru_h_p', 'lru_h_s', 'lru_c_p', 'lru_c_s', 'ssm_h_p', 'ssm_h_s', 'ssm_c_p', 'ssm_c_s',
                            'k_p', 'k_s', 'v_p', 'v_s', 'lf_p', 'lf_s')}
    yp = ys = None
    for i in range(depth):
        j = i // n_mix
        g = prm['norm_mix'][i]
        if i % n_mix == 0:
            hp, buf, hl = _lru_layer_prompt(hp, B, L, j, g, prm)
            outs['lru_c_p'].append(buf)
            outs['lru_h_p'].append(hl)
            hs, buf, hl = _lru_layer_sample(hs, state_lru_conv[j], state_lru_h[j], j, g, prm)
            outs['lru_c_s'].append(buf)
            outs['lru_h_s'].append(hl)
        elif i % n_mix == 1:
            hp, buf, st = _ssd_layer_prompt(hp, B, L, j, g, prm)
            outs['ssm_c_p'].append(buf)
            outs['ssm_h_p'].append(st)
            hs, buf, st = _ssd_layer_sample(hs, state_ssm_conv[j], state_ssm_h[j], j, g, prm)
            outs['ssm_c_s'].append(buf)
            outs['ssm_h_s'].append(st)
        else:
            hp, k, v, lf = _fox_layer_prompt(hp, B, L, j, g, prm)
            outs['k_p'].append(k)
            outs['v_p'].append(v)
            outs['lf_p'].append(lf[:, :, :att_h])
            hs, k, v, lf = _fox_layer_sample(hs, cache, n_phys, page_table, j, g, prm)
            outs['k_s'].append(k.reshape(Bs, 1, att_h, ATT_HD))
            outs['v_s'].append(v.reshape(Bs, 1, att_h, ATT_HD))
            outs['lf_s'].append(lf[:, :, :att_h])
        last = i == depth - 1
        hp = _moe_and_ple(hp, p_prompt[i].reshape(B * L, -1), i, prm, last)
        hs = _moe_and_ple(hs, p_sample[i].reshape(Bs, -1), i, prm, last)
        if last:
            hp, yp = hp
            hs, ys = hs
    st = {k: jnp.stack(v) for k, v in outs.items()}
    return (yp.reshape(B, L, D), ys.reshape(Bs, 1, D), st['lru_h_p'], st['lru_h_s'], st['lru_c_p'], st['lru_c_s'],
            st['ssm_h_p'], st['ssm_h_s'], st['ssm_c_p'], st['ssm_c_s'], st['k_p'], st['k_s'], st['v_p'], st['v_s'],
            st['lf_p'], st['lf_s'])
```

```python
import functools

import jax
import jax.numpy as jnp
from jax import lax
from jax.experimental import pallas as pl
from jax.experimental.pallas import tpu as pltpu

F32 = jnp.float32
BF16 = jnp.bfloat16
HI = lax.Precision.HIGHEST

EPS = 1e-6
CONV_W = 4
LANES = 128
SUBLANES = 8
LRU_C = 8.0
LRU_BLOCKS = 16
SSM_P = 64
SSM_G = 8
SSM_N = 128
ATT_HD = 64
MOE_GROUPS = 4
MOE_EPG = 4
MOE_E = MOE_GROUPS * MOE_EPG
NEG = -1e30
MIB = 1024 * 1024


def _cparams(sem, vmem_mib=48):
    return pltpu.CompilerParams(dimension_semantics=sem, vmem_limit_bytes=vmem_mib * MIB)


def _rms(x, g):
    ms = jnp.mean(x * x, axis=-1, keepdims=True)
    return x * lax.rsqrt(ms + EPS) * g


def _softplus(z):
    return jnp.maximum(z, 0.0) + jnp.log1p(jnp.exp(-jnp.abs(z)))


def _sigmoid(z):
    return 1.0 / (1.0 + jnp.exp(-z))


def _silu(z):
    return z * _sigmoid(z)


def _gelu_tanh(z):
    c = 0.7978845608028654
    return 0.5 * z * (1.0 + jnp.tanh(c * (z + 0.044715 * (z * z * z))))


def _dot(a, b):
    return jnp.dot(a, b, preferred_element_type=F32)


def _dot_hi(a, b):
    return jnp.dot(a, b, preferred_element_type=F32, precision=HI)


def _dot_nt(a, b):
    return lax.dot_general(a, b, (((1,), (1,)), ((), ())), preferred_element_type=F32)


def _dot_nt_hi(a, b):
    return lax.dot_general(a, b, (((1,), (1,)), ((), ())), preferred_element_type=F32, precision=HI)


def _place_dot(x, m, pieces=3):
    out = None
    rest = x
    for _ in range(pieces):
        piece = rest.astype(BF16)
        rest = rest - piece.astype(F32)
        out = _dot(piece, m) if out is None else out + _dot(piece, m)
    return out


def _pad_cols(w, n):
    return jnp.pad(w, ((0, 0), (0, n - w.shape[1])))


def _norm_matmul_kernel(x_ref, g_ref, w_ref, o_ref, xn_ref):
    @pl.when(pl.program_id(1) == 0)
    def _():
        xn_ref[...] = _rms(x_ref[...], g_ref[...]).astype(xn_ref.dtype)

    o_ref[...] = _dot(xn_ref[...], w_ref[...]).astype(o_ref.dtype)


def _norm_matmul_hi_kernel(x_ref, g_ref, w_ref, o_ref, *, exact):
    xn = _rms(x_ref[...], g_ref[...])
    o_ref[...] = _dot_hi(xn, w_ref[...]) if exact else _dot_x3(xn, w_ref[...])


def norm_matmul(x, g, w, *, tm=1024, tn=1024, out_dtype=F32):
    T, D = x.shape
    N = w.shape[1]
    tm, tn = min(tm, T), min(tn, N)
    assert T % tm == 0 and N % tn == 0
    return pl.pallas_call(
        _norm_matmul_kernel,
        grid=(T // tm, N // tn),
        in_specs=[pl.BlockSpec((tm, D), lambda i, j: (i, 0)),
                  pl.BlockSpec((1, D), lambda i, j: (0, 0)),
                  pl.BlockSpec((D, tn), lambda i, j: (0, j))],
        out_specs=pl.BlockSpec((tm, tn), lambda i, j: (i, j)),
        out_shape=jax.ShapeDtypeStruct((T, N), out_dtype),
        scratch_shapes=[pltpu.VMEM((tm, D), BF16)],
        compiler_params=_cparams(("parallel", "arbitrary")),
        name="norm_matmul",
    )(x, g.reshape(1, D), w)


def norm_matmul_hi(x, g, w, *, exact=True, tm=512):
    T, D = x.shape
    N = w.shape[1]
    tm = min(tm, T)
    assert T % tm == 0
    return pl.pallas_call(
        functools.partial(_norm_matmul_hi_kernel, exact=exact),
        grid=(T // tm,),
        in_specs=[pl.BlockSpec((tm, D), lambda i: (i, 0)),
                  pl.BlockSpec((1, D), lambda i: (0, 0)),
                  pl.BlockSpec((D, N), lambda i: (0, 0))],
        out_specs=pl.BlockSpec((tm, N), lambda i: (i, 0)),
        out_shape=jax.ShapeDtypeStruct((T, N), F32),
        compiler_params=_cparams(("parallel",)),
        name="norm_matmul_hi",
    )(x, g.reshape(1, D), w)


def _matmul_res_kernel(a_ref, w_ref, r_ref, *rest):
    out = r_ref[...] + _dot(a_ref[...], w_ref[...])
    if len(rest) == 1:
        rest[0][...] = out
    else:
        g_ref, wr_ref, br_ref, o_ref, cls_ref = rest
        o_ref[...] = out
        cls_ref[...] = jnp.broadcast_to(_moe_class(out, g_ref, wr_ref, br_ref), cls_ref.shape)


def matmul_res(a, w, res, route=None, *, tm=512):
    T, K = a.shape
    N = w.shape[1]
    tm = min(tm, T)
    assert T % tm == 0
    row = pl.BlockSpec((tm, N), lambda i: (i, 0))
    in_specs = [pl.BlockSpec((tm, K), lambda i: (i, 0)), pl.BlockSpec((K, N), lambda i: (0, 0)), row]
    args = (a, w, res)
    out_specs, out_shape = row, jax.ShapeDtypeStruct((T, N), F32)
    if route is not None:
        g, wr, br = route
        in_specs += [pl.BlockSpec((1, N), lambda i: (0, 0)), pl.BlockSpec((N, LANES), lambda i: (0, 0)),
                     pl.BlockSpec((1, LANES), lambda i: (0, 0))]
        args += (g.reshape(1, N), wr, br)
        out_specs = (row, pl.BlockSpec((tm, LANES), lambda i: (i, 0)))
        out_shape = (out_shape, jax.ShapeDtypeStruct((T, LANES), jnp.int32))
    return pl.pallas_call(
        _matmul_res_kernel,
        grid=(T // tm,),
        in_specs=in_specs,
        out_specs=out_specs,
        out_shape=out_shape,
        compiler_params=_cparams(("parallel",)),
        name="matmul_res",
    )(*args)


def _ple_kernel(h_ref, p_ref, g_ref, wg_ref, wp_ref, gf_ref, o_ref, *maybe_final):
    h = h_ref[...]
    xn = _rms(h, g_ref[...]).astype(BF16)
    gate = _sigmoid(_dot(xn, wg_ref[...]))
    out = h + gate * _dot(p_ref[...].astype(BF16), wp_ref[...])
    o_ref[...] = out
    if maybe_final:
        maybe_final[0][...] = _rms(out, gf_ref[...])


def ple(h, p, g, w_gate, w_proj, g_final=None, *, tm=512):
    T, D = h.shape
    P = p.shape[1]
    tm = min(tm, T)
    assert T % tm == 0
    final = g_final is not None
    gf = (g_final if final else g).reshape(1, D)
    row = pl.BlockSpec((tm, D), lambda i: (i, 0))
    vec = pl.BlockSpec((1, D), lambda i: (0, 0))
    out_shape = jax.ShapeDtypeStruct((T, D), F32)
    return pl.pallas_call(
        _ple_kernel,
        grid=(T // tm,),
        in_specs=[row, pl.BlockSpec((tm, P), lambda i: (i, 0)), vec,
                  pl.BlockSpec((D, D), lambda i: (0, 0)),
                  pl.BlockSpec((P, D), lambda i: (0, 0)), vec],
        out_specs=(row, row) if final else row,
        out_shape=(out_shape, out_shape) if final else out_shape,
        compiler_params=_cparams(("parallel",)),
        name="ple",
    )(h, p, g.reshape(1, D), w_gate, w_proj, gf)


def _route(logits):
    lane = lax.broadcasted_iota(jnp.int32, logits.shape, 1)
    big = jnp.int32(1 << 20)
    is_g = (lane >= MOE_E) & (lane < MOE_E + MOE_GROUPS)
    glog = jnp.where(is_g, logits, NEG)
    gmax = jnp.max(glog, axis=-1, keepdims=True)
    gsel = jnp.min(jnp.where(is_g & (glog == gmax), lane, big), axis=-1, keepdims=True) - MOE_E
    gden = jnp.sum(jnp.where(is_g, jnp.exp(glog - gmax), 0.0), axis=-1, keepdims=True)
    gprob = 1.0 / gden
    in_g = (lane < MOE_E) & ((lane // MOE_EPG) == gsel)
    e1 = jnp.where(in_g, logits, NEG)
    v1 = jnp.max(e1, axis=-1, keepdims=True)
    i1 = jnp.min(jnp.where(in_g & (e1 == v1), lane, big), axis=-1, keepdims=True)
    in_g2 = in_g & (lane != i1)
    e2 = jnp.where(in_g2, logits, NEG)
    v2 = jnp.max(e2, axis=-1, keepdims=True)
    i2 = jnp.min(jnp.where(in_g2 & (e2 == v2), lane, big), axis=-1, keepdims=True)
    t = jnp.exp(v2 - v1)
    w1 = gprob / (1.0 + t)
    w2 = gprob * t / (1.0 + t)
    return jnp.where(lane == i1, w1, jnp.where(lane == i2, w2, 0.0))


def _moe_dense_kernel(h_ref, g_ref, wr_ref, br_ref, wg_ref, wu_ref, wd_ref, o_ref, xn_ref, comb_ref, acc_ref):
    e = pl.program_id(1)

    @pl.when(e == 0)
    def _():
        xn = _rms(h_ref[...], g_ref[...])
        comb_ref[...] = _route(_dot_hi(xn, wr_ref[...]) + br_ref[...])
        xn_ref[...] = xn.astype(BF16)
        acc_ref[...] = jnp.zeros_like(acc_ref)

    xn = xn_ref[...]
    hg = _dot(xn, wg_ref[0])
    hu = _dot(xn, wu_ref[0])
    comb = comb_ref[...]
    lane = lax.broadcasted_iota(jnp.int32, comb.shape, 1)
    c = jnp.sum(jnp.where(lane == e, comb, 0.0), axis=-1, keepdims=True)
    hid = (_silu(hg) * hu * c).astype(BF16)
    acc_ref[...] += _dot(hid, wd_ref[0])

    @pl.when(e == pl.num_programs(1) - 1)
    def _():
        o_ref[...] = h_ref[...] + acc_ref[...]


def moe_dense(h, g, w_router, b_router, w_gate, w_up, w_down, *, tm=1024):
    T, D = h.shape
    E, _, Fd = w_gate.shape
    tm = min(tm, T)
    assert T % tm == 0
    row = pl.BlockSpec((tm, D), lambda i, e: (i, 0))
    return pl.pallas_call(
        _moe_dense_kernel,
        grid=(T // tm, E),
        in_specs=[row, pl.BlockSpec((1, D), lambda i, e: (0, 0)),
                  pl.BlockSpec((D, LANES), lambda i, e: (0, 0)),
                  pl.BlockSpec((1, LANES), lambda i, e: (0, 0)),
                  pl.BlockSpec((1, D, Fd), lambda i, e: (e, 0, 0)),
                  pl.BlockSpec((1, D, Fd), lambda i, e: (e, 0, 0)),
                  pl.BlockSpec((1, Fd, D), lambda i, e: (e, 0, 0))],
        out_specs=row,
        out_shape=jax.ShapeDtypeStruct((T, D), F32),
        scratch_shapes=[pltpu.VMEM((tm, D), BF16), pltpu.VMEM((tm, LANES), F32), pltpu.VMEM((tm, D), F32)],
        compiler_params=_cparams(("parallel", "arbitrary")),
        name="moe_dense",
    )(h, g.reshape(1, D), w_router, b_router, w_gate, w_up, w_down)


MOE_PAIRS = MOE_EPG * (MOE_EPG - 1) // 2
MOE_CLASSES = MOE_GROUPS * MOE_PAIRS


def _route_select(logits):
    lane = lax.broadcasted_iota(jnp.int32, logits.shape, 1)
    big = jnp.int32(1 << 20)
    is_g = (lane >= MOE_E) & (lane < MOE_E + MOE_GROUPS)
    glog = jnp.where(is_g, logits, NEG)
    gmax = jnp.max(glog, axis=-1, keepdims=True)
    gsel = jnp.min(jnp.where(is_g & (glog == gmax), lane, big), axis=-1, keepdims=True) - MOE_E
    in_g = (lane < MOE_E) & ((lane // MOE_EPG) == gsel)
    e1 = jnp.where(in_g, logits, NEG)
    v1 = jnp.max(e1, axis=-1, keepdims=True)
    i1 = jnp.min(jnp.where(in_g & (e1 == v1), lane, big), axis=-1, keepdims=True)
    in_g2 = in_g & (lane != i1)
    e2 = jnp.where(in_g2, logits, NEG)
    v2 = jnp.max(e2, axis=-1, keepdims=True)
    i2 = jnp.min(jnp.where(in_g2 & (e2 == v2), lane, big), axis=-1, keepdims=True)
    return gsel, i1, i2


def _dot_x3(x, w):
    xh, wh = x.astype(BF16), w.astype(BF16)
    xl, wl = (x - xh.astype(F32)).astype(BF16), (w - wh.astype(F32)).astype(BF16)
    return _dot(xh, wh) + _dot(xl, wh) + _dot(xh, wl)


def _moe_class(h, g_ref, wr_ref, br_ref):
    gsel, i1, i2 = _route_select(_dot_x3(_rms(h, g_ref[...]), wr_ref[...]) + br_ref[...])
    a = jnp.minimum(i1, i2) - gsel * MOE_EPG
    b = jnp.maximum(i1, i2) - gsel * MOE_EPG
    return gsel * MOE_PAIRS + (a * (2 * MOE_EPG - 1 - a)) // 2 + (b - a - 1)


def _moe_plan(cls, tr):
    T = cls.shape[0]
    n_tiles = T // tr + MOE_CLASSES
    order = jnp.argsort(cls, stable=True).astype(jnp.int32)
    counts = jnp.sum(cls[:, None] == jnp.arange(MOE_CLASSES, dtype=jnp.int32)[None, :], axis=0, dtype=jnp.int32)
    tiles_per = (counts + tr - 1) // tr
    tile_end = jnp.cumsum(tiles_per)
    tile_ids = jnp.arange(n_tiles, dtype=jnp.int32)
    tcls = jnp.minimum(jnp.sum(tile_ids[:, None] >= tile_end[None, :], axis=1, dtype=jnp.int32), MOE_CLASSES - 1)
    k = tile_ids - (tile_end - tiles_per)[tcls]
    n_rows = jnp.where(tile_ids < tile_end[-1], jnp.clip(counts[tcls] - k * tr, 0, tr), 0).astype(jnp.int32)
    first = (jnp.cumsum(counts) - counts)[tcls] + k * tr
    rows = jnp.arange(tr, dtype=jnp.int32)[None, :]
    src = jnp.where(rows < n_rows[:, None], order[jnp.clip(first[:, None] + rows, 0, T - 1)], 0).astype(jnp.int32)
    pair = tcls % MOE_PAIRS
    grp = tcls // MOE_PAIRS
    pa = jnp.array([a for a in range(MOE_EPG) for b in range(a + 1, MOE_EPG)], jnp.int32)[pair]
    pb = jnp.array([b for a in range(MOE_EPG) for b in range(a + 1, MOE_EPG)], jnp.int32)[pair]
    return grp * MOE_EPG + pa, grp * MOE_EPG + pb, grp, n_rows, src


def _moe_routed_kernel(e1_ref, e2_ref, grp_ref, nrow_ref, prev_ref, src_ref, next_ref, h_ref, g_ref, wr_ref, br_ref,
                       wg1_ref, wu1_ref, wd1_ref, wg2_ref, wu2_ref, wd2_ref, o_ref, x_ref, y_ref, sem_in, sem_out):
    i = pl.program_id(0)
    last = pl.num_programs(0) - 1
    tr = x_ref.shape[1]
    n_rows = nrow_ref[i]
    slot = i % 2
    n_prev = jnp.where(i > 0, nrow_ref[jnp.maximum(i - 1, 0)], 0)

    def row_in(idx_ref, s, r):
        return pltpu.make_async_copy(h_ref.at[pl.ds(idx_ref[0, 0, r], 1)], x_ref.at[s, pl.ds(r, 1)], sem_in.at[s])

    def row_out(idx_ref, s, r):
        return pltpu.make_async_copy(y_ref.at[s, pl.ds(r, 1)], o_ref.at[pl.ds(idx_ref[0, 0, r], 1)], sem_out.at[s])

    def each(n, fn):
        def body8(j, carry):
            for u in range(SUBLANES):
                fn(j * SUBLANES + u)
            return carry

        def body1(r, carry):
            fn(r)
            return carry

        whole = n // SUBLANES
        lax.fori_loop(0, whole, body8, 0)
        if not isinstance(n, int) or n % SUBLANES:
            lax.fori_loop(whole * SUBLANES, n, body1, 0)

    @pl.when((i == 0) & (n_rows > 0))
    def _():
        each(tr, lambda r: row_in(src_ref, slot, r).start())

    @pl.when(((i == 0) & (n_rows > 0)) | (n_prev > 0))
    def _():
        each(tr, lambda r: row_in(src_ref, slot, r).wait())

    @pl.when(n_rows > 0)
    def _():
        for r in range(tr):
            row_in(next_ref, 1 - slot, r).start()
        x = x_ref[slot]
        xn = _rms(x, g_ref[...])
        logits = _dot_x3(xn, wr_ref[...]) + br_ref[...]
        lane = lax.broadcasted_iota(jnp.int32, logits.shape, 1)

        def pick(idx):
            return jnp.sum(jnp.where(lane == idx, logits, 0.0), axis=-1, keepdims=True)

        la, lb, lg = pick(e1_ref[i]), pick(e2_ref[i]), pick(MOE_E + grp_ref[i])
        is_g = (lane >= MOE_E) & (lane < MOE_E + MOE_GROUPS)
        gprob = 1.0 / jnp.sum(jnp.where(is_g, jnp.exp(logits - lg), 0.0), axis=-1, keepdims=True)
        top = jnp.maximum(la, lb)
        pa, pb = jnp.exp(la - top), jnp.exp(lb - top)
        scale = gprob / (pa + pb)
        xb = xn.astype(BF16)
        hid1 = (_silu(_dot(xb, wg1_ref[0])) * _dot(xb, wu1_ref[0]) * (pa * scale)).astype(BF16)
        hid2 = (_silu(_dot(xb, wg2_ref[0])) * _dot(xb, wu2_ref[0]) * (pb * scale)).astype(BF16)
        y_ref[slot] = x + _dot(hid1, wd1_ref[0]) + _dot(hid2, wd2_ref[0])

    each(n_prev, lambda r: row_out(prev_ref, 1 - slot, r).wait())
    each(n_rows, lambda r: row_out(src_ref, slot, r).start())

    @pl.when(i == last)
    def _():
        each(n_rows, lambda r: row_out(src_ref, slot, r).wait())


def moe_routed(h, cls, g, w_router, b_router, w_gate, w_up, w_down, *, tr=256):
    T, D = h.shape
    E, _, Fd = w_gate.shape
    assert T % tr == 0
    e1, e2, grp, n_rows, src = _moe_plan(cls, tr)
    n_tiles = src.shape[0]

    def wspec(shape, which):
        return pl.BlockSpec((1,) + shape, lambda i, e1, e2, grp, nr: ((e1, e2)[which][i], 0, 0))

    const = lambda shape: pl.BlockSpec(shape, lambda i, e1, e2, grp, nr: (0,) * len(shape))

    def rows_of(shift):
        return pl.BlockSpec((1, 1, tr), lambda i, e1, e2, grp, nr: (jnp.clip(i + shift, 0, n_tiles - 1), 0, 0),
                            memory_space=pltpu.SMEM)

    grid_spec = pltpu.PrefetchScalarGridSpec(
        num_scalar_prefetch=4, grid=(n_tiles,),
        in_specs=[rows_of(-1), rows_of(0), rows_of(1),
                  pl.BlockSpec(memory_space=pl.ANY), const((1, D)), const((D, LANES)), const((1, LANES)),
                  wspec((D, Fd), 0), wspec((D, Fd), 0), wspec((Fd, D), 0),
                  wspec((D, Fd), 1), wspec((D, Fd), 1), wspec((Fd, D), 1)],
        out_specs=pl.BlockSpec(memory_space=pl.ANY),
        scratch_shapes=[pltpu.VMEM((2, tr, D), F32), pltpu.VMEM((2, tr, D), F32),
                        pltpu.SemaphoreType.DMA((2,)), pltpu.SemaphoreType.DMA((2,))])
    src3 = src.reshape(n_tiles, 1, tr)
    return pl.pallas_call(
        _moe_routed_kernel,
        grid_spec=grid_spec,
        out_shape=jax.ShapeDtypeStruct((T, D), F32),
        compiler_params=_cparams(("arbitrary",)),
        name="moe_routed",
    )(e1, e2, grp, n_rows, src3, src3, src3, h, g.reshape(1, D), w_router, b_router,
      w_gate, w_up, w_down, w_gate, w_up, w_down)


def _conv_tile(xpad_ref, x, cw_ref, cb_ref, first):
    tl = x.shape[0]

    @pl.when(first)
    def _():
        xpad_ref[0:SUBLANES, :] = jnp.zeros((SUBLANES, x.shape[1]), F32)

    xpad_ref[SUBLANES:SUBLANES + tl, :] = x
    out = cb_ref[...] + x * cw_ref[CONV_W - 1:CONV_W, :]
    for k in range(CONV_W - 1):
        lo = SUBLANES - (CONV_W - 1) + k
        out = out + xpad_ref[lo:lo + tl, :] * cw_ref[k:k + 1, :]
    tail = xpad_ref[tl:tl + SUBLANES, :]
    xpad_ref[0:SUBLANES, :] = tail
    return out, tail[SUBLANES - (CONV_W - 1):, :]


def _lru_gates(xc, wa_ref, ba_ref, wi_ref, bi_ref, lam_ref):
    xcb = xc.astype(BF16)
    nb = wa_ref.shape[0]
    cw = wa_ref.shape[1]
    ra = jnp.concatenate([_dot(xcb[:, c * cw:(c + 1) * cw], wa_ref[c]) for c in range(nb)], axis=1)
    ia = jnp.concatenate([_dot(xcb[:, c * cw:(c + 1) * cw], wi_ref[c]) for c in range(nb)], axis=1)
    r = _sigmoid(ra + ba_ref[...])
    ig = _sigmoid(ia + bi_ref[...])
    log_a = (-LRU_C) * r * _softplus(-lam_ref[...])
    a = jnp.exp(log_a)
    b = jnp.sqrt(-jnp.tanh(log_a) * (a * a + 1.0)) * (ig * xc)
    return a, b


def _lru_seq_kernel(xb_ref, yb_ref, cw_ref, cb_ref, wa_ref, ba_ref, wi_ref, bi_ref, lam_ref,
                    o_ref, buf_ref, hl_ref, xpad_ref, a_ref, b_ref, hc_ref):
    t = pl.program_id(1)
    tl, W = xb_ref.shape[1], xb_ref.shape[2]
    xc, tail = _conv_tile(xpad_ref, xb_ref[0], cw_ref, cb_ref, t == 0)
    a, b = _lru_gates(xc, wa_ref, ba_ref, wi_ref, bi_ref, lam_ref)
    a_ref[...] = a
    b_ref[...] = b

    @pl.when(t == 0)
    def _():
        hc_ref[...] = jnp.zeros_like(hc_ref)

    row = lax.broadcasted_iota(jnp.int32, (SUBLANES, W), 0)

    def body(i, hc):
        r0 = pl.multiple_of(i * SUBLANES, SUBLANES)
        av = a_ref[pl.ds(r0, SUBLANES), :]
        bv = b_ref[pl.ds(r0, SUBLANES), :]
        for s in (1, 2, 4):
            keep = row >= s
            a_sh = pltpu.roll(av, s, 0)
            b_sh = pltpu.roll(bv, s, 0)
            bv = jnp.where(keep, av * b_sh + bv, bv)
            av = jnp.where(keep, av * a_sh, av)
        h = av * hc + bv
        b_ref[pl.ds(r0, SUBLANES), :] = h
        return jnp.broadcast_to(h[SUBLANES - 1:SUBLANES, :], (SUBLANES, W))

    hc = lax.fori_loop(0, tl // SUBLANES, body, hc_ref[...])
    hc_ref[...] = hc
    o_ref[0] = (b_ref[...] * _gelu_tanh(yb_ref[0])).astype(o_ref.dtype)
    buf_ref[0] = tail
    hl_ref[0] = hc[0:1, :]


def lru_seq(xy, conv_w, conv_b, wa_bd, b_a, wi_bd, b_i, lam, *, tl=256):
    B, L, W2 = xy.shape
    W = W2 // 2
    tl = min(tl, L)
    assert L % tl == 0 and tl % SUBLANES == 0
    vec = pl.BlockSpec((1, W), lambda b, t: (0, 0))
    wbd = pl.BlockSpec(wa_bd.shape, lambda b, t: (0, 0, 0))
    return pl.pallas_call(
        _lru_seq_kernel,
        grid=(B, L // tl),
        in_specs=[pl.BlockSpec((1, tl, W), lambda b, t: (b, t, 0)),
                  pl.BlockSpec((1, tl, W), lambda b, t: (b, t, 1)),
                  pl.BlockSpec((CONV_W, W), lambda b, t: (0, 0)), vec, wbd, vec, wbd, vec, vec],
        out_specs=(pl.BlockSpec((1, tl, W), lambda b, t: (b, t, 0)),
                   pl.BlockSpec((1, CONV_W - 1, W), lambda b, t: (b, 0, 0)),
                   pl.BlockSpec((1, 1, W), lambda b, t: (b, 0, 0))),
        out_shape=(jax.ShapeDtypeStruct((B, L, W), BF16),
                   jax.ShapeDtypeStruct((B, CONV_W - 1, W), F32),
                   jax.ShapeDtypeStruct((B, 1, W), F32)),
        scratch_shapes=[pltpu.VMEM((tl + SUBLANES, W), F32), pltpu.VMEM((tl, W), F32),
                        pltpu.VMEM((tl, W), F32), pltpu.VMEM((SUBLANES, W), F32)],
        compiler_params=_cparams(("parallel", "arbitrary")),
        name="lru_seq",
    )(xy, xy, conv_w, conv_b.reshape(1, W), wa_bd, b_a.reshape(1, W), wi_bd, b_i.reshape(1, W), lam.reshape(1, W))


def _lru_step_kernel(xb_ref, yb_ref, buf_ref, h0_ref, cw_ref, cb_ref, wa_ref, ba_ref, wi_ref, bi_ref, lam_ref,
                     o_ref, nbuf_ref, hl_ref):
    x = xb_ref[...]
    xc = cb_ref[...] + x * cw_ref[CONV_W - 1:CONV_W, :]
    for k in range(CONV_W - 1):
        xc = xc + buf_ref[k] * cw_ref[k:k + 1, :]
    a, b = _lru_gates(xc, wa_ref, ba_ref, wi_ref, bi_ref, lam_ref)
    h = a * h0_ref[...] + b
    o_ref[...] = (h * _gelu_tanh(yb_ref[...])).astype(o_ref.dtype)
    hl_ref[...] = h
    for k in range(CONV_W - 2):
        nbuf_ref[k] = buf_ref[k + 1]
    nbuf_ref[CONV_W - 2] = x


def lru_step(xy, buf_t, h0, conv_w, conv_b, wa_bd, b_a, wi_bd, b_i, lam):
    B, W2 = xy.shape
    W = W2 // 2
    vec = pl.BlockSpec((1, W), lambda i: (0, 0))
    wbd = pl.BlockSpec(wa_bd.shape, lambda i: (0, 0, 0))
    mat = pl.BlockSpec((B, W), lambda i: (0, 0))
    cube = pl.BlockSpec((CONV_W - 1, B, W), lambda i: (0, 0, 0))
    return pl.pallas_call(
        _lru_step_kernel,
        grid=(1,),
        in_specs=[mat, pl.BlockSpec((B, W), lambda i: (0, 1)), cube, mat,
                  pl.BlockSpec((CONV_W, W), lambda i: (0, 0)), vec, wbd, vec, wbd, vec, vec],
        out_specs=(mat, cube, mat),
        out_shape=(jax.ShapeDtypeStruct((B, W), BF16),
                   jax.ShapeDtypeStruct((CONV_W - 1, B, W), F32),
                   jax.ShapeDtypeStruct((B, W), F32)),
        compiler_params=_cparams(("arbitrary",)),
        name="lru_step",
    )(xy, xy, buf_t, h0, conv_w, conv_b.reshape(1, W), wa_bd, b_a.reshape(1, W), wi_bd, b_i.reshape(1, W),
      lam.reshape(1, W))


def _group_norm_gate(y, z, ng, n_groups):
    y = y * _silu(z)
    gw = y.shape[1] // n_groups
    outs = []
    for g in range(n_groups):
        yg = y[:, g * gw:(g + 1) * gw]
        ms = jnp.mean(yg * yg, axis=-1, keepdims=True)
        outs.append(yg * lax.rsqrt(ms + EPS))
    return jnp.concatenate(outs, axis=1) * ng


def _ssd_seq_kernel(z_ref, xbc_ref, dt_ref, cw_ref, cb_ref, dtb_ref, alog_ref, dsk_ref, ng_ref, ex_ref,
                    o_ref, buf_ref, st_ref, xpad_ref, s_ref):
    t = pl.program_id(1)
    Q = xbc_ref.shape[1]
    DI = z_ref.shape[2]
    GN = SSM_G * SSM_N
    RP = DI // SSM_G
    xc, tail = _conv_tile(xpad_ref, xbc_ref[0], cw_ref, cb_ref, t == 0)
    xc = _silu(xc)
    xs = xc[:, :DI]
    bm = xc[:, DI:DI + GN]
    cm = xc[:, DI + GN:]

    @pl.when(t == 0)
    def _():
        s_ref[...] = jnp.zeros_like(s_ref)

    dt = _softplus(dt_ref[0] + dtb_ref[...])
    a_neg = -jnp.exp(alog_ref[...])
    ri = lax.broadcasted_iota(jnp.int32, (Q, Q), 0)
    ci = lax.broadcasted_iota(jnp.int32, (Q, Q), 1)
    tri = ci <= ri
    acum = _dot_hi(tri.astype(F32), dt * a_neg)
    acum_t = acum.T
    ex = ex_ref[...]
    dt_e = _place_dot(dt, ex)
    acum_e = _place_dot(acum, ex)
    last_e = acum_e[Q - 1:Q, :]
    xdt = xs * dt_e
    xdtw = (xdt * jnp.exp(last_e - acum_e)).astype(BF16)
    xdt = xdt.astype(BF16)
    eacum = jnp.exp(acum_e)
    edec = jnp.exp(last_e)
    lane = lax.broadcasted_iota(jnp.int32, (Q, LANES), 1)
    lo = lane < SSM_P
    ys = []
    for g in range(SSM_G):
        cg = cm[:, g * SSM_N:(g + 1) * SSM_N].astype(BF16)
        bg32 = bm[:, g * SSM_N:(g + 1) * SSM_N]
        bg = bg32.astype(BF16)
        cb = _dot_nt(cg, bg)
        st = s_ref[g]
        yoff = _dot(cg, st.astype(BF16)) * eacum[:, g * RP:(g + 1) * RP]
        for pr in range(RP // LANES):
            ms = []
            for k in range(LANES // SSM_P):
                hd = (g * RP + pr * LANES) // SSM_P + k
                seg = acum[:, hd:hd + 1] - acum_t[hd:hd + 1, :]
                ms.append((cb * jnp.exp(jnp.where(tri, seg, NEG))).astype(BF16))
            c0 = g * RP + pr * LANES
            xp = xdt[:, c0:c0 + LANES]
            zero = jnp.zeros_like(xp)
            rhs = jnp.concatenate([jnp.where(lo, xp, zero), jnp.where(lo, zero, xp)], axis=0)
            ydiag = _dot(jnp.concatenate(ms, axis=1), rhs)
            ys.append(ydiag + yoff[:, pr * LANES:(pr + 1) * LANES])
        new = st * edec[:, g * RP:(g + 1) * RP] + _dot(bg32.T.astype(BF16), xdtw[:, g * RP:(g + 1) * RP])
        s_ref[g] = new
    y = jnp.concatenate(ys, axis=1) + xs * dsk_ref[...]
    o_ref[0] = _group_norm_gate(y, z_ref[0], ng_ref[...], SSM_G).astype(o_ref.dtype)
    buf_ref[0] = tail

    @pl.when(t == pl.num_programs(1) - 1)
    def _():
        hpg = RP // SSM_P
        for g in range(SSM_G):
            st_ref[0, g * hpg:(g + 1) * hpg] = s_ref[g].T.reshape(hpg, SSM_P, SSM_N)


def ssd_seq(z, xbc, dt, conv_w, conv_b, dt_bias, a_log, d_lanes, norm_g, expand, *, q=128):
    B, L, DI = z.shape
    C = xbc.shape[2]
    H = DI // SSM_P
    q = min(q, L)
    assert L % q == 0
    vecd = pl.BlockSpec((1, DI), lambda b, t: (0, 0))
    vecl = pl.BlockSpec((1, LANES), lambda b, t: (0, 0))
    return pl.pallas_call(
        _ssd_seq_kernel,
        grid=(B, L // q),
        in_specs=[pl.BlockSpec((1, q, DI), lambda b, t: (b, t, 0)),
                  pl.BlockSpec((1, q, C), lambda b, t: (b, t, 0)),
                  pl.BlockSpec((1, q, LANES), lambda b, t: (b, t, 0)),
                  pl.BlockSpec((CONV_W, C), lambda b, t: (0, 0)),
                  pl.BlockSpec((1, C), lambda b, t: (0, 0)),
                  vecl, vecl, vecd, vecd,
                  pl.BlockSpec((LANES, DI), lambda b, t: (0, 0))],
        out_specs=(pl.BlockSpec((1, q, DI), lambda b, t: (b, t, 0)),
                   pl.BlockSpec((1, CONV_W - 1, C), lambda b, t: (b, 0, 0)),
                   pl.BlockSpec((1, H, SSM_P, SSM_N), lambda b, t: (b, 0, 0, 0))),
        out_shape=(jax.ShapeDtypeStruct((B, L, DI), BF16),
                   jax.ShapeDtypeStruct((B, CONV_W - 1, C), F32),
                   jax.ShapeDtypeStruct((B, H, SSM_P, SSM_N), F32)),
        scratch_shapes=[pltpu.VMEM((q + SUBLANES, C), F32),
                        pltpu.VMEM((SSM_G, SSM_N, DI // SSM_G), F32)],
        compiler_params=_cparams(("parallel", "arbitrary")),
        name="ssd_seq",
    )(z, xbc, dt, conv_w, conv_b.reshape(1, C), dt_bias, a_log, d_lanes, norm_g.reshape(1, DI), expand)


def _ssd_step_kernel(z_ref, xbc_ref, dt_ref, buf_ref, h0_ref, cw_ref, cb_ref, dtb_ref, alog_ref, dsk_ref, ng_ref,
                     ex_ref, o_ref, nbuf_ref, hn_ref):
    DI = z_ref.shape[2]
    GN = SSM_G * SSM_N
    hpg = DI // SSM_P // SSM_G
    x = xbc_ref[0]
    buf = buf_ref[0]
    xc = cb_ref[...] + x * cw_ref[CONV_W - 1:CONV_W, :]
    for k in range(CONV_W - 1):
        xc = xc + buf[k:k + 1, :] * cw_ref[k:k + 1, :]
    nbuf_ref[0, 0:CONV_W - 2, :] = buf[1:CONV_W - 1, :]
    nbuf_ref[0, CONV_W - 2:CONV_W - 1, :] = x
    xc = _silu(xc)
    xs = xc[:, :DI]
    dt = _softplus(dt_ref[0] + dtb_ref[...])
    dta = dt * (-jnp.exp(alog_ref[...]))
    ex = ex_ref[...]
    dec = jnp.exp(dta)
    dt_e = _place_dot(jnp.broadcast_to(dt, (SUBLANES, LANES)), ex)[0:1, :]
    xdt = xs * dt_e
    RP = hpg * SSM_P
    eye = (lax.broadcasted_iota(jnp.int32, (RP, RP), 0) == lax.broadcasted_iota(jnp.int32, (RP, RP), 1))
    ys = []
    for g in range(SSM_G):
        brow = xc[:, DI + g * SSM_N:DI + (g + 1) * SSM_N]
        crow = xc[:, DI + GN + g * SSM_N:DI + GN + (g + 1) * SSM_N]
        xg = xdt[:, g * RP:(g + 1) * RP]
        xcol = jnp.sum(jnp.where(eye, jnp.broadcast_to(xg, (RP, RP)), 0.0), axis=-1, keepdims=True)
        news = []
        for r in range(hpg):
            hd = g * hpg + r
            new = h0_ref[0, hd] * dec[:, hd:hd + 1] + xcol[r * SSM_P:(r + 1) * SSM_P, :] * brow
            hn_ref[0, hd] = new
            news.append(new)
        new_g = jnp.concatenate(news, axis=0).astype(BF16)
        ys.append(_dot_nt(jnp.broadcast_to(crow, (SUBLANES, SSM_N)).astype(BF16), new_g)[0:1, :])
    y = jnp.concatenate(ys, axis=1) + xs * dsk_ref[...]
    o_ref[0] = _group_norm_gate(y, z_ref[0], ng_ref[...], SSM_G).astype(o_ref.dtype)


def ssd_step(z, xbc, dt, buf, h0, conv_w, conv_b, dt_bias, a_log, d_lanes, norm_g, expand):
    B, _, DI = z.shape
    C = xbc.shape[2]
    H = DI // SSM_P
    vecd = pl.BlockSpec((1, DI), lambda b: (0, 0))
    vecl = pl.BlockSpec((1, LANES), lambda b: (0, 0))
    st = pl.BlockSpec((1, H, SSM_P, SSM_N), lambda b: (b, 0, 0, 0))
    return pl.pallas_call(
        _ssd_step_kernel,
        grid=(B,),
        in_specs=[pl.BlockSpec((1, 1, DI), lambda b: (b, 0, 0)),
                  pl.BlockSpec((1, 1, C), lambda b: (b, 0, 0)),
                  pl.BlockSpec((1, 1, LANES), lambda b: (b, 0, 0)),
                  pl.BlockSpec((1, CONV_W - 1, C), lambda b: (b, 0, 0)), st,
                  pl.BlockSpec((CONV_W, C), lambda b: (0, 0)),
                  pl.BlockSpec((1, C), lambda b: (0, 0)),
                  vecl, vecl, vecd, vecd,
                  pl.BlockSpec((LANES, DI), lambda b: (0, 0))],
        out_specs=(pl.BlockSpec((1, 1, DI), lambda b: (b, 0, 0)),
                   pl.BlockSpec((1, CONV_W - 1, C), lambda b: (b, 0, 0)), st),
        out_shape=(jax.ShapeDtypeStruct((B, 1, DI), BF16),
                   jax.ShapeDtypeStruct((B, CONV_W - 1, C), F32),
                   jax.ShapeDtypeStruct((B, H, SSM_P, SSM_N), F32)),
        compiler_params=_cparams(("parallel",)),
        name="ssd_step",
    )(z, xbc, dt, buf, h0, conv_w, conv_b.reshape(1, C), dt_bias, a_log, d_lanes, norm_g.reshape(1, DI), expand)


def _split_dot(x, m):
    hi = x.astype(BF16)
    lo = (x - hi.astype(F32)).astype(BF16)
    return _dot(hi, m) + _dot(lo, m)


LOG2E = 1.4426950408889634
N_PIECES = 3
PAIR = LANES // ATT_HD


def _fox_prep_kernel(q_ref, k_ref, v_ref, fl_ref, bf_ref, qg_ref, kg_ref, hs_ref, he_ref, pl_ref,
                     qo_ref, ko_ref, lf_ref, kx_ref, vt_ref, kt32_ref, vt32_ref, carry_ref):
    t = pl.program_id(1)
    tl = q_ref.shape[1]
    D = q_ref.shape[2]

    def head_norm(x, g):
        ss = _split_dot(x * x, hs_ref[...])
        inv = lax.rsqrt(ss * (1.0 / ATT_HD) + EPS)
        return x * _split_dot(inv, he_ref[...]) * g

    qo_ref[0] = (head_norm(q_ref[0], qg_ref[...]) * (ATT_HD ** -0.5 * LOG2E)).astype(qo_ref.dtype)
    kn = head_norm(k_ref[0], kg_ref[...])
    ko_ref[0] = kn
    kt32_ref[0] = kn.T
    vt = v_ref[0].T
    vt32_ref[0] = vt
    vt_ref[0] = vt.astype(BF16)
    z = fl_ref[0] + bf_ref[...]
    logf = jnp.minimum(z, 0.0) - jnp.log1p(jnp.exp(-jnp.abs(z)))
    lf_ref[0] = logf

    @pl.when(t == 0)
    def _():
        carry_ref[...] = jnp.zeros_like(carry_ref)

    ri = lax.broadcasted_iota(jnp.int32, (tl, tl), 0)
    ci = lax.broadcasted_iota(jnp.int32, (tl, tl), 1)
    c = _dot_hi((ci <= ri).astype(F32), logf) + carry_ref[0:1, :]
    carry_ref[...] = jnp.broadcast_to(c[tl - 1:tl, :], carry_ref.shape)
    rest = c * (-LOG2E)
    extra = jnp.zeros((tl, D), F32)
    for j in range(N_PIECES):
        piece = rest.astype(BF16)
        rest = rest - piece.astype(F32)
        extra = extra + _dot(piece, pl_ref[j])
    knb = kn.astype(BF16)
    extra = extra.astype(BF16)
    kx_ref[0] = jnp.concatenate(
        [x[:, p * LANES:(p + 1) * LANES] for p in range(D // LANES) for x in (knb, extra)], axis=1)


def fox_prep(q, k, v, fl, b_f, q_g, k_g, head_sum, head_expand, place, *, tl=256):
    B, L, D = q.shape
    tl = min(tl, L)
    assert L % tl == 0
    row = pl.BlockSpec((1, tl, D), lambda b, t: (b, t, 0))
    colt = pl.BlockSpec((1, D, tl), lambda b, t: (b, 0, t))
    nar = pl.BlockSpec((1, tl, LANES), lambda b, t: (b, t, 0))
    vecd = pl.BlockSpec((1, D), lambda b, t: (0, 0))
    return pl.pallas_call(
        _fox_prep_kernel,
        grid=(B, L // tl),
        in_specs=[row, row, row, nar, pl.BlockSpec((1, LANES), lambda b, t: (0, 0)), vecd, vecd,
                  pl.BlockSpec((D, LANES), lambda b, t: (0, 0)),
                  pl.BlockSpec((LANES, D), lambda b, t: (0, 0)),
                  pl.BlockSpec((N_PIECES, LANES, D), lambda b, t: (0, 0, 0))],
        out_specs=(row, row, nar, pl.BlockSpec((1, tl, 2 * D), lambda b, t: (b, t, 0)),
                   colt, colt, colt),
        out_shape=(jax.ShapeDtypeStruct((B, L, D), BF16),
                   jax.ShapeDtypeStruct((B, L, D), F32),
                   jax.ShapeDtypeStruct((B, L, LANES), F32),
                   jax.ShapeDtypeStruct((B, L, 2 * D), BF16),
                   jax.ShapeDtypeStruct((B, D, L), BF16),
                   jax.ShapeDtypeStruct((B, D, L), F32),
                   jax.ShapeDtypeStruct((B, D, L), F32)),
        scratch_shapes=[pltpu.VMEM((SUBLANES, LANES), F32)],
        compiler_params=_cparams(("parallel", "arbitrary")),
        name="fox_prep",
    )(q, k, v, fl, b_f, q_g, k_g, head_sum, head_expand, place)


def _fox_attn_kernel(q_ref, kx_ref, vt_ref, g_ref, o_ref, qx_ref, m_ref, l_ref, acc_ref, s_ref):
    qi = pl.program_id(2)
    tq = q_ref.shape[1]
    q = q_ref[0].astype(F32)
    lane = lax.broadcasted_iota(jnp.int32, (tq, LANES), 1)
    for k in range(PAIR):
        mine = (lane >= k * ATT_HD) & (lane < (k + 1) * ATT_HD)
        pick = (lane >= k * N_PIECES) & (lane < (k + 1) * N_PIECES)
        qx_ref[k] = jnp.concatenate([jnp.where(mine, q, 0.0), jnp.where(pick, 1.0, 0.0)], axis=1).astype(BF16)
    m_ref[...] = jnp.full_like(m_ref, NEG)
    l_ref[...] = jnp.zeros_like(l_ref)
    acc_ref[...] = jnp.zeros_like(acc_ref)

    def scores(j):
        r0 = pl.multiple_of(j * tq, tq)
        kb = kx_ref[0, pl.ds(r0, tq), :]
        return [_dot_nt(kb, qx_ref[k]) for k in range(PAIR)]

    def consume(j, diagonal):
        r0 = pl.multiple_of(j * tq, tq)
        vt = vt_ref[0, :, pl.ds(r0, tq)]
        for k in range(PAIR):
            s = s_ref[k]
            if diagonal:
                rows = lax.broadcasted_iota(jnp.int32, (tq, tq), 0)
                cols = lax.broadcasted_iota(jnp.int32, (tq, tq), 1)
                s = jnp.where(rows <= cols, s, NEG)
            m_old = m_ref[k]
            m_new = jnp.maximum(m_old, jnp.max(s, axis=0, keepdims=True))
            alpha = jnp.exp2(m_old - m_new)
            p = jnp.exp2(s - m_new)
            l_ref[k] = l_ref[k] * alpha + jnp.sum(p, axis=0, keepdims=True)
            m_ref[k] = m_new
            acc_ref[k] = acc_ref[k] * alpha + _dot(vt[k * ATT_HD:(k + 1) * ATT_HD, :], p.astype(BF16))

    def stash(s_list):
        for k in range(PAIR):
            s_ref[k] = s_list[k]

    stash(scores(0))

    def body(j, carry):
        nxt = scores(j + 1)
        consume(j, False)
        stash(nxt)
        return carry

    lax.fori_loop(0, qi, body, 0)
    consume(qi, True)
    o = jnp.concatenate([acc_ref[k] / l_ref[k] for k in range(PAIR)], axis=0)
    o_ref[0] = (o.T * _sigmoid(g_ref[0])).astype(o_ref.dtype)


def fox_attn(q, kx, vt, g, *, tq=512):
    B, L, D = q.shape
    tq = min(tq, L)
    assert L % tq == 0
    qs = pl.BlockSpec((1, tq, LANES), lambda b, p, i: (b, i, p))
    return pl.pallas_call(
        _fox_attn_kernel,
        grid=(B, D // LANES, L // tq),
        in_specs=[qs,
                  pl.BlockSpec((1, L, 2 * LANES), lambda b, p, i: (b, 0, p)),
                  pl.BlockSpec((1, LANES, L), lambda b, p, i: (b, p, 0)),
                  qs],
        out_specs=qs,
        out_shape=jax.ShapeDtypeStruct((B, L, D), BF16),
        scratch_shapes=[pltpu.VMEM((PAIR, tq, 2 * LANES), BF16), pltpu.VMEM((PAIR, 1, tq), F32),
                        pltpu.VMEM((PAIR, 1, tq), F32), pltpu.VMEM((PAIR, ATT_HD, tq), F32),
                        pltpu.VMEM((PAIR, tq, tq), F32)],
        compiler_params=_cparams(("parallel", "parallel", "arbitrary")),
        name="fox_attn",
    )(q, kx, vt, g)


def _lanes_to_sublanes(row):
    n = row.shape[1]
    eye = lax.broadcasted_iota(jnp.int32, (n, n), 0) == lax.broadcasted_iota(jnp.int32, (n, n), 1)
    return jnp.sum(jnp.where(eye, jnp.broadcast_to(row, (n, n)), 0.0), axis=1, keepdims=True)


def _sublanes_to_lanes(col):
    n = col.shape[0]
    eye = lax.broadcasted_iota(jnp.int32, (n, n), 0) == lax.broadcasted_iota(jnp.int32, (n, n), 1)
    return jnp.sum(jnp.where(eye, jnp.broadcast_to(col, (n, n)), 0.0), axis=0, keepdims=True)


def _fox_decode_kernel(pt_ref, q_ref, kn_ref, vn_ref, lfn_ref, g_ref, *rest, npg):
    k_refs = rest[:npg]
    v_refs = rest[npg:2 * npg]
    lf_refs = rest[2 * npg:3 * npg]
    o_ref = rest[3 * npg]
    qb_ref, m_ref, l_ref, acc_ref, coff_ref = rest[3 * npg + 1:]
    s_id = pl.program_id(1)
    H, hd, ps = k_refs[0].shape[1:]

    @pl.when(s_id == 0)
    def _():
        q = q_ref[0]
        for h in range(H):
            qb_ref[h] = jnp.broadcast_to(_lanes_to_sublanes(q[h:h + 1, :]), (hd, ps))
        m_ref[...] = jnp.full_like(m_ref, NEG)
        l_ref[...] = jnp.zeros_like(l_ref)
        acc_ref[...] = jnp.zeros_like(acc_ref)
        coff_ref[...] = jnp.zeros_like(coff_ref)

    ri = lax.broadcasted_iota(jnp.int32, (ps, ps), 0)
    ci = lax.broadcasted_iota(jnp.int32, (ps, ps), 1)
    upper = (ri <= ci).astype(F32)

    for i in range(npg):
        qk = jnp.concatenate([jnp.sum(k_refs[i][0, h] * qb_ref[h], axis=0, keepdims=True) for h in range(H)], axis=0)
        c = _dot_hi(lf_refs[i][0], upper) + coff_ref[...]
        coff_ref[...] = c[:, ps - 1:ps]
        s = qk - c * LOG2E
        m_old = m_ref[...]
        m_new = jnp.maximum(m_old, jnp.max(s, axis=1, keepdims=True))
        alpha = jnp.exp2(m_old - m_new)
        p = jnp.exp2(s - m_new)
        l_ref[...] = l_ref[...] * alpha + jnp.sum(p, axis=1, keepdims=True)
        m_ref[...] = m_new
        for h in range(H):
            acc_ref[h] = acc_ref[h] * alpha[h:h + 1, :] + p[h:h + 1, :] * v_refs[i][0, h]

    @pl.when(s_id == pl.num_programs(1) - 1)
    def _():
        s = (jnp.sum(q_ref[0] * kn_ref[0], axis=1, keepdims=True)
             - (coff_ref[...] + _lanes_to_sublanes(lfn_ref[0][:, 0:H])) * LOG2E)
        m_old = m_ref[...]
        m_new = jnp.maximum(m_old, s)
        alpha = jnp.exp2(m_old - m_new)
        p = jnp.exp2(s - m_new)
        l = l_ref[...] * alpha + p
        past = jnp.concatenate(
            [_sublanes_to_lanes(jnp.sum(acc_ref[h], axis=1, keepdims=True)) for h in range(H)], axis=0)
        o = (past * alpha + p * vn_ref[0]) / l
        o_ref[0] = (o * _sigmoid(g_ref[0])).astype(o_ref.dtype)


def fox_decode(q, k_new, v_new, lf_new, g, cache_kt, cache_vt, cache_lft, page_table, *, npg=8):
    B, H, hd = q.shape
    n_pages = page_table.shape[1]
    ps = cache_kt.shape[3]
    npg = min(npg, n_pages)
    assert n_pages % npg == 0
    steps = n_pages // npg
    row = pl.BlockSpec((1, H, hd), lambda b, s, pt: (b, 0, 0))

    def page(i, *tail):
        return pl.BlockSpec((1,) + tail, lambda b, s, pt: (pt[b * n_pages + s * npg + i],) + (0,) * len(tail))

    in_specs = ([row, row, row, pl.BlockSpec((1, 1, LANES), lambda b, s, pt: (b, 0, 0)), row]
                + [page(i, H, hd, ps) for i in range(npg)] + [page(i, H, hd, ps) for i in range(npg)]
                + [page(i, H, ps) for i in range(npg)])
    grid_spec = pltpu.PrefetchScalarGridSpec(
        num_scalar_prefetch=1, grid=(B, steps), in_specs=in_specs, out_specs=row,
        scratch_shapes=[pltpu.VMEM((H, hd, ps), F32), pltpu.VMEM((H, 1), F32), pltpu.VMEM((H, 1), F32),
                        pltpu.VMEM((H, hd, ps), F32), pltpu.VMEM((H, 1), F32)])
    return pl.pallas_call(
        functools.partial(_fox_decode_kernel, npg=npg),
        grid_spec=grid_spec,
        out_shape=jax.ShapeDtypeStruct((B, H, hd), BF16),
        compiler_params=_cparams(("parallel", "arbitrary")),
        name="fox_decode",
    )(page_table.reshape(-1), q, k_new, v_new, lf_new, g,
      *([cache_kt] * npg), *([cache_vt] * npg), *([cache_lft] * npg))


def _block_diag_chunks(w, per_chunk):
    nblk, bw, _ = w.shape
    w = w.reshape(nblk // per_chunk, per_chunk, bw, bw)
    eye = jnp.eye(per_chunk, dtype=w.dtype)
    out = jnp.einsum('cpij,pq->cpiqj', w, eye)
    return out.reshape(nblk // per_chunk, per_chunk * bw, per_chunk * bw)


def _head_expand(n_heads, width):
    r = jnp.arange(LANES)[:, None]
    c = jnp.arange(n_heads * width)[None, :] // width
    return (r == c).astype(F32)


def _pad_lanes(v):
    return jnp.pad(v.reshape(1, -1), ((0, 0), (0, LANES - v.size)))


MOE_ROUTED_MIN_TOKENS = 2048


def _moe_route(i, prm, n_tokens):
    if n_tokens < MOE_ROUTED_MIN_TOKENS:
        return None
    return prm['norm_ffn'][i], prm['moe_w_router'][i], prm['moe_b_router'][i]


def _moe_and_ple(h, p_i, i, prm, last):
    moe_args = (prm['norm_ffn'][i], prm['moe_w_router'][i], prm['moe_b_router'][i],
                prm['moe_w_gate'][i], prm['moe_w_up'][i], prm['moe_w_down'][i])
    if isinstance(h, tuple):
        h = moe_routed(h[0], h[1][:, 0], *moe_args)
    else:
        h = moe_dense(h, *moe_args)
    return ple(h, p_i, prm['norm_ple'][i], prm['ple_w_gate'][i], prm['ple_w_proj'][i],
               prm['norm_final'] if last else None)


def _lru_layer_prompt(h, B, L, j, g, prm, route=None):
    D = h.shape[1]
    xy = norm_matmul(h, g, prm['lru_w_in'][j])
    gated, buf, hl = lru_seq(xy.reshape(B, L, -1), prm['lru_conv_w'][j], prm['lru_conv_b'][j],
                             prm['lru_wa_bd'][j], prm['lru_b_a'][j], prm['lru_wi_bd'][j], prm['lru_b_i'][j],
                             prm['lru_lambda'][j])
    h = matmul_res(gated.reshape(B * L, -1), prm['lru_w_out'][j], h, route)
    return h, buf, hl.reshape(B, D)


def _lru_layer_sample(h, buf, h0, j, g, prm):
    xy = norm_matmul(h, g, prm['lru_w_in'][j])
    gated, nbuf, hl = lru_step(xy, jnp.swapaxes(buf, 0, 1), h0, prm['lru_conv_w'][j], prm['lru_conv_b'][j],
                               prm['lru_wa_bd'][j], prm['lru_b_a'][j], prm['lru_wi_bd'][j], prm['lru_b_i'][j],
                               prm['lru_lambda'][j])
    h = matmul_res(gated, prm['lru_w_out'][j], h)
    return h, jnp.swapaxes(nbuf, 0, 1), hl


def _ssd_proj(h, g, j, prm):
    z = norm_matmul(h, g, prm['ssm_w_z'][j])
    xbc = norm_matmul(h, g, prm['ssm_w_xbc'][j])
    dt = norm_matmul_hi(h, g, prm['ssm_w_dt'][j])
    return z, xbc, dt


def _ssd_args(j, prm):
    return (prm['ssm_conv_w'][j], prm['ssm_conv_b'][j], prm['ssm_dt_bias'][j], prm['ssm_a_log'][j],
            prm['ssm_d_lanes'][j], prm['ssm_norm'][j], prm['ssm_expand'])


def _ssd_layer_prompt(h, B, L, j, g, prm, route=None):
    z, xbc, dt = _ssd_proj(h, g, j, prm)
    y, buf, st = ssd_seq(z.reshape(B, L, -1), xbc.reshape(B, L, -1), dt.reshape(B, L, -1), *_ssd_args(j, prm))
    return matmul_res(y.reshape(B * L, -1), prm['ssm_w_out'][j], h, route), buf, st


def _ssd_layer_sample(h, buf, h0, j, g, prm):
    B = h.shape[0]
    z, xbc, dt = _ssd_proj(h, g, j, prm)
    y, nbuf, st = ssd_step(z.reshape(B, 1, -1), xbc.reshape(B, 1, -1), dt.reshape(B, 1, -1), buf, h0,
                           *_ssd_args(j, prm))
    return matmul_res(y.reshape(B, -1), prm['ssm_w_out'][j], h), nbuf, st


def _fox_proj(h, g, j, prm):
    q = norm_matmul(h, g, prm['fox_w_q'][j])
    k = norm_matmul(h, g, prm['fox_w_k'][j])
    v = norm_matmul(h, g, prm['fox_w_v'][j])
    og = norm_matmul(h, g, prm['fox_w_g'][j])
    fl = norm_matmul_hi(h, g, prm['fox_w_f'][j], exact=False)
    return q, k, v, og, fl


def _fox_prep_args(j, prm):
    return (prm['fox_b_f'][j], prm['fox_q_norm'][j], prm['fox_k_norm'][j], prm['fox_head_sum'],
            prm['fox_head_expand'], prm['fox_place'])


def _fox_layer_prompt(h, B, L, j, g, prm, route=None, tq=512):
    q, k, v, og, fl = _fox_proj(h, g, j, prm)
    shp = (B, L, -1)
    qs, _, logf, kx, vt, kt32, vt32 = fox_prep(q.reshape(shp), k.reshape(shp), v.reshape(shp), fl.reshape(shp),
                                               *_fox_prep_args(j, prm))
    o = fox_attn(qs, kx, vt, og.reshape(shp), tq=tq)

    def per_head(xt):
        return jnp.transpose(xt.reshape(B, -1, ATT_HD, L), (0, 3, 1, 2))

    return matmul_res(o.reshape(B * L, -1), prm['fox_w_out'][j], h, route), per_head(kt32), per_head(vt32), logf


def _fox_layer_sample(h, cache, n_phys, page_table, j, g, prm):
    B, D = h.shape
    q, k, v, og, fl = _fox_proj(h, g, j, prm)
    one = (1, B, -1)
    qs, kn, logf = fox_prep(q.reshape(one), k.reshape(one), v.reshape(one), fl.reshape(one),
                            *_fox_prep_args(j, prm))[:3]
    ck, cv, clf = cache
    heads = (B, D // ATT_HD, ATT_HD)
    tok = (B, 1, -1)
    o = fox_decode(qs.astype(F32).reshape(heads), kn.reshape(heads), v.reshape(heads), logf.reshape(tok),
                   og.reshape(heads), ck, cv, clf, page_table + j * n_phys)
    return matmul_res(o.reshape(B, D), prm['fox_w_out'][j], h), kn.reshape(tok), v.reshape(tok), logf.reshape(tok)


def _prepare_params(raw):
    prm = dict(raw)
    D = raw['norm_final'].shape[0]
    for name in ('lru_w_in', 'lru_w_out', 'ssm_w_out', 'fox_w_out', 'moe_w_gate', 'moe_w_up', 'moe_w_down',
                 'ple_w_proj', 'ple_w_gate'):
        prm[name] = raw[name].astype(BF16)
    per = 2 * LANES // (D // LRU_BLOCKS)
    prm['lru_wa_bd'] = jax.vmap(lambda w: _block_diag_chunks(w, per))(raw['lru_w_a']).astype(BF16)
    prm['lru_wi_bd'] = jax.vmap(lambda w: _block_diag_chunks(w, per))(raw['lru_w_i']).astype(BF16)
    prm['lru_b_a'] = raw['lru_b_a'].reshape(raw['lru_b_a'].shape[0], -1)
    prm['lru_b_i'] = raw['lru_b_i'].reshape(raw['lru_b_i'].shape[0], -1)
    n_h = raw['ssm_a_log'].shape[1]
    di = n_h * SSM_P
    w = raw['ssm_w_in']
    conv_dim = raw['ssm_conv_w'].shape[2]
    prm['ssm_w_z'] = w[:, :, :di].astype(BF16)
    prm['ssm_w_xbc'] = w[:, :, di:di + conv_dim].astype(BF16)
    prm['ssm_w_dt'] = jax.vmap(lambda m: _pad_cols(m, LANES))(w[:, :, di + conv_dim:])
    prm['ssm_dt_bias'] = jax.vmap(_pad_lanes)(raw['ssm_dt_bias'])
    prm['ssm_a_log'] = jax.vmap(_pad_lanes)(raw['ssm_a_log'])
    prm['ssm_d_lanes'] = jnp.repeat(raw['ssm_d'], SSM_P, axis=1)[:, None, :]
    prm['ssm_expand'] = _head_expand(n_h, SSM_P).astype(BF16)
    w = raw['fox_w_in']
    for n, name in enumerate(('fox_w_q', 'fox_w_k', 'fox_w_v', 'fox_w_g')):
        prm[name] = w[:, :, n * D:(n + 1) * D].astype(BF16)
    prm['fox_w_f'] = jax.vmap(lambda m: _pad_cols(m, LANES))(w[:, :, 4 * D:])
    prm['fox_b_f'] = jax.vmap(_pad_lanes)(raw['fox_b_f'])
    n_ah = D // ATT_HD
    prm['fox_q_norm'] = jnp.tile(raw['fox_q_norm'], (1, n_ah))[:, None, :]
    prm['fox_k_norm'] = jnp.tile(raw['fox_k_norm'], (1, n_ah))[:, None, :]
    prm['fox_head_expand'] = _head_expand(n_ah, ATT_HD)
    prm['fox_head_sum'] = prm['fox_head_expand'].T.astype(BF16)
    hh = jnp.arange(LANES)[None, :, None]
    jj = jnp.arange(N_PIECES)[:, None, None]
    col = jnp.arange(D)[None, None, :]
    prm['fox_place'] = ((hh < n_ah) & (col == LANES * (hh // PAIR) + N_PIECES * (hh % PAIR) + jj)).astype(BF16)
    prm['moe_w_router'] = jax.vmap(lambda we, wg: _pad_cols(jnp.concatenate([we, wg], axis=1), LANES))(
        raw['moe_w_expert'], raw['moe_w_group'])
    prm['moe_b_router'] = jax.vmap(lambda be, bg: _pad_lanes(jnp.concatenate([be, bg])))(
        raw['moe_b_expert'], raw['moe_b_group'])
    return prm


def kernel(x_prompt, x_sample, state_lru_h, state_lru_conv, state_ssm_h, state_ssm_conv, cache_k, cache_v, cache_logf, page_table, p_prompt, p_sample, lru_w_in, lru_conv_w, lru_conv_b, lru_w_a, lru_b_a, lru_w_i, lru_b_i, lru_lambda, lru_w_out, ssm_w_in, ssm_conv_w, ssm_conv_b, ssm_dt_bias, ssm_a_log, ssm_d, ssm_norm, ssm_w_out, fox_w_in, fox_b_f, fox_q_norm, fox_k_norm, fox_w_out, moe_w_group, moe_b_group, moe_w_expert, moe_b_expert, moe_w_gate, moe_w_up, moe_w_down, ple_w_proj, ple_w_gate, norm_mix, norm_ffn, norm_ple, norm_final):
    prm = _prepare_params(dict(
        lru_w_in=lru_w_in, lru_conv_w=lru_conv_w, lru_conv_b=lru_conv_b, lru_w_a=lru_w_a, lru_b_a=lru_b_a,
        lru_w_i=lru_w_i, lru_b_i=lru_b_i, lru_lambda=lru_lambda, lru_w_out=lru_w_out, ssm_w_in=ssm_w_in,
        ssm_conv_w=ssm_conv_w, ssm_conv_b=ssm_conv_b, ssm_dt_bias=ssm_dt_bias, ssm_a_log=ssm_a_log, ssm_d=ssm_d,
        ssm_norm=ssm_norm, ssm_w_out=ssm_w_out, fox_w_in=fox_w_in, fox_b_f=fox_b_f, fox_q_norm=fox_q_norm,
        fox_k_norm=fox_k_norm, fox_w_out=fox_w_out, moe_w_group=moe_w_group, moe_b_group=moe_b_group,
        moe_w_expert=moe_w_expert, moe_b_expert=moe_b_expert, moe_w_gate=moe_w_gate, moe_w_up=moe_w_up,
        moe_w_down=moe_w_down, ple_w_proj=ple_w_proj, ple_w_gate=ple_w_gate, norm_mix=norm_mix,
        norm_ffn=norm_ffn, norm_ple=norm_ple, norm_final=norm_final))
    depth = norm_mix.shape[0]
    B, L, D = x_prompt.shape
    Bs = x_sample.shape[0]
    n_mix = 3
    att_h = D // ATT_HD
    n_phys, page = cache_k.shape[1], cache_k.shape[2]
    cache = (jnp.transpose(cache_k, (0, 1, 3, 4, 2)).reshape(-1, att_h, ATT_HD, page),
             jnp.transpose(cache_v, (0, 1, 3, 4, 2)).reshape(-1, att_h, ATT_HD, page),
             jnp.transpose(cache_logf, (0, 1, 3, 2)).reshape(-1, att_h, page))

    hp = x_prompt.reshape(B * L, D)
    hs = x_sample.reshape(Bs, D)
    outs = {k: [] for k in ('lru_h_p', 'lru_h_s', 'lru_c_p', 'lru_c_s', 'ssm_h_p', 'ssm_h_s', 'ssm_c_p', 'ssm_c_s',
                            'k_p', 'k_s', 'v_p', 'v_s', 'lf_p', 'lf_s')}
    yp = ys = None
    for i in range(depth):
        j = i // n_mix
        g = prm['norm_mix'][i]
        if i % n_mix == 0:
            hp, buf, hl = _lru_layer_prompt(hp, B, L, j, g, prm, _moe_route(i, prm, B * L))
            outs['lru_c_p'].append(buf)
            outs['lru_h_p'].append(hl)
            hs, buf, hl = _lru_layer_sample(hs, state_lru_conv[j], state_lru_h[j], j, g, prm)
            outs['lru_c_s'].append(buf)
            outs['lru_h_s'].append(hl)
        elif i % n_mix == 1:
            hp, buf, st = _ssd_layer_prompt(hp, B, L, j, g, prm, _moe_route(i, prm, B * L))
            outs['ssm_c_p'].append(buf)
            outs['ssm_h_p'].append(st)
            hs, buf, st = _ssd_layer_sample(hs, state_ssm_conv[j], state_ssm_h[j], j, g, prm)
            outs['ssm_c_s'].append(buf)
            outs['ssm_h_s'].append(st)
        else:
            hp, k, v, lf = _fox_layer_prompt(hp, B, L, j, g, prm, _moe_route(i, prm, B * L))
            outs['k_p'].append(k)
            outs['v_p'].append(v)
            outs['lf_p'].append(lf[:, :, :att_h])
            hs, k, v, lf = _fox_layer_sample(hs, cache, n_phys, page_table, j, g, prm)
            outs['k_s'].append(k.reshape(Bs, 1, att_h, ATT_HD))
            outs['v_s'].append(v.reshape(Bs, 1, att_h, ATT_HD))
            outs['lf_s'].append(lf[:, :, :att_h])
        last = i == depth - 1
        hp = _moe_and_ple(hp, p_prompt[i].reshape(B * L, -1), i, prm, last)
        hs = _moe_and_ple(hs, p_sample[i].reshape(Bs, -1), i, prm, last)
        if last:
            hp, yp = hp
            hs, ys = hs
    st = {k: jnp.stack(v) for k, v in outs.items()}
    return (yp.reshape(B, L, D), ys.reshape(Bs, 1, D), st['lru_h_p'], st['lru_h_s'], st['lru_c_p'], st['lru_c_s'],
            st['ssm_h_p'], st['ssm_h_s'], st['ssm_c_p'], st['ssm_c_s'], st['k_p'], st['k_s'], st['v_p'], st['v_s'],
            st['lf_p'], st['lf_s'])
```

```python
import functools

import jax
import jax.numpy as jnp
from jax import lax
from jax.experimental import pallas as pl
from jax.experimental.pallas import tpu as pltpu

F32 = jnp.float32
BF16 = jnp.bfloat16
HI = lax.Precision.HIGHEST

EPS = 1e-6
CONV_W = 4
LANES = 128
SUBLANES = 8
LRU_C = 8.0
LRU_BLOCKS = 16
SSM_P = 64
SSM_G = 8
SSM_N = 128
ATT_HD = 64
MOE_GROUPS = 4
MOE_EPG = 4
MOE_E = MOE_GROUPS * MOE_EPG
NEG = -1e30
MIB = 1024 * 1024


def _cparams(sem, vmem_mib=48):
    return pltpu.CompilerParams(dimension_semantics=sem, vmem_limit_bytes=vmem_mib * MIB)


def _rms(x, g):
    ms = jnp.mean(x * x, axis=-1, keepdims=True)
    return x * lax.rsqrt(ms + EPS) * g


def _softplus(z):
    return jnp.maximum(z, 0.0) + jnp.log1p(jnp.exp(-jnp.abs(z)))


def _sigmoid(z):
    return 1.0 / (1.0 + jnp.exp(-z))


def _silu(z):
    return z * _sigmoid(z)


def _gelu_tanh(z):
    c = 0.7978845608028654
    return 0.5 * z * (1.0 + jnp.tanh(c * (z + 0.044715 * (z * z * z))))


def _dot(a, b):
    return jnp.dot(a, b, preferred_element_type=F32)


def _dot_hi(a, b):
    return jnp.dot(a, b, preferred_element_type=F32, precision=HI)


def _dot_nt(a, b):
    return lax.dot_general(a, b, (((1,), (1,)), ((), ())), preferred_element_type=F32)


def _dot_nt_hi(a, b):
    return lax.dot_general(a, b, (((1,), (1,)), ((), ())), preferred_element_type=F32, precision=HI)


def _place_dot(x, m, pieces=3):
    out = None
    rest = x
    for _ in range(pieces):
        piece = rest.astype(BF16)
        rest = rest - piece.astype(F32)
        out = _dot(piece, m) if out is None else out + _dot(piece, m)
    return out


def _pad_cols(w, n):
    return jnp.pad(w, ((0, 0), (0, n - w.shape[1])))


def _norm_matmul_kernel(x_ref, g_ref, w_ref, o_ref, xn_ref):
    @pl.when(pl.program_id(1) == 0)
    def _():
        xn_ref[...] = _rms(x_ref[...], g_ref[...]).astype(xn_ref.dtype)

    o_ref[...] = _dot(xn_ref[...], w_ref[...]).astype(o_ref.dtype)


def _norm_matmul_hi_kernel(x_ref, g_ref, w_ref, o_ref, *, exact):
    xn = _rms(x_ref[...], g_ref[...])
    o_ref[...] = _dot_hi(xn, w_ref[...]) if exact else _dot_x3(xn, w_ref[...])


def norm_matmul(x, g, w, *, tm=1024, tn=1024, out_dtype=F32):
    T, D = x.shape
    N = w.shape[1]
    tm, tn = min(tm, T), min(tn, N)
    assert T % tm == 0 and N % tn == 0
    return pl.pallas_call(
        _norm_matmul_kernel,
        grid=(T // tm, N // tn),
        in_specs=[pl.BlockSpec((tm, D), lambda i, j: (i, 0)),
                  pl.BlockSpec((1, D), lambda i, j: (0, 0)),
                  pl.BlockSpec((D, tn), lambda i, j: (0, j))],
        out_specs=pl.BlockSpec((tm, tn), lambda i, j: (i, j)),
        out_shape=jax.ShapeDtypeStruct((T, N), out_dtype),
        scratch_shapes=[pltpu.VMEM((tm, D), BF16)],
        compiler_params=_cparams(("parallel", "arbitrary")),
        name="norm_matmul",
    )(x, g.reshape(1, D), w)


def norm_matmul_hi(x, g, w, *, exact=True, tm=512):
    T, D = x.shape
    N = w.shape[1]
    tm = min(tm, T)
    assert T % tm == 0
    return pl.pallas_call(
        functools.partial(_norm_matmul_hi_kernel, exact=exact),
        grid=(T // tm,),
        in_specs=[pl.BlockSpec((tm, D), lambda i: (i, 0)),
                  pl.BlockSpec((1, D), lambda i: (0, 0)),
                  pl.BlockSpec((D, N), lambda i: (0, 0))],
        out_specs=pl.BlockSpec((tm, N), lambda i: (i, 0)),
        out_shape=jax.ShapeDtypeStruct((T, N), F32),
        compiler_params=_cparams(("parallel",)),
        name="norm_matmul_hi",
    )(x, g.reshape(1, D), w)


def _matmul_res_kernel(a_ref, w_ref, r_ref, *rest):
    out = r_ref[...] + _dot(a_ref[...], w_ref[...])
    if len(rest) == 1:
        rest[0][...] = out
    else:
        g_ref, wr_ref, br_ref, o_ref, cls_ref = rest
        o_ref[...] = out
        cls_ref[...] = jnp.broadcast_to(_moe_class(out, g_ref, wr_ref, br_ref), cls_ref.shape)


def matmul_res(a, w, res, route=None, *, tm=512):
    T, K = a.shape
    N = w.shape[1]
    tm = min(tm, T)
    assert T % tm == 0
    row = pl.BlockSpec((tm, N), lambda i: (i, 0))
    in_specs = [pl.BlockSpec((tm, K), lambda i: (i, 0)), pl.BlockSpec((K, N), lambda i: (0, 0)), row]
    args = (a, w, res)
    out_specs, out_shape = row, jax.ShapeDtypeStruct((T, N), F32)
    if route is not None:
        g, wr, br = route
        in_specs += [pl.BlockSpec((1, N), lambda i: (0, 0)), pl.BlockSpec((N, LANES), lambda i: (0, 0)),
                     pl.BlockSpec((1, LANES), lambda i: (0, 0))]
        args += (g.reshape(1, N), wr, br)
        out_specs = (row, pl.BlockSpec((tm, LANES), lambda i: (i, 0)))
        out_shape = (out_shape, jax.ShapeDtypeStruct((T, LANES), jnp.int32))
    return pl.pallas_call(
        _matmul_res_kernel,
        grid=(T // tm,),
        in_specs=in_specs,
        out_specs=out_specs,
        out_shape=out_shape,
        compiler_params=_cparams(("parallel",)),
        name="matmul_res",
    )(*args)


def _ple_kernel(h_ref, p_ref, g_ref, wg_ref, wp_ref, gf_ref, o_ref, *maybe_final):
    h = h_ref[...]
    xn = _rms(h, g_ref[...]).astype(BF16)
    gate = _sigmoid(_dot(xn, wg_ref[...]))
    out = h + gate * _dot(p_ref[...].astype(BF16), wp_ref[...])
    o_ref[...] = out
    if maybe_final:
        maybe_final[0][...] = _rms(out, gf_ref[...])


def ple(h, p, g, w_gate, w_proj, g_final=None, *, tm=512):
    T, D = h.shape
    P = p.shape[1]
    tm = min(tm, T)
    assert T % tm == 0
    final = g_final is not None
    gf = (g_final if final else g).reshape(1, D)
    row = pl.BlockSpec((tm, D), lambda i: (i, 0))
    vec = pl.BlockSpec((1, D), lambda i: (0, 0))
    out_shape = jax.ShapeDtypeStruct((T, D), F32)
    return pl.pallas_call(
        _ple_kernel,
        grid=(T // tm,),
        in_specs=[row, pl.BlockSpec((tm, P), lambda i: (i, 0)), vec,
                  pl.BlockSpec((D, D), lambda i: (0, 0)),
                  pl.BlockSpec((P, D), lambda i: (0, 0)), vec],
        out_specs=(row, row) if final else row,
        out_shape=(out_shape, out_shape) if final else out_shape,
        compiler_params=_cparams(("parallel",)),
        name="ple",
    )(h, p, g.reshape(1, D), w_gate, w_proj, gf)


def _route(logits):
    lane = lax.broadcasted_iota(jnp.int32, logits.shape, 1)
    big = jnp.int32(1 << 20)
    is_g = (lane >= MOE_E) & (lane < MOE_E + MOE_GROUPS)
    glog = jnp.where(is_g, logits, NEG)
    gmax = jnp.max(glog, axis=-1, keepdims=True)
    gsel = jnp.min(jnp.where(is_g & (glog == gmax), lane, big), axis=-1, keepdims=True) - MOE_E
    gden = jnp.sum(jnp.where(is_g, jnp.exp(glog - gmax), 0.0), axis=-1, keepdims=True)
    gprob = 1.0 / gden
    in_g = (lane < MOE_E) & ((lane // MOE_EPG) == gsel)
    e1 = jnp.where(in_g, logits, NEG)
    v1 = jnp.max(e1, axis=-1, keepdims=True)
    i1 = jnp.min(jnp.where(in_g & (e1 == v1), lane, big), axis=-1, keepdims=True)
    in_g2 = in_g & (lane != i1)
    e2 = jnp.where(in_g2, logits, NEG)
    v2 = jnp.max(e2, axis=-1, keepdims=True)
    i2 = jnp.min(jnp.where(in_g2 & (e2 == v2), lane, big), axis=-1, keepdims=True)
    t = jnp.exp(v2 - v1)
    w1 = gprob / (1.0 + t)
    w2 = gprob * t / (1.0 + t)
    return jnp.where(lane == i1, w1, jnp.where(lane == i2, w2, 0.0))


def _moe_dense_kernel(h_ref, g_ref, wr_ref, br_ref, wg_ref, wu_ref, wd_ref, o_ref, xn_ref, comb_ref, acc_ref):
    e = pl.program_id(1)

    @pl.when(e == 0)
    def _():
        xn = _rms(h_ref[...], g_ref[...])
        comb_ref[...] = _route(_dot_hi(xn, wr_ref[...]) + br_ref[...])
        xn_ref[...] = xn.astype(BF16)
        acc_ref[...] = jnp.zeros_like(acc_ref)

    xn = xn_ref[...]
    hg = _dot(xn, wg_ref[0])
    hu = _dot(xn, wu_ref[0])
    comb = comb_ref[...]
    lane = lax.broadcasted_iota(jnp.int32, comb.shape, 1)
    c = jnp.sum(jnp.where(lane == e, comb, 0.0), axis=-1, keepdims=True)
    hid = (_silu(hg) * hu * c).astype(BF16)
    acc_ref[...] += _dot(hid, wd_ref[0])

    @pl.when(e == pl.num_programs(1) - 1)
    def _():
        o_ref[...] = h_ref[...] + acc_ref[...]


def moe_dense(h, g, w_router, b_router, w_gate, w_up, w_down, *, tm=1024):
    T, D = h.shape
    E, _, Fd = w_gate.shape
    tm = min(tm, T)
    assert T % tm == 0
    row = pl.BlockSpec((tm, D), lambda i, e: (i, 0))
    return pl.pallas_call(
        _moe_dense_kernel,
        grid=(T // tm, E),
        in_specs=[row, pl.BlockSpec((1, D), lambda i, e: (0, 0)),
                  pl.BlockSpec((D, LANES), lambda i, e: (0, 0)),
                  pl.BlockSpec((1, LANES), lambda i, e: (0, 0)),
                  pl.BlockSpec((1, D, Fd), lambda i, e: (e, 0, 0)),
                  pl.BlockSpec((1, D, Fd), lambda i, e: (e, 0, 0)),
                  pl.BlockSpec((1, Fd, D), lambda i, e: (e, 0, 0))],
        out_specs=row,
        out_shape=jax.ShapeDtypeStruct((T, D), F32),
        scratch_shapes=[pltpu.VMEM((tm, D), BF16), pltpu.VMEM((tm, LANES), F32), pltpu.VMEM((tm, D), F32)],
        compiler_params=_cparams(("parallel", "arbitrary")),
        name="moe_dense",
    )(h, g.reshape(1, D), w_router, b_router, w_gate, w_up, w_down)


MOE_PAIRS = MOE_EPG * (MOE_EPG - 1) // 2
MOE_CLASSES = MOE_GROUPS * MOE_PAIRS


def _route_select(logits):
    lane = lax.broadcasted_iota(jnp.int32, logits.shape, 1)
    big = jnp.int32(1 << 20)
    is_g = (lane >= MOE_E) & (lane < MOE_E + MOE_GROUPS)
    glog = jnp.where(is_g, logits, NEG)
    gmax = jnp.max(glog, axis=-1, keepdims=True)
    gsel = jnp.min(jnp.where(is_g & (glog == gmax), lane, big), axis=-1, keepdims=True) - MOE_E
    in_g = (lane < MOE_E) & ((lane // MOE_EPG) == gsel)
    e1 = jnp.where(in_g, logits, NEG)
    v1 = jnp.max(e1, axis=-1, keepdims=True)
    i1 = jnp.min(jnp.where(in_g & (e1 == v1), lane, big), axis=-1, keepdims=True)
    in_g2 = in_g & (lane != i1)
    e2 = jnp.where(in_g2, logits, NEG)
    v2 = jnp.max(e2, axis=-1, keepdims=True)
    i2 = jnp.min(jnp.where(in_g2 & (e2 == v2), lane, big), axis=-1, keepdims=True)
    return gsel, i1, i2


def _dot_x3(x, w):
    xh, wh = x.astype(BF16), w.astype(BF16)
    xl, wl = (x - xh.astype(F32)).astype(BF16), (w - wh.astype(F32)).astype(BF16)
    return _dot(xh, wh) + _dot(xl, wh) + _dot(xh, wl)


def _moe_class(h, g_ref, wr_ref, br_ref):
    gsel, i1, i2 = _route_select(_dot_x3(_rms(h, g_ref[...]), wr_ref[...]) + br_ref[...])
    a = jnp.minimum(i1, i2) - gsel * MOE_EPG
    b = jnp.maximum(i1, i2) - gsel * MOE_EPG
    return gsel * MOE_PAIRS + (a * (2 * MOE_EPG - 1 - a)) // 2 + (b - a - 1)


def _moe_plan(cls, tr):
    T = cls.shape[0]
    n_tiles = T // tr + MOE_CLASSES
    order = jnp.argsort(cls, stable=True).astype(jnp.int32)
    counts = jnp.sum(cls[:, None] == jnp.arange(MOE_CLASSES, dtype=jnp.int32)[None, :], axis=0, dtype=jnp.int32)
    tiles_per = (counts + tr - 1) // tr
    tile_end = jnp.cumsum(tiles_per)
    tile_ids = jnp.arange(n_tiles, dtype=jnp.int32)
    tcls = jnp.minimum(jnp.sum(tile_ids[:, None] >= tile_end[None, :], axis=1, dtype=jnp.int32), MOE_CLASSES - 1)
    k = tile_ids - (tile_end - tiles_per)[tcls]
    n_rows = jnp.where(tile_ids < tile_end[-1], jnp.clip(counts[tcls] - k * tr, 0, tr), 0).astype(jnp.int32)
    first = (jnp.cumsum(counts) - counts)[tcls] + k * tr
    rows = jnp.arange(tr, dtype=jnp.int32)[None, :]
    src = jnp.where(rows < n_rows[:, None], order[jnp.clip(first[:, None] + rows, 0, T - 1)], 0).astype(jnp.int32)
    pair = tcls % MOE_PAIRS
    grp = tcls // MOE_PAIRS
    pa = jnp.array([a for a in range(MOE_EPG) for b in range(a + 1, MOE_EPG)], jnp.int32)[pair]
    pb = jnp.array([b for a in range(MOE_EPG) for b in range(a + 1, MOE_EPG)], jnp.int32)[pair]
    return grp * MOE_EPG + pa, grp * MOE_EPG + pb, grp, n_rows, src


def _moe_routed_kernel(e1_ref, e2_ref, grp_ref, nrow_ref, prev_ref, src_ref, next_ref, h_ref, g_ref, wr_ref, br_ref,
                       wg1_ref, wu1_ref, wd1_ref, wg2_ref, wu2_ref, wd2_ref, o_ref, x_ref, y_ref, sem_in, sem_out):
    i = pl.program_id(0)
    last = pl.num_programs(0) - 1
    tr = x_ref.shape[1]
    n_rows = nrow_ref[i]
    n_prev = jnp.where(i > 0, nrow_ref[jnp.maximum(i - 1, 0)], 0)

    def row_in(idx_ref, s, r):
        return pltpu.make_async_copy(h_ref.at[pl.ds(idx_ref[0, 0, r], 1)], x_ref.at[s, pl.ds(r, 1)], sem_in.at[s])

    def row_out(idx_ref, s, r):
        return pltpu.make_async_copy(y_ref.at[s, pl.ds(r, 1)], o_ref.at[pl.ds(idx_ref[0, 0, r], 1)], sem_out.at[s])

    def each(n, fn):
        def body8(j, carry):
            for u in range(SUBLANES):
                fn(j * SUBLANES + u)
            return carry

        def body1(r, carry):
            fn(r)
            return carry

        whole = n // SUBLANES
        lax.fori_loop(0, whole, body8, 0)
        if not isinstance(n, int) or n % SUBLANES:
            lax.fori_loop(whole * SUBLANES, n, body1, 0)

    refs = (e1_ref, e2_ref, grp_ref, prev_ref, src_ref, next_ref, g_ref, wr_ref, br_ref,
            wg1_ref, wu1_ref, wd1_ref, wg2_ref, wu2_ref, wd2_ref, x_ref, y_ref)
    for slot in range(2):
        pl.when(i % 2 == slot)(functools.partial(
            _moe_routed_step, slot, i, last, tr, n_rows, n_prev, row_in, row_out, each, refs))


def _moe_routed_step(slot, i, last, tr, n_rows, n_prev, row_in, row_out, each, refs):
    (e1_ref, e2_ref, grp_ref, prev_ref, src_ref, next_ref, g_ref, wr_ref, br_ref,
     wg1_ref, wu1_ref, wd1_ref, wg2_ref, wu2_ref, wd2_ref, x_ref, y_ref) = refs

    @pl.when((i == 0) & (n_rows > 0))
    def _():
        each(tr, lambda r: row_in(src_ref, slot, r).start())

    @pl.when(((i == 0) & (n_rows > 0)) | (n_prev > 0))
    def _():
        each(tr, lambda r: row_in(src_ref, slot, r).wait())

    @pl.when(n_rows > 0)
    def _():
        for r in range(tr):
            row_in(next_ref, 1 - slot, r).start()
        x = x_ref[slot]
        xn = _rms(x, g_ref[...])
        logits = _dot_x3(xn, wr_ref[...]) + br_ref[...]
        lane = lax.broadcasted_iota(jnp.int32, logits.shape, 1)

        def pick(idx):
            return jnp.sum(jnp.where(lane == idx, logits, 0.0), axis=-1, keepdims=True)

        la, lb, lg = pick(e1_ref[i]), pick(e2_ref[i]), pick(MOE_E + grp_ref[i])
        is_g = (lane >= MOE_E) & (lane < MOE_E + MOE_GROUPS)
        gprob = 1.0 / jnp.sum(jnp.where(is_g, jnp.exp(logits - lg), 0.0), axis=-1, keepdims=True)
        top = jnp.maximum(la, lb)
        pa, pb = jnp.exp(la - top), jnp.exp(lb - top)
        scale = gprob / (pa + pb)
        xb = xn.astype(BF16)
        hid1 = (_silu(_dot(xb, wg1_ref[0])) * _dot(xb, wu1_ref[0]) * (pa * scale)).astype(BF16)
        hid2 = (_silu(_dot(xb, wg2_ref[0])) * _dot(xb, wu2_ref[0]) * (pb * scale)).astype(BF16)
        y_ref[slot] = x + _dot(hid1, wd1_ref[0]) + _dot(hid2, wd2_ref[0])

    each(n_prev, lambda r: row_out(prev_ref, 1 - slot, r).wait())
    each(n_rows, lambda r: row_out(src_ref, slot, r).start())

    @pl.when(i == last)
    def _():
        each(n_rows, lambda r: row_out(src_ref, slot, r).wait())


def moe_routed(h, cls, g, w_router, b_router, w_gate, w_up, w_down, *, tr=256):
    T, D = h.shape
    E, _, Fd = w_gate.shape
    assert T % tr == 0
    e1, e2, grp, n_rows, src = _moe_plan(cls, tr)
    n_tiles = src.shape[0]

    def wspec(shape, which):
        return pl.BlockSpec((1,) + shape, lambda i, e1, e2, grp, nr: ((e1, e2)[which][i], 0, 0))

    const = lambda shape: pl.BlockSpec(shape, lambda i, e1, e2, grp, nr: (0,) * len(shape))

    def rows_of(shift):
        return pl.BlockSpec((1, 1, tr), lambda i, e1, e2, grp, nr: (jnp.clip(i + shift, 0, n_tiles - 1), 0, 0),
                            memory_space=pltpu.SMEM)

    grid_spec = pltpu.PrefetchScalarGridSpec(
        num_scalar_prefetch=4, grid=(n_tiles,),
        in_specs=[rows_of(-1), rows_of(0), rows_of(1),
                  pl.BlockSpec(memory_space=pl.ANY), const((1, D)), const((D, LANES)), const((1, LANES)),
                  wspec((D, Fd), 0), wspec((D, Fd), 0), wspec((Fd, D), 0),
                  wspec((D, Fd), 1), wspec((D, Fd), 1), wspec((Fd, D), 1)],
        out_specs=pl.BlockSpec(memory_space=pl.ANY),
        scratch_shapes=[pltpu.VMEM((2, tr, D), F32), pltpu.VMEM((2, tr, D), F32),
                        pltpu.SemaphoreType.DMA((2,)), pltpu.SemaphoreType.DMA((2,))])
    src3 = src.reshape(n_tiles, 1, tr)
    return pl.pallas_call(
        _moe_routed_kernel,
        grid_spec=grid_spec,
        out_shape=jax.ShapeDtypeStruct((T, D), F32),
        compiler_params=_cparams(("arbitrary",)),
        name="moe_routed",
    )(e1, e2, grp, n_rows, src3, src3, src3, h, g.reshape(1, D), w_router, b_router,
      w_gate, w_up, w_down, w_gate, w_up, w_down)


def _conv_tile(xpad_ref, x, cw_ref, cb_ref, first):
    tl = x.shape[0]

    @pl.when(first)
    def _():
        xpad_ref[0:SUBLANES, :] = jnp.zeros((SUBLANES, x.shape[1]), F32)

    xpad_ref[SUBLANES:SUBLANES + tl, :] = x
    out = cb_ref[...] + x * cw_ref[CONV_W - 1:CONV_W, :]
    for k in range(CONV_W - 1):
        lo = SUBLANES - (CONV_W - 1) + k
        out = out + xpad_ref[lo:lo + tl, :] * cw_ref[k:k + 1, :]
    tail = xpad_ref[tl:tl + SUBLANES, :]
    xpad_ref[0:SUBLANES, :] = tail
    return out, tail[SUBLANES - (CONV_W - 1):, :]


def _lru_gates(xc, wa_ref, ba_ref, wi_ref, bi_ref, lam_ref):
    xcb = xc.astype(BF16)
    nb = wa_ref.shape[0]
    cw = wa_ref.shape[1]
    ra = jnp.concatenate([_dot(xcb[:, c * cw:(c + 1) * cw], wa_ref[c]) for c in range(nb)], axis=1)
    ia = jnp.concatenate([_dot(xcb[:, c * cw:(c + 1) * cw], wi_ref[c]) for c in range(nb)], axis=1)
    r = _sigmoid(ra + ba_ref[...])
    ig = _sigmoid(ia + bi_ref[...])
    log_a = (-LRU_C) * r * _softplus(-lam_ref[...])
    a = jnp.exp(log_a)
    b = jnp.sqrt(-jnp.tanh(log_a) * (a * a + 1.0)) * (ig * xc)
    return a, b


def _lru_seq_kernel(xb_ref, yb_ref, cw_ref, cb_ref, wa_ref, ba_ref, wi_ref, bi_ref, lam_ref,
                    o_ref, buf_ref, hl_ref, xpad_ref, a_ref, b_ref, hc_ref):
    t = pl.program_id(1)
    tl, W = xb_ref.shape[1], xb_ref.shape[2]
    xc, tail = _conv_tile(xpad_ref, xb_ref[0], cw_ref, cb_ref, t == 0)
    a, b = _lru_gates(xc, wa_ref, ba_ref, wi_ref, bi_ref, lam_ref)
    a_ref[...] = a
    b_ref[...] = b

    @pl.when(t == 0)
    def _():
        hc_ref[...] = jnp.zeros_like(hc_ref)

    row = lax.broadcasted_iota(jnp.int32, (SUBLANES, W), 0)

    def body(i, hc):
        r0 = pl.multiple_of(i * SUBLANES, SUBLANES)
        av = a_ref[pl.ds(r0, SUBLANES), :]
        bv = b_ref[pl.ds(r0, SUBLANES), :]
        for s in (1, 2, 4):
            keep = row >= s
            a_sh = pltpu.roll(av, s, 0)
            b_sh = pltpu.roll(bv, s, 0)
            bv = jnp.where(keep, av * b_sh + bv, bv)
            av = jnp.where(keep, av * a_sh, av)
        h = av * hc + bv
        b_ref[pl.ds(r0, SUBLANES), :] = h
        return jnp.broadcast_to(h[SUBLANES - 1:SUBLANES, :], (SUBLANES, W))

    hc = lax.fori_loop(0, tl // SUBLANES, body, hc_ref[...])
    hc_ref[...] = hc
    o_ref[0] = (b_ref[...] * _gelu_tanh(yb_ref[0])).astype(o_ref.dtype)
    buf_ref[0] = tail
    hl_ref[0] = hc[0:1, :]


def lru_seq(xy, conv_w, conv_b, wa_bd, b_a, wi_bd, b_i, lam, *, tl=256):
    B, L, W2 = xy.shape
    W = W2 // 2
    tl = min(tl, L)
    assert L % tl == 0 and tl % SUBLANES == 0
    vec = pl.BlockSpec((1, W), lambda b, t: (0, 0))
    wbd = pl.BlockSpec(wa_bd.shape, lambda b, t: (0, 0, 0))
    return pl.pallas_call(
        _lru_seq_kernel,
        grid=(B, L // tl),
        in_specs=[pl.BlockSpec((1, tl, W), lambda b, t: (b, t, 0)),
                  pl.BlockSpec((1, tl, W), lambda b, t: (b, t, 1)),
                  pl.BlockSpec((CONV_W, W), lambda b, t: (0, 0)), vec, wbd, vec, wbd, vec, vec],
        out_specs=(pl.BlockSpec((1, tl, W), lambda b, t: (b, t, 0)),
                   pl.BlockSpec((1, CONV_W - 1, W), lambda b, t: (b, 0, 0)),
                   pl.BlockSpec((1, 1, W), lambda b, t: (b, 0, 0))),
        out_shape=(jax.ShapeDtypeStruct((B, L, W), BF16),
                   jax.ShapeDtypeStruct((B, CONV_W - 1, W), F32),
                   jax.ShapeDtypeStruct((B, 1, W), F32)),
        scratch_shapes=[pltpu.VMEM((tl + SUBLANES, W), F32), pltpu.VMEM((tl, W), F32),
                        pltpu.VMEM((tl, W), F32), pltpu.VMEM((SUBLANES, W), F32)],
        compiler_params=_cparams(("parallel", "arbitrary")),
        name="lru_seq",
    )(xy, xy, conv_w, conv_b.reshape(1, W), wa_bd, b_a.reshape(1, W), wi_bd, b_i.reshape(1, W), lam.reshape(1, W))


def _lru_step_kernel(xb_ref, yb_ref, buf_ref, h0_ref, cw_ref, cb_ref, wa_ref, ba_ref, wi_ref, bi_ref, lam_ref,
                     o_ref, nbuf_ref, hl_ref):
    x = xb_ref[...]
    xc = cb_ref[...] + x * cw_ref[CONV_W - 1:CONV_W, :]
    for k in range(CONV_W - 1):
        xc = xc + buf_ref[k] * cw_ref[k:k + 1, :]
    a, b = _lru_gates(xc, wa_ref, ba_ref, wi_ref, bi_ref, lam_ref)
    h = a * h0_ref[...] + b
    o_ref[...] = (h * _gelu_tanh(yb_ref[...])).astype(o_ref.dtype)
    hl_ref[...] = h
    for k in range(CONV_W - 2):
        nbuf_ref[k] = buf_ref[k + 1]
    nbuf_ref[CONV_W - 2] = x


def lru_step(xy, buf_t, h0, conv_w, conv_b, wa_bd, b_a, wi_bd, b_i, lam):
    B, W2 = xy.shape
    W = W2 // 2
    vec = pl.BlockSpec((1, W), lambda i: (0, 0))
    wbd = pl.BlockSpec(wa_bd.shape, lambda i: (0, 0, 0))
    mat = pl.BlockSpec((B, W), lambda i: (0, 0))
    cube = pl.BlockSpec((CONV_W - 1, B, W), lambda i: (0, 0, 0))
    return pl.pallas_call(
        _lru_step_kernel,
        grid=(1,),
        in_specs=[mat, pl.BlockSpec((B, W), lambda i: (0, 1)), cube, mat,
                  pl.BlockSpec((CONV_W, W), lambda i: (0, 0)), vec, wbd, vec, wbd, vec, vec],
        out_specs=(mat, cube, mat),
        out_shape=(jax.ShapeDtypeStruct((B, W), BF16),
                   jax.ShapeDtypeStruct((CONV_W - 1, B, W), F32),
                   jax.ShapeDtypeStruct((B, W), F32)),
        compiler_params=_cparams(("arbitrary",)),
        name="lru_step",
    )(xy, xy, buf_t, h0, conv_w, conv_b.reshape(1, W), wa_bd, b_a.reshape(1, W), wi_bd, b_i.reshape(1, W),
      lam.reshape(1, W))


def _group_norm_gate(y, z, ng, n_groups):
    y = y * _silu(z)
    gw = y.shape[1] // n_groups
    outs = []
    for g in range(n_groups):
        yg = y[:, g * gw:(g + 1) * gw]
        ms = jnp.mean(yg * yg, axis=-1, keepdims=True)
        outs.append(yg * lax.rsqrt(ms + EPS))
    return jnp.concatenate(outs, axis=1) * ng


def _ssd_seq_kernel(z_ref, xbc_ref, dt_ref, cw_ref, cb_ref, dtb_ref, alog_ref, dsk_ref, ng_ref, ex_ref,
                    o_ref, buf_ref, st_ref, xpad_ref, s_ref):
    t = pl.program_id(1)
    Q = xbc_ref.shape[1]
    DI = z_ref.shape[2]
    GN = SSM_G * SSM_N
    RP = DI // SSM_G
    xc, tail = _conv_tile(xpad_ref, xbc_ref[0], cw_ref, cb_ref, t == 0)
    xc = _silu(xc)
    xs = xc[:, :DI]
    bm = xc[:, DI:DI + GN]
    cm = xc[:, DI + GN:]

    @pl.when(t == 0)
    def _():
        s_ref[...] = jnp.zeros_like(s_ref)

    dt = _softplus(dt_ref[0] + dtb_ref[...])
    a_neg = -jnp.exp(alog_ref[...])
    ri = lax.broadcasted_iota(jnp.int32, (Q, Q), 0)
    ci = lax.broadcasted_iota(jnp.int32, (Q, Q), 1)
    tri = ci <= ri
    acum = _dot_hi(tri.astype(F32), dt * a_neg)
    acum_t = acum.T
    ex = ex_ref[...]
    dt_e = _place_dot(dt, ex)
    acum_e = _place_dot(acum, ex)
    last_e = acum_e[Q - 1:Q, :]
    xdt = xs * dt_e
    xdtw = (xdt * jnp.exp(last_e - acum_e)).astype(BF16)
    xdt = xdt.astype(BF16)
    eacum = jnp.exp(acum_e)
    edec = jnp.exp(last_e)
    lane = lax.broadcasted_iota(jnp.int32, (Q, LANES), 1)
    lo = lane < SSM_P
    ys = []
    for g in range(SSM_G):
        cg = cm[:, g * SSM_N:(g + 1) * SSM_N].astype(BF16)
        bg32 = bm[:, g * SSM_N:(g + 1) * SSM_N]
        bg = bg32.astype(BF16)
        cb = _dot_nt(cg, bg)
        st = s_ref[g]
        yoff = _dot(cg, st.astype(BF16)) * eacum[:, g * RP:(g + 1) * RP]
        for pr in range(RP // LANES):
            ms = []
            for k in range(LANES // SSM_P):
                hd = (g * RP + pr * LANES) // SSM_P + k
                seg = acum[:, hd:hd + 1] - acum_t[hd:hd + 1, :]
                ms.append((cb * jnp.exp(jnp.where(tri, seg, NEG))).astype(BF16))
            c0 = g * RP + pr * LANES
            xp = xdt[:, c0:c0 + LANES]
            zero = jnp.zeros_like(xp)
            rhs = jnp.concatenate([jnp.where(lo, xp, zero), jnp.where(lo, zero, xp)], axis=0)
            ydiag = _dot(jnp.concatenate(ms, axis=1), rhs)
            ys.append(ydiag + yoff[:, pr * LANES:(pr + 1) * LANES])
        new = st * edec[:, g * RP:(g + 1) * RP] + _dot(bg32.T.astype(BF16), xdtw[:, g * RP:(g + 1) * RP])
        s_ref[g] = new
    y = jnp.concatenate(ys, axis=1) + xs * dsk_ref[...]
    o_ref[0] = _group_norm_gate(y, z_ref[0], ng_ref[...], SSM_G).astype(o_ref.dtype)
    buf_ref[0] = tail

    @pl.when(t == pl.num_programs(1) - 1)
    def _():
        hpg = RP // SSM_P
        for g in range(SSM_G):
            st_ref[0, g * hpg:(g + 1) * hpg] = s_ref[g].T.reshape(hpg, SSM_P, SSM_N)


def ssd_seq(z, xbc, dt, conv_w, conv_b, dt_bias, a_log, d_lanes, norm_g, expand, *, q=128):
    B, L, DI = z.shape
    C = xbc.shape[2]
    H = DI // SSM_P
    q = min(q, L)
    assert L % q == 0
    vecd = pl.BlockSpec((1, DI), lambda b, t: (0, 0))
    vecl = pl.BlockSpec((1, LANES), lambda b, t: (0, 0))
    return pl.pallas_call(
        _ssd_seq_kernel,
        grid=(B, L // q),
        in_specs=[pl.BlockSpec((1, q, DI), lambda b, t: (b, t, 0)),
                  pl.BlockSpec((1, q, C), lambda b, t: (b, t, 0)),
                  pl.BlockSpec((1, q, LANES), lambda b, t: (b, t, 0)),
                  pl.BlockSpec((CONV_W, C), lambda b, t: (0, 0)),
                  pl.BlockSpec((1, C), lambda b, t: (0, 0)),
                  vecl, vecl, vecd, vecd,
                  pl.BlockSpec((LANES, DI), lambda b, t: (0, 0))],
        out_specs=(pl.BlockSpec((1, q, DI), lambda b, t: (b, t, 0)),
                   pl.BlockSpec((1, CONV_W - 1, C), lambda b, t: (b, 0, 0)),
                   pl.BlockSpec((1, H, SSM_P, SSM_N), lambda b, t: (b, 0, 0, 0))),
        out_shape=(jax.ShapeDtypeStruct((B, L, DI), BF16),
                   jax.ShapeDtypeStruct((B, CONV_W - 1, C), F32),
                   jax.ShapeDtypeStruct((B, H, SSM_P, SSM_N), F32)),
        scratch_shapes=[pltpu.VMEM((q + SUBLANES, C), F32),
                        pltpu.VMEM((SSM_G, SSM_N, DI // SSM_G), F32)],
        compiler_params=_cparams(("parallel", "arbitrary")),
        name="ssd_seq",
    )(z, xbc, dt, conv_w, conv_b.reshape(1, C), dt_bias, a_log, d_lanes, norm_g.reshape(1, DI), expand)


def _ssd_step_kernel(z_ref, xbc_ref, dt_ref, buf_ref, h0_ref, cw_ref, cb_ref, dtb_ref, alog_ref, dsk_ref, ng_ref,
                     ex_ref, o_ref, nbuf_ref, hn_ref):
    DI = z_ref.shape[2]
    GN = SSM_G * SSM_N
    hpg = DI // SSM_P // SSM_G
    x = xbc_ref[0]
    buf = buf_ref[0]
    xc = cb_ref[...] + x * cw_ref[CONV_W - 1:CONV_W, :]
    for k in range(CONV_W - 1):
        xc = xc + buf[k:k + 1, :] * cw_ref[k:k + 1, :]
    nbuf_ref[0, 0:CONV_W - 2, :] = buf[1:CONV_W - 1, :]
    nbuf_ref[0, CONV_W - 2:CONV_W - 1, :] = x
    xc = _silu(xc)
    xs = xc[:, :DI]
    dt = _softplus(dt_ref[0] + dtb_ref[...])
    dta = dt * (-jnp.exp(alog_ref[...]))
    ex = ex_ref[...]
    dec = jnp.exp(dta)
    dt_e = _place_dot(jnp.broadcast_to(dt, (SUBLANES, LANES)), ex)[0:1, :]
    xdt = xs * dt_e
    RP = hpg * SSM_P
    eye = (lax.broadcasted_iota(jnp.int32, (RP, RP), 0) == lax.broadcasted_iota(jnp.int32, (RP, RP), 1))
    ys = []
    for g in range(SSM_G):
        brow = xc[:, DI + g * SSM_N:DI + (g + 1) * SSM_N]
        crow = xc[:, DI + GN + g * SSM_N:DI + GN + (g + 1) * SSM_N]
        xg = xdt[:, g * RP:(g + 1) * RP]
        xcol = jnp.sum(jnp.where(eye, jnp.broadcast_to(xg, (RP, RP)), 0.0), axis=-1, keepdims=True)
        news = []
        for r in range(hpg):
            hd = g * hpg + r
            new = h0_ref[0, hd] * dec[:, hd:hd + 1] + xcol[r * SSM_P:(r + 1) * SSM_P, :] * brow
            hn_ref[0, hd] = new
            news.append(new)
        new_g = jnp.concatenate(news, axis=0).astype(BF16)
        ys.append(_dot_nt(jnp.broadcast_to(crow, (SUBLANES, SSM_N)).astype(BF16), new_g)[0:1, :])
    y = jnp.concatenate(ys, axis=1) + xs * dsk_ref[...]
    o_ref[0] = _group_norm_gate(y, z_ref[0], ng_ref[...], SSM_G).astype(o_ref.dtype)


def ssd_step(z, xbc, dt, buf, h0, conv_w, conv_b, dt_bias, a_log, d_lanes, norm_g, expand):
    B, _, DI = z.shape
    C = xbc.shape[2]
    H = DI // SSM_P
    vecd = pl.BlockSpec((1, DI), lambda b: (0, 0))
    vecl = pl.BlockSpec((1, LANES), lambda b: (0, 0))
    st = pl.BlockSpec((1, H, SSM_P, SSM_N), lambda b: (b, 0, 0, 0))
    return pl.pallas_call(
        _ssd_step_kernel,
        grid=(B,),
        in_specs=[pl.BlockSpec((1, 1, DI), lambda b: (b, 0, 0)),
                  pl.BlockSpec((1, 1, C), lambda b: (b, 0, 0)),
                  pl.BlockSpec((1, 1, LANES), lambda b: (b, 0, 0)),
                  pl.BlockSpec((1, CONV_W - 1, C), lambda b: (b, 0, 0)), st,
                  pl.BlockSpec((CONV_W, C), lambda b: (0, 0)),
                  pl.BlockSpec((1, C), lambda b: (0, 0)),
                  vecl, vecl, vecd, vecd,
                  pl.BlockSpec((LANES, DI), lambda b: (0, 0))],
        out_specs=(pl.BlockSpec((1, 1, DI), lambda b: (b, 0, 0)),
                   pl.BlockSpec((1, CONV_W - 1, C), lambda b: (b, 0, 0)), st),
        out_shape=(jax.ShapeDtypeStruct((B, 1, DI), BF16),
                   jax.ShapeDtypeStruct((B, CONV_W - 1, C), F32),
                   jax.ShapeDtypeStruct((B, H, SSM_P, SSM_N), F32)),
        compiler_params=_cparams(("parallel",)),
        name="ssd_step",
    )(z, xbc, dt, buf, h0, conv_w, conv_b.reshape(1, C), dt_bias, a_log, d_lanes, norm_g.reshape(1, DI), expand)


def _split_dot(x, m):
    hi = x.astype(BF16)
    lo = (x - hi.astype(F32)).astype(BF16)
    return _dot(hi, m) + _dot(lo, m)


LOG2E = 1.4426950408889634
N_PIECES = 3
PAIR = LANES // ATT_HD


def _fox_prep_kernel(q_ref, k_ref, v_ref, fl_ref, bf_ref, qg_ref, kg_ref, hs_ref, he_ref, pl_ref,
                     qo_ref, ko_ref, lf_ref, kx_ref, vt_ref, kt32_ref, vt32_ref, carry_ref):
    t = pl.program_id(1)
    tl = q_ref.shape[1]
    D = q_ref.shape[2]

    def head_norm(x, g):
        ss = _split_dot(x * x, hs_ref[...])
        inv = lax.rsqrt(ss * (1.0 / ATT_HD) + EPS)
        return x * _split_dot(inv, he_ref[...]) * g

    qo_ref[0] = (head_norm(q_ref[0], qg_ref[...]) * (ATT_HD ** -0.5 * LOG2E)).astype(qo_ref.dtype)
    kn = head_norm(k_ref[0], kg_ref[...])
    ko_ref[0] = kn
    kt32_ref[0] = kn.T
    vt = v_ref[0].T
    vt32_ref[0] = vt
    vt_ref[0] = vt.astype(BF16)
    z = fl_ref[0] + bf_ref[...]
    logf = jnp.minimum(z, 0.0) - jnp.log1p(jnp.exp(-jnp.abs(z)))
    lf_ref[0] = logf

    @pl.when(t == 0)
    def _():
        carry_ref[...] = jnp.zeros_like(carry_ref)

    ri = lax.broadcasted_iota(jnp.int32, (tl, tl), 0)
    ci = lax.broadcasted_iota(jnp.int32, (tl, tl), 1)
    c = _dot_hi((ci <= ri).astype(F32), logf) + carry_ref[0:1, :]
    carry_ref[...] = jnp.broadcast_to(c[tl - 1:tl, :], carry_ref.shape)
    rest = c * (-LOG2E)
    extra = jnp.zeros((tl, D), F32)
    for j in range(N_PIECES):
        piece = rest.astype(BF16)
        rest = rest - piece.astype(F32)
        extra = extra + _dot(piece, pl_ref[j])
    knb = kn.astype(BF16)
    extra = extra.astype(BF16)
    kx_ref[0] = jnp.concatenate(
        [x[:, p * LANES:(p + 1) * LANES] for p in range(D // LANES) for x in (knb, extra)], axis=1)


def fox_prep(q, k, v, fl, b_f, q_g, k_g, head_sum, head_expand, place, *, tl=256):
    B, L, D = q.shape
    tl = min(tl, L)
    assert L % tl == 0
    row = pl.BlockSpec((1, tl, D), lambda b, t: (b, t, 0))
    colt = pl.BlockSpec((1, D, tl), lambda b, t: (b, 0, t))
    nar = pl.BlockSpec((1, tl, LANES), lambda b, t: (b, t, 0))
    vecd = pl.BlockSpec((1, D), lambda b, t: (0, 0))
    return pl.pallas_call(
        _fox_prep_kernel,
        grid=(B, L // tl),
        in_specs=[row, row, row, nar, pl.BlockSpec((1, LANES), lambda b, t: (0, 0)), vecd, vecd,
                  pl.BlockSpec((D, LANES), lambda b, t: (0, 0)),
                  pl.BlockSpec((LANES, D), lambda b, t: (0, 0)),
                  pl.BlockSpec((N_PIECES, LANES, D), lambda b, t: (0, 0, 0))],
        out_specs=(row, row, nar, pl.BlockSpec((1, tl, 2 * D), lambda b, t: (b, t, 0)),
                   colt, colt, colt),
        out_shape=(jax.ShapeDtypeStruct((B, L, D), BF16),
                   jax.ShapeDtypeStruct((B, L, D), F32),
                   jax.ShapeDtypeStruct((B, L, LANES), F32),
                   jax.ShapeDtypeStruct((B, L, 2 * D), BF16),
                   jax.ShapeDtypeStruct((B, D, L), BF16),
                   jax.ShapeDtypeStruct((B, D, L), F32),
                   jax.ShapeDtypeStruct((B, D, L), F32)),
        scratch_shapes=[pltpu.VMEM((SUBLANES, LANES), F32)],
        compiler_params=_cparams(("parallel", "arbitrary")),
        name="fox_prep",
    )(q, k, v, fl, b_f, q_g, k_g, head_sum, head_expand, place)


def _fox_attn_kernel(q_ref, kx_ref, vt_ref, g_ref, o_ref, qx_ref, m_ref, l_ref, acc_ref, s_ref):
    qi = pl.program_id(2)
    tq = q_ref.shape[1]
    q = q_ref[0].astype(F32)
    lane = lax.broadcasted_iota(jnp.int32, (tq, LANES), 1)
    for k in range(PAIR):
        mine = (lane >= k * ATT_HD) & (lane < (k + 1) * ATT_HD)
        pick = (lane >= k * N_PIECES) & (lane < (k + 1) * N_PIECES)
        qx_ref[k] = jnp.concatenate([jnp.where(mine, q, 0.0), jnp.where(pick, 1.0, 0.0)], axis=1).astype(BF16)
    m_ref[...] = jnp.full_like(m_ref, NEG)
    l_ref[...] = jnp.zeros_like(l_ref)
    acc_ref[...] = jnp.zeros_like(acc_ref)

    def scores(j):
        r0 = pl.multiple_of(j * tq, tq)
        kb = kx_ref[0, pl.ds(r0, tq), :]
        return [_dot_nt(kb, qx_ref[k]) for k in range(PAIR)]

    def consume(j, diagonal):
        r0 = pl.multiple_of(j * tq, tq)
        vt = vt_ref[0, :, pl.ds(r0, tq)]
        for k in range(PAIR):
            s = s_ref[k]
            if diagonal:
                rows = lax.broadcasted_iota(jnp.int32, (tq, tq), 0)
                cols = lax.broadcasted_iota(jnp.int32, (tq, tq), 1)
                s = jnp.where(rows <= cols, s, NEG)
            m_old = m_ref[k]
            m_new = jnp.maximum(m_old, jnp.max(s, axis=0, keepdims=True))
            alpha = jnp.exp2(m_old - m_new)
            p = jnp.exp2(s - m_new)
            l_ref[k] = l_ref[k] * alpha + jnp.sum(p, axis=0, keepdims=True)
            m_ref[k] = m_new
            acc_ref[k] = acc_ref[k] * alpha + _dot(vt[k * ATT_HD:(k + 1) * ATT_HD, :], p.astype(BF16))

    def stash(s_list):
        for k in range(PAIR):
            s_ref[k] = s_list[k]

    stash(scores(0))

    def body(j, carry):
        nxt = scores(j + 1)
        consume(j, False)
        stash(nxt)
        return carry

    lax.fori_loop(0, qi, body, 0)
    consume(qi, True)
    o = jnp.concatenate([acc_ref[k] / l_ref[k] for k in range(PAIR)], axis=0)
    o_ref[0] = (o.T * _sigmoid(g_ref[0])).astype(o_ref.dtype)


def fox_attn(q, kx, vt, g, *, tq=512):
    B, L, D = q.shape
    tq = min(tq, L)
    assert L % tq == 0
    qs = pl.BlockSpec((1, tq, LANES), lambda b, p, i: (b, i, p))
    return pl.pallas_call(
        _fox_attn_kernel,
        grid=(B, D // LANES, L // tq),
        in_specs=[qs,
                  pl.BlockSpec((1, L, 2 * LANES), lambda b, p, i: (b, 0, p)),
                  pl.BlockSpec((1, LANES, L), lambda b, p, i: (b, p, 0)),
                  qs],
        out_specs=qs,
        out_shape=jax.ShapeDtypeStruct((B, L, D), BF16),
        scratch_shapes=[pltpu.VMEM((PAIR, tq, 2 * LANES), BF16), pltpu.VMEM((PAIR, 1, tq), F32),
                        pltpu.VMEM((PAIR, 1, tq), F32), pltpu.VMEM((PAIR, ATT_HD, tq), F32),
                        pltpu.VMEM((PAIR, tq, tq), F32)],
        compiler_params=_cparams(("parallel", "parallel", "arbitrary")),
        name="fox_attn",
    )(q, kx, vt, g)


def _lanes_to_sublanes(row):
    n = row.shape[1]
    eye = lax.broadcasted_iota(jnp.int32, (n, n), 0) == lax.broadcasted_iota(jnp.int32, (n, n), 1)
    return jnp.sum(jnp.where(eye, jnp.broadcast_to(row, (n, n)), 0.0), axis=1, keepdims=True)


def _sublanes_to_lanes(col):
    n = col.shape[0]
    eye = lax.broadcasted_iota(jnp.int32, (n, n), 0) == lax.broadcasted_iota(jnp.int32, (n, n), 1)
    return jnp.sum(jnp.where(eye, jnp.broadcast_to(col, (n, n)), 0.0), axis=0, keepdims=True)


def _fox_decode_kernel(pt_ref, q_ref, kn_ref, vn_ref, lfn_ref, g_ref, *rest, npg):
    k_refs = rest[:npg]
    v_refs = rest[npg:2 * npg]
    lf_refs = rest[2 * npg:3 * npg]
    o_ref = rest[3 * npg]
    qb_ref, m_ref, l_ref, acc_ref, coff_ref = rest[3 * npg + 1:]
    s_id = pl.program_id(1)
    H, hd, ps = k_refs[0].shape[1:]

    @pl.when(s_id == 0)
    def _():
        q = q_ref[0]
        for h in range(H):
            qb_ref[h] = jnp.broadcast_to(_lanes_to_sublanes(q[h:h + 1, :]), (hd, ps))
        m_ref[...] = jnp.full_like(m_ref, NEG)
        l_ref[...] = jnp.zeros_like(l_ref)
        acc_ref[...] = jnp.zeros_like(acc_ref)
        coff_ref[...] = jnp.zeros_like(coff_ref)

    ri = lax.broadcasted_iota(jnp.int32, (ps, ps), 0)
    ci = lax.broadcasted_iota(jnp.int32, (ps, ps), 1)
    upper = (ri <= ci).astype(F32)

    for i in range(npg):
        qk = jnp.concatenate([jnp.sum(k_refs[i][0, h] * qb_ref[h], axis=0, keepdims=True) for h in range(H)], axis=0)
        c = _dot_hi(lf_refs[i][0], upper) + coff_ref[...]
        coff_ref[...] = c[:, ps - 1:ps]
        s = qk - c * LOG2E
        m_old = m_ref[...]
        m_new = jnp.maximum(m_old, jnp.max(s, axis=1, keepdims=True))
        alpha = jnp.exp2(m_old - m_new)
        p = jnp.exp2(s - m_new)
        l_ref[...] = l_ref[...] * alpha + jnp.sum(p, axis=1, keepdims=True)
        m_ref[...] = m_new
        for h in range(H):
            acc_ref[h] = acc_ref[h] * alpha[h:h + 1, :] + p[h:h + 1, :] * v_refs[i][0, h]

    @pl.when(s_id == pl.num_programs(1) - 1)
    def _():
        s = (jnp.sum(q_ref[0] * kn_ref[0], axis=1, keepdims=True)
             - (coff_ref[...] + _lanes_to_sublanes(lfn_ref[0][:, 0:H])) * LOG2E)
        m_old = m_ref[...]
        m_new = jnp.maximum(m_old, s)
        alpha = jnp.exp2(m_old - m_new)
        p = jnp.exp2(s - m_new)
        l = l_ref[...] * alpha + p
        past = jnp.concatenate(
            [_sublanes_to_lanes(jnp.sum(acc_ref[h], axis=1, keepdims=True)) for h in range(H)], axis=0)
        o = (past * alpha + p * vn_ref[0]) / l
        o_ref[0] = (o * _sigmoid(g_ref[0])).astype(o_ref.dtype)


def fox_decode(q, k_new, v_new, lf_new, g, cache_kt, cache_vt, cache_lft, page_table, *, npg=8):
    B, H, hd = q.shape
    n_pages = page_table.shape[1]
    ps = cache_kt.shape[3]
    npg = min(npg, n_pages)
    assert n_pages % npg == 0
    steps = n_pages // npg
    row = pl.BlockSpec((1, H, hd), lambda b, s, pt: (b, 0, 0))

    def page(i, *tail):
        return pl.BlockSpec((1,) + tail, lambda b, s, pt: (pt[b * n_pages + s * npg + i],) + (0,) * len(tail))

    in_specs = ([row, row, row, pl.BlockSpec((1, 1, LANES), lambda b, s, pt: (b, 0, 0)), row]
                + [page(i, H, hd, ps) for i in range(npg)] + [page(i, H, hd, ps) for i in range(npg)]
                + [page(i, H, ps) for i in range(npg)])
    grid_spec = pltpu.PrefetchScalarGridSpec(
        num_scalar_prefetch=1, grid=(B, steps), in_specs=in_specs, out_specs=row,
        scratch_shapes=[pltpu.VMEM((H, hd, ps), F32), pltpu.VMEM((H, 1), F32), pltpu.VMEM((H, 1), F32),
                        pltpu.VMEM((H, hd, ps), F32), pltpu.VMEM((H, 1), F32)])
    return pl.pallas_call(
        functools.partial(_fox_decode_kernel, npg=npg),
        grid_spec=grid_spec,
        out_shape=jax.ShapeDtypeStruct((B, H, hd), BF16),
        compiler_params=_cparams(("parallel", "arbitrary")),
        name="fox_decode",
    )(page_table.reshape(-1), q, k_new, v_new, lf_new, g,
      *([cache_kt] * npg), *([cache_vt] * npg), *([cache_lft] * npg))


def _block_diag_chunks(w, per_chunk):
    nblk, bw, _ = w.shape
    w = w.reshape(nblk // per_chunk, per_chunk, bw, bw)
    eye = jnp.eye(per_chunk, dtype=w.dtype)
    out = jnp.einsum('cpij,pq->cpiqj', w, eye)
    return out.reshape(nblk // per_chunk, per_chunk * bw, per_chunk * bw)


def _head_expand(n_heads, width):
    r = jnp.arange(LANES)[:, None]
    c = jnp.arange(n_heads * width)[None, :] // width
    return (r == c).astype(F32)


def _pad_lanes(v):
    return jnp.pad(v.reshape(1, -1), ((0, 0), (0, LANES - v.size)))


MOE_ROUTED_MIN_TOKENS = 2048


def _moe_route(i, prm, n_tokens):
    if n_tokens < MOE_ROUTED_MIN_TOKENS:
        return None
    return prm['norm_ffn'][i], prm['moe_w_router'][i], prm['moe_b_router'][i]


def _moe_and_ple(h, p_i, i, prm, last):
    moe_args = (prm['norm_ffn'][i], prm['moe_w_router'][i], prm['moe_b_router'][i],
                prm['moe_w_gate'][i], prm['moe_w_up'][i], prm['moe_w_down'][i])
    if isinstance(h, tuple):
        h = moe_routed(h[0], h[1][:, 0], *moe_args)
    else:
        h = moe_dense(h, *moe_args)
    return ple(h, p_i, prm['norm_ple'][i], prm['ple_w_gate'][i], prm['ple_w_proj'][i],
               prm['norm_final'] if last else None)


def _lru_layer_prompt(h, B, L, j, g, prm, route=None):
    D = h.shape[1]
    xy = norm_matmul(h, g, prm['lru_w_in'][j])
    gated, buf, hl = lru_seq(xy.reshape(B, L, -1), prm['lru_conv_w'][j], prm['lru_conv_b'][j],
                             prm['lru_wa_bd'][j], prm['lru_b_a'][j], prm['lru_wi_bd'][j], prm['lru_b_i'][j],
                             prm['lru_lambda'][j])
    h = matmul_res(gated.reshape(B * L, -1), prm['lru_w_out'][j], h, route)
    return h, buf, hl.reshape(B, D)


def _lru_layer_sample(h, buf, h0, j, g, prm):
    xy = norm_matmul(h, g, prm['lru_w_in'][j])
    gated, nbuf, hl = lru_step(xy, jnp.swapaxes(buf, 0, 1), h0, prm['lru_conv_w'][j], prm['lru_conv_b'][j],
                               prm['lru_wa_bd'][j], prm['lru_b_a'][j], prm['lru_wi_bd'][j], prm['lru_b_i'][j],
                               prm['lru_lambda'][j])
    h = matmul_res(gated, prm['lru_w_out'][j], h)
    return h, jnp.swapaxes(nbuf, 0, 1), hl


def _ssd_proj(h, g, j, prm):
    z = norm_matmul(h, g, prm['ssm_w_z'][j])
    xbc = norm_matmul(h, g, prm['ssm_w_xbc'][j])
    dt = norm_matmul_hi(h, g, prm['ssm_w_dt'][j])
    return z, xbc, dt


def _ssd_args(j, prm):
    return (prm['ssm_conv_w'][j], prm['ssm_conv_b'][j], prm['ssm_dt_bias'][j], prm['ssm_a_log'][j],
            prm['ssm_d_lanes'][j], prm['ssm_norm'][j], prm['ssm_expand'])


def _ssd_layer_prompt(h, B, L, j, g, prm, route=None):
    z, xbc, dt = _ssd_proj(h, g, j, prm)
    y, buf, st = ssd_seq(z.reshape(B, L, -1), xbc.reshape(B, L, -1), dt.reshape(B, L, -1), *_ssd_args(j, prm))
    return matmul_res(y.reshape(B * L, -1), prm['ssm_w_out'][j], h, route), buf, st


def _ssd_layer_sample(h, buf, h0, j, g, prm):
    B = h.shape[0]
    z, xbc, dt = _ssd_proj(h, g, j, prm)
    y, nbuf, st = ssd_step(z.reshape(B, 1, -1), xbc.reshape(B, 1, -1), dt.reshape(B, 1, -1), buf, h0,
                           *_ssd_args(j, prm))
    return matmul_res(y.reshape(B, -1), prm['ssm_w_out'][j], h), nbuf, st


def _fox_proj(h, g, j, prm):
    q = norm_matmul(h, g, prm['fox_w_q'][j])
    k = norm_matmul(h, g, prm['fox_w_k'][j])
    v = norm_matmul(h, g, prm['fox_w_v'][j])
    og = norm_matmul(h, g, prm['fox_w_g'][j])
    fl = norm_matmul_hi(h, g, prm['fox_w_f'][j], exact=False)
    return q, k, v, og, fl


def _fox_prep_args(j, prm):
    return (prm['fox_b_f'][j], prm['fox_q_norm'][j], prm['fox_k_norm'][j], prm['fox_head_sum'],
            prm['fox_head_expand'], prm['fox_place'])


def _fox_layer_prompt(h, B, L, j, g, prm, route=None, tq=512):
    q, k, v, og, fl = _fox_proj(h, g, j, prm)
    shp = (B, L, -1)
    qs, _, logf, kx, vt, kt32, vt32 = fox_prep(q.reshape(shp), k.reshape(shp), v.reshape(shp), fl.reshape(shp),
                                               *_fox_prep_args(j, prm))
    o = fox_attn(qs, kx, vt, og.reshape(shp), tq=tq)

    def per_head(xt):
        return jnp.transpose(xt.reshape(B, -1, ATT_HD, L), (0, 3, 1, 2))

    return matmul_res(o.reshape(B * L, -1), prm['fox_w_out'][j], h, route), per_head(kt32), per_head(vt32), logf


def _fox_layer_sample(h, cache, n_phys, page_table, j, g, prm):
    B, D = h.shape
    q, k, v, og, fl = _fox_proj(h, g, j, prm)
    one = (1, B, -1)
    qs, kn, logf = fox_prep(q.reshape(one), k.reshape(one), v.reshape(one), fl.reshape(one),
                            *_fox_prep_args(j, prm))[:3]
    ck, cv, clf = cache
    heads = (B, D // ATT_HD, ATT_HD)
    tok = (B, 1, -1)
    o = fox_decode(qs.astype(F32).reshape(heads), kn.reshape(heads), v.reshape(heads), logf.reshape(tok),
                   og.reshape(heads), ck, cv, clf, page_table + j * n_phys)
    return matmul_res(o.reshape(B, D), prm['fox_w_out'][j], h), kn.reshape(tok), v.reshape(tok), logf.reshape(tok)


def _prepare_params(raw):
    prm = dict(raw)
    D = raw['norm_final'].shape[0]
    for name in ('lru_w_in', 'lru_w_out', 'ssm_w_out', 'fox_w_out', 'moe_w_gate', 'moe_w_up', 'moe_w_down',
                 'ple_w_proj', 'ple_w_gate'):
        prm[name] = raw[name].astype(BF16)
    per = 2 * LANES // (D // LRU_BLOCKS)
    prm['lru_wa_bd'] = jax.vmap(lambda w: _block_diag_chunks(w, per))(raw['lru_w_a']).astype(BF16)
    prm['lru_wi_bd'] = jax.vmap(lambda w: _block_diag_chunks(w, per))(raw['lru_w_i']).astype(BF16)
    prm['lru_b_a'] = raw['lru_b_a'].reshape(raw['lru_b_a'].shape[0], -1)
    prm['lru_b_i'] = raw['lru_b_i'].reshape(raw['lru_b_i'].shape[0], -1)
    n_h = raw['ssm_a_log'].shape[1]
    di = n_h * SSM_P
    w = raw['ssm_w_in']
    conv_dim = raw['ssm_conv_w'].shape[2]
    prm['ssm_w_z'] = w[:, :, :di].astype(BF16)
    prm['ssm_w_xbc'] = w[:, :, di:di + conv_dim].astype(BF16)
    prm['ssm_w_dt'] = jax.vmap(lambda m: _pad_cols(m, LANES))(w[:, :, di + conv_dim:])
    prm['ssm_dt_bias'] = jax.vmap(_pad_lanes)(raw['ssm_dt_bias'])
    prm['ssm_a_log'] = jax.vmap(_pad_lanes)(raw['ssm_a_log'])
    prm['ssm_d_lanes'] = jnp.repeat(raw['ssm_d'], SSM_P, axis=1)[:, None, :]
    prm['ssm_expand'] = _head_expand(n_h, SSM_P).astype(BF16)
    w = raw['fox_w_in']
    for n, name in enumerate(('fox_w_q', 'fox_w_k', 'fox_w_v', 'fox_w_g')):
        prm[name] = w[:, :, n * D:(n + 1) * D].astype(BF16)
    prm['fox_w_f'] = jax.vmap(lambda m: _pad_cols(m, LANES))(w[:, :, 4 * D:])
    prm['fox_b_f'] = jax.vmap(_pad_lanes)(raw['fox_b_f'])
    n_ah = D // ATT_HD
    prm['fox_q_norm'] = jnp.tile(raw['fox_q_norm'], (1, n_ah))[:, None, :]
    prm['fox_k_norm'] = jnp.tile(raw['fox_k_norm'], (1, n_ah))[:, None, :]
    prm['fox_head_expand'] = _head_expand(n_ah, ATT_HD)
    prm['fox_head_sum'] = prm['fox_head_expand'].T.astype(BF16)
    hh = jnp.arange(LANES)[None, :, None]
    jj = jnp.arange(N_PIECES)[:, None, None]
    col = jnp.arange(D)[None, None, :]
    prm['fox_place'] = ((hh < n_ah) & (col == LANES * (hh // PAIR) + N_PIECES * (hh % PAIR) + jj)).astype(BF16)
    prm['moe_w_router'] = jax.vmap(lambda we, wg: _pad_cols(jnp.concatenate([we, wg], axis=1), LANES))(
        raw['moe_w_expert'], raw['moe_w_group'])
    prm['moe_b_router'] = jax.vmap(lambda be, bg: _pad_lanes(jnp.concatenate([be, bg])))(
        raw['moe_b_expert'], raw['moe_b_group'])
    return prm


def kernel(x_prompt, x_sample, state_lru_h, state_lru_conv, state_ssm_h, state_ssm_conv, cache_k, cache_v, cache_logf, page_table, p_prompt, p_sample, lru_w_in, lru_conv_w, lru_conv_b, lru_w_a, lru_b_a, lru_w_i, lru_b_i, lru_lambda, lru_w_out, ssm_w_in, ssm_conv_w, ssm_conv_b, ssm_dt_bias, ssm_a_log, ssm_d, ssm_norm, ssm_w_out, fox_w_in, fox_b_f, fox_q_norm, fox_k_norm, fox_w_out, moe_w_group, moe_b_group, moe_w_expert, moe_b_expert, moe_w_gate, moe_w_up, moe_w_down, ple_w_proj, ple_w_gate, norm_mix, norm_ffn, norm_ple, norm_final):
    prm = _prepare_params(dict(
        lru_w_in=lru_w_in, lru_conv_w=lru_conv_w, lru_conv_b=lru_conv_b, lru_w_a=lru_w_a, lru_b_a=lru_b_a,
        lru_w_i=lru_w_i, lru_b_i=lru_b_i, lru_lambda=lru_lambda, lru_w_out=lru_w_out, ssm_w_in=ssm_w_in,
        ssm_conv_w=ssm_conv_w, ssm_conv_b=ssm_conv_b, ssm_dt_bias=ssm_dt_bias, ssm_a_log=ssm_a_log, ssm_d=ssm_d,
        ssm_norm=ssm_norm, ssm_w_out=ssm_w_out, fox_w_in=fox_w_in, fox_b_f=fox_b_f, fox_q_norm=fox_q_norm,
        fox_k_norm=fox_k_norm, fox_w_out=fox_w_out, moe_w_group=moe_w_group, moe_b_group=moe_b_group,
        moe_w_expert=moe_w_expert, moe_b_expert=moe_b_expert, moe_w_gate=moe_w_gate, moe_w_up=moe_w_up,
        moe_w_down=moe_w_down, ple_w_proj=ple_w_proj, ple_w_gate=ple_w_gate, norm_mix=norm_mix,
        norm_ffn=norm_ffn, norm_ple=norm_ple, norm_final=norm_final))
    depth = norm_mix.shape[0]
    B, L, D = x_prompt.shape
    Bs = x_sample.shape[0]
    n_mix = 3
    att_h = D // ATT_HD
    n_phys, page = cache_k.shape[1], cache_k.shape[2]
    cache = (jnp.transpose(cache_k, (0, 1, 3, 4, 2)).reshape(-1, att_h, ATT_HD, page),
             jnp.transpose(cache_v, (0, 1, 3, 4, 2)).reshape(-1, att_h, ATT_HD, page),
             jnp.transpose(cache_logf, (0, 1, 3, 2)).reshape(-1, att_h, page))

    hp = x_prompt.reshape(B * L, D)
    hs = x_sample.reshape(Bs, D)
    outs = {k: [] for k in ('lru_h_p', 'lru_h_s', 'lru_c_p', 'lru_c_s', 'ssm_h_p', 'ssm_h_s', 'ssm_c_p', 'ssm_c_s',
                            'k_p', 'k_s', 'v_p', 'v_s', 'lf_p', 'lf_s')}
    yp = ys = None
    for i in range(depth):
        j = i // n_mix
        g = prm['norm_mix'][i]
        if i % n_mix == 0:
            hp, buf, hl = _lru_layer_prompt(hp, B, L, j, g, prm, _moe_route(i, prm, B * L))
            outs['lru_c_p'].append(buf)
            outs['lru_h_p'].append(hl)
            hs, buf, hl = _lru_layer_sample(hs, state_lru_conv[j], state_lru_h[j], j, g, prm)
            outs['lru_c_s'].append(buf)
            outs['lru_h_s'].append(hl)
        elif i % n_mix == 1:
            hp, buf, st = _ssd_layer_prompt(hp, B, L, j, g, prm, _moe_route(i, prm, B * L))
            outs['ssm_c_p'].append(buf)
            outs['ssm_h_p'].append(st)
            hs, buf, st = _ssd_layer_sample(hs, state_ssm_conv[j], state_ssm_h[j], j, g, prm)
            outs['ssm_c_s'].append(buf)
            outs['ssm_h_s'].append(st)
        else:
            hp, k, v, lf = _fox_layer_prompt(hp, B, L, j, g, prm, _moe_route(i, prm, B * L))
            outs['k_p'].append(k)
            outs['v_p'].append(v)
            outs['lf_p'].append(lf[:, :, :att_h])
            hs, k, v, lf = _fox_layer_sample(hs, cache, n_phys, page_table, j, g, prm)
            outs['k_s'].append(k.reshape(Bs, 1, att_h, ATT_HD))
            outs['v_s'].append(v.reshape(Bs, 1, att_h, ATT_HD))
            outs['lf_s'].append(lf[:, :, :att_h])
        last = i == depth - 1
        hp = _moe_and_ple(hp, p_prompt[i].reshape(B * L, -1), i, prm, last)
        hs = _moe_and_ple(hs, p_sample[i].reshape(Bs, -1), i, prm, last)
        if last:
            hp, yp = hp
            hs, ys = hs
    st = {k: jnp.stack(v) for k, v in outs.items()}
    return (yp.reshape(B, L, D), ys.reshape(Bs, 1, D), st['lru_h_p'], st['lru_h_s'], st['lru_c_p'], st['lru_c_s'],
            st['ssm_h_p'], st['ssm_h_s'], st['ssm_c_p'], st['ssm_c_s'], st['k_p'], st['k_s'], st['v_p'], st['v_s'],
            st['lf_p'], st['lf_s'])
```

```python
import functools

import jax
import jax.numpy as jnp
from jax import lax
from jax.experimental import pallas as pl
from jax.experimental.pallas import tpu as pltpu

F32 = jnp.float32
BF16 = jnp.bfloat16
HI = lax.Precision.HIGHEST

EPS = 1e-6
CONV_W = 4
LANES = 128
SUBLANES = 8
LRU_C = 8.0
LRU_BLOCKS = 16
SSM_P = 64
SSM_G = 8
SSM_N = 128
ATT_HD = 64
MOE_GROUPS = 4
MOE_EPG = 4
MOE_E = MOE_GROUPS * MOE_EPG
NEG = -1e30
MIB = 1024 * 1024


def _cparams(sem, vmem_mib=48):
    return pltpu.CompilerParams(dimension_semantics=sem, vmem_limit_bytes=vmem_mib * MIB)


def _rms(x, g):
    ms = jnp.mean(x * x, axis=-1, keepdims=True)
    return x * lax.rsqrt(ms + EPS) * g


def _softplus(z):
    return jnp.maximum(z, 0.0) + jnp.log1p(jnp.exp(-jnp.abs(z)))


def _sigmoid(z):
    return 1.0 / (1.0 + jnp.exp(-z))


def _silu(z):
    return z * _sigmoid(z)


def _gelu_tanh(z):
    c = 0.7978845608028654
    return 0.5 * z * (1.0 + jnp.tanh(c * (z + 0.044715 * (z * z * z))))


def _dot(a, b):
    return jnp.dot(a, b, preferred_element_type=F32)


def _dot_hi(a, b):
    return jnp.dot(a, b, preferred_element_type=F32, precision=HI)


def _dot_nt(a, b):
    return lax.dot_general(a, b, (((1,), (1,)), ((), ())), preferred_element_type=F32)


def _dot_nt_hi(a, b):
    return lax.dot_general(a, b, (((1,), (1,)), ((), ())), preferred_element_type=F32, precision=HI)


def _place_dot(x, m, pieces=3):
    out = None
    rest = x
    for _ in range(pieces):
        piece = rest.astype(BF16)
        rest = rest - piece.astype(F32)
        out = _dot(piece, m) if out is None else out + _dot(piece, m)
    return out


def _pad_cols(w, n):
    return jnp.pad(w, ((0, 0), (0, n - w.shape[1])))


def _norm_matmul_kernel(x_ref, g_ref, w_ref, o_ref, xn_ref):
    @pl.when(pl.program_id(1) == 0)
    def _():
        xn_ref[...] = _rms(x_ref[...], g_ref[...]).astype(xn_ref.dtype)

    o_ref[...] = _dot(xn_ref[...], w_ref[...]).astype(o_ref.dtype)


def _norm_matmul_hi_kernel(x_ref, g_ref, w_ref, o_ref, *, exact):
    xn = _rms(x_ref[...], g_ref[...])
    o_ref[...] = _dot_hi(xn, w_ref[...]) if exact else _dot_x3(xn, w_ref[...])


def norm_matmul(x, g, w, *, tm=1024, tn=1024, out_dtype=F32):
    T, D = x.shape
    N = w.shape[1]
    tm, tn = min(tm, T), min(tn, N)
    assert T % tm == 0 and N % tn == 0
    return pl.pallas_call(
        _norm_matmul_kernel,
        grid=(T // tm, N // tn),
        in_specs=[pl.BlockSpec((tm, D), lambda i, j: (i, 0)),
                  pl.BlockSpec((1, D), lambda i, j: (0, 0)),
                  pl.BlockSpec((D, tn), lambda i, j: (0, j))],
        out_specs=pl.BlockSpec((tm, tn), lambda i, j: (i, j)),
        out_shape=jax.ShapeDtypeStruct((T, N), out_dtype),
        scratch_shapes=[pltpu.VMEM((tm, D), BF16)],
        compiler_params=_cparams(("parallel", "arbitrary")),
        name="norm_matmul",
    )(x, g.reshape(1, D), w)


def norm_matmul_hi(x, g, w, *, exact=True, tm=512):
    T, D = x.shape
    N = w.shape[1]
    tm = min(tm, T)
    assert T % tm == 0
    return pl.pallas_call(
        functools.partial(_norm_matmul_hi_kernel, exact=exact),
        grid=(T // tm,),
        in_specs=[pl.BlockSpec((tm, D), lambda i: (i, 0)),
                  pl.BlockSpec((1, D), lambda i: (0, 0)),
                  pl.BlockSpec((D, N), lambda i: (0, 0))],
        out_specs=pl.BlockSpec((tm, N), lambda i: (i, 0)),
        out_shape=jax.ShapeDtypeStruct((T, N), F32),
        compiler_params=_cparams(("parallel",)),
        name="norm_matmul_hi",
    )(x, g.reshape(1, D), w)


def _matmul_res_kernel(a_ref, w_ref, r_ref, *rest):
    out = r_ref[...] + _dot(a_ref[...], w_ref[...])
    if len(rest) == 1:
        rest[0][...] = out
    else:
        g_ref, wr_ref, br_ref, o_ref, cls_ref = rest
        o_ref[...] = out
        cls_ref[...] = jnp.broadcast_to(_moe_class(out, g_ref, wr_ref, br_ref), cls_ref.shape)


def matmul_res(a, w, res, route=None, *, tm=512):
    T, K = a.shape
    N = w.shape[1]
    tm = min(tm, T)
    assert T % tm == 0
    row = pl.BlockSpec((tm, N), lambda i: (i, 0))
    in_specs = [pl.BlockSpec((tm, K), lambda i: (i, 0)), pl.BlockSpec((K, N), lambda i: (0, 0)), row]
    args = (a, w, res)
    out_specs, out_shape = row, jax.ShapeDtypeStruct((T, N), F32)
    if route is not None:
        g, wr, br = route
        in_specs += [pl.BlockSpec((1, N), lambda i: (0, 0)), pl.BlockSpec((N, LANES), lambda i: (0, 0)),
                     pl.BlockSpec((1, LANES), lambda i: (0, 0))]
        args += (g.reshape(1, N), wr, br)
        out_specs = (row, pl.BlockSpec((tm, LANES), lambda i: (i, 0)))
        out_shape = (out_shape, jax.ShapeDtypeStruct((T, LANES), jnp.int32))
    return pl.pallas_call(
        _matmul_res_kernel,
        grid=(T // tm,),
        in_specs=in_specs,
        out_specs=out_specs,
        out_shape=out_shape,
        compiler_params=_cparams(("parallel",)),
        name="matmul_res",
    )(*args)


def _ple_kernel(h_ref, p_ref, g_ref, wg_ref, wp_ref, gf_ref, o_ref, *maybe_final):
    h = h_ref[...]
    xn = _rms(h, g_ref[...]).astype(BF16)
    gate = _sigmoid(_dot(xn, wg_ref[...]))
    out = h + gate * _dot(p_ref[...].astype(BF16), wp_ref[...])
    o_ref[...] = out
    if maybe_final:
        maybe_final[0][...] = _rms(out, gf_ref[...])


def ple(h, p, g, w_gate, w_proj, g_final=None, *, tm=512):
    T, D = h.shape
    P = p.shape[1]
    tm = min(tm, T)
    assert T % tm == 0
    final = g_final is not None
    gf = (g_final if final else g).reshape(1, D)
    row = pl.BlockSpec((tm, D), lambda i: (i, 0))
    vec = pl.BlockSpec((1, D), lambda i: (0, 0))
    out_shape = jax.ShapeDtypeStruct((T, D), F32)
    return pl.pallas_call(
        _ple_kernel,
        grid=(T // tm,),
        in_specs=[row, pl.BlockSpec((tm, P), lambda i: (i, 0)), vec,
                  pl.BlockSpec((D, D), lambda i: (0, 0)),
                  pl.BlockSpec((P, D), lambda i: (0, 0)), vec],
        out_specs=(row, row) if final else row,
        out_shape=(out_shape, out_shape) if final else out_shape,
        compiler_params=_cparams(("parallel",)),
        name="ple",
    )(h, p, g.reshape(1, D), w_gate, w_proj, gf)


def _route(logits):
    lane = lax.broadcasted_iota(jnp.int32, logits.shape, 1)
    big = jnp.int32(1 << 20)
    is_g = (lane >= MOE_E) & (lane < MOE_E + MOE_GROUPS)
    glog = jnp.where(is_g, logits, NEG)
    gmax = jnp.max(glog, axis=-1, keepdims=True)
    gsel = jnp.min(jnp.where(is_g & (glog == gmax), lane, big), axis=-1, keepdims=True) - MOE_E
    gden = jnp.sum(jnp.where(is_g, jnp.exp(glog - gmax), 0.0), axis=-1, keepdims=True)
    gprob = 1.0 / gden
    in_g = (lane < MOE_E) & ((lane // MOE_EPG) == gsel)
    e1 = jnp.where(in_g, logits, NEG)
    v1 = jnp.max(e1, axis=-1, keepdims=True)
    i1 = jnp.min(jnp.where(in_g & (e1 == v1), lane, big), axis=-1, keepdims=True)
    in_g2 = in_g & (lane != i1)
    e2 = jnp.where(in_g2, logits, NEG)
    v2 = jnp.max(e2, axis=-1, keepdims=True)
    i2 = jnp.min(jnp.where(in_g2 & (e2 == v2), lane, big), axis=-1, keepdims=True)
    t = jnp.exp(v2 - v1)
    w1 = gprob / (1.0 + t)
    w2 = gprob * t / (1.0 + t)
    return jnp.where(lane == i1, w1, jnp.where(lane == i2, w2, 0.0))


def _moe_dense_kernel(h_ref, g_ref, wr_ref, br_ref, wg_ref, wu_ref, wd_ref, o_ref, xn_ref, comb_ref, acc_ref):
    e = pl.program_id(1)

    @pl.when(e == 0)
    def _():
        xn = _rms(h_ref[...], g_ref[...])
        comb_ref[...] = _route(_dot_hi(xn, wr_ref[...]) + br_ref[...])
        xn_ref[...] = xn.astype(BF16)
        acc_ref[...] = jnp.zeros_like(acc_ref)

    xn = xn_ref[...]
    hg = _dot(xn, wg_ref[0])
    hu = _dot(xn, wu_ref[0])
    comb = comb_ref[...]
    lane = lax.broadcasted_iota(jnp.int32, comb.shape, 1)
    c = jnp.sum(jnp.where(lane == e, comb, 0.0), axis=-1, keepdims=True)
    hid = (_silu(hg) * hu * c).astype(BF16)
    acc_ref[...] += _dot(hid, wd_ref[0])

    @pl.when(e == pl.num_programs(1) - 1)
    def _():
        o_ref[...] = h_ref[...] + acc_ref[...]


def moe_dense(h, g, w_router, b_router, w_gate, w_up, w_down, *, tm=1024):
    T, D = h.shape
    E, _, Fd = w_gate.shape
    tm = min(tm, T)
    assert T % tm == 0
    row = pl.BlockSpec((tm, D), lambda i, e: (i, 0))
    return pl.pallas_call(
        _moe_dense_kernel,
        grid=(T // tm, E),
        in_specs=[row, pl.BlockSpec((1, D), lambda i, e: (0, 0)),
                  pl.BlockSpec((D, LANES), lambda i, e: (0, 0)),
                  pl.BlockSpec((1, LANES), lambda i, e: (0, 0)),
                  pl.BlockSpec((1, D, Fd), lambda i, e: (e, 0, 0)),
                  pl.BlockSpec((1, D, Fd), lambda i, e: (e, 0, 0)),
                  pl.BlockSpec((1, Fd, D), lambda i, e: (e, 0, 0))],
        out_specs=row,
        out_shape=jax.ShapeDtypeStruct((T, D), F32),
        scratch_shapes=[pltpu.VMEM((tm, D), BF16), pltpu.VMEM((tm, LANES), F32), pltpu.VMEM((tm, D), F32)],
        compiler_params=_cparams(("parallel", "arbitrary")),
        name="moe_dense",
    )(h, g.reshape(1, D), w_router, b_router, w_gate, w_up, w_down)


MOE_PAIRS = MOE_EPG * (MOE_EPG - 1) // 2
MOE_CLASSES = MOE_GROUPS * MOE_PAIRS


def _route_select(logits):
    lane = lax.broadcasted_iota(jnp.int32, logits.shape, 1)
    big = jnp.int32(1 << 20)
    is_g = (lane >= MOE_E) & (lane < MOE_E + MOE_GROUPS)
    glog = jnp.where(is_g, logits, NEG)
    gmax = jnp.max(glog, axis=-1, keepdims=True)
    gsel = jnp.min(jnp.where(is_g & (glog == gmax), lane, big), axis=-1, keepdims=True) - MOE_E
    in_g = (lane < MOE_E) & ((lane // MOE_EPG) == gsel)
    e1 = jnp.where(in_g, logits, NEG)
    v1 = jnp.max(e1, axis=-1, keepdims=True)
    i1 = jnp.min(jnp.where(in_g & (e1 == v1), lane, big), axis=-1, keepdims=True)
    in_g2 = in_g & (lane != i1)
    e2 = jnp.where(in_g2, logits, NEG)
    v2 = jnp.max(e2, axis=-1, keepdims=True)
    i2 = jnp.min(jnp.where(in_g2 & (e2 == v2), lane, big), axis=-1, keepdims=True)
    return gsel, i1, i2


def _dot_x3(x, w):
    xh, wh = x.astype(BF16), w.astype(BF16)
    xl, wl = (x - xh.astype(F32)).astype(BF16), (w - wh.astype(F32)).astype(BF16)
    return _dot(xh, wh) + _dot(xl, wh) + _dot(xh, wl)


def _moe_class(h, g_ref, wr_ref, br_ref):
    gsel, i1, i2 = _route_select(_dot_x3(_rms(h, g_ref[...]), wr_ref[...]) + br_ref[...])
    a = jnp.minimum(i1, i2) - gsel * MOE_EPG
    b = jnp.maximum(i1, i2) - gsel * MOE_EPG
    return gsel * MOE_PAIRS + (a * (2 * MOE_EPG - 1 - a)) // 2 + (b - a - 1)


def _moe_plan(cls, tr):
    T = cls.shape[0]
    n_tiles = T // tr + MOE_CLASSES
    order = jnp.argsort(cls, stable=True).astype(jnp.int32)
    counts = jnp.sum(cls[:, None] == jnp.arange(MOE_CLASSES, dtype=jnp.int32)[None, :], axis=0, dtype=jnp.int32)
    tiles_per = (counts + tr - 1) // tr
    tile_end = jnp.cumsum(tiles_per)
    tile_ids = jnp.arange(n_tiles, dtype=jnp.int32)
    tcls = jnp.minimum(jnp.sum(tile_ids[:, None] >= tile_end[None, :], axis=1, dtype=jnp.int32), MOE_CLASSES - 1)
    k = tile_ids - (tile_end - tiles_per)[tcls]
    n_rows = jnp.where(tile_ids < tile_end[-1], jnp.clip(counts[tcls] - k * tr, 0, tr), 0).astype(jnp.int32)
    first = (jnp.cumsum(counts) - counts)[tcls] + k * tr
    rows = jnp.arange(tr, dtype=jnp.int32)[None, :]
    src = jnp.where(rows < n_rows[:, None], order[jnp.clip(first[:, None] + rows, 0, T - 1)], 0).astype(jnp.int32)
    pair = tcls % MOE_PAIRS
    grp = tcls // MOE_PAIRS
    pa = jnp.array([a for a in range(MOE_EPG) for b in range(a + 1, MOE_EPG)], jnp.int32)[pair]
    pb = jnp.array([b for a in range(MOE_EPG) for b in range(a + 1, MOE_EPG)], jnp.int32)[pair]
    return grp * MOE_EPG + pa, grp * MOE_EPG + pb, grp, n_rows, src


def _moe_routed_kernel(e1_ref, e2_ref, grp_ref, nrow_ref, prev_ref, src_ref, next_ref, h_ref, g_ref, wr_ref, br_ref,
                       wg1_ref, wu1_ref, wd1_ref, wg2_ref, wu2_ref, wd2_ref, o_ref, x_ref, y_ref, sem_in, sem_out):
    i = pl.program_id(0)
    last = pl.num_programs(0) - 1
    tr = x_ref.shape[1]
    n_rows = nrow_ref[i]
    n_prev = jnp.where(i > 0, nrow_ref[jnp.maximum(i - 1, 0)], 0)

    def row_in(idx_ref, s, r):
        return pltpu.make_async_copy(h_ref.at[pl.ds(idx_ref[0, 0, r], 1)], x_ref.at[s, pl.ds(r, 1)], sem_in.at[s])

    def row_out(idx_ref, s, r):
        return pltpu.make_async_copy(y_ref.at[s, pl.ds(r, 1)], o_ref.at[pl.ds(idx_ref[0, 0, r], 1)], sem_out.at[s])

    def each(n, fn):
        def body8(j, carry):
            for u in range(SUBLANES):
                fn(j * SUBLANES + u)
            return carry

        def body1(r, carry):
            fn(r)
            return carry

        whole = n // SUBLANES
        lax.fori_loop(0, whole, body8, 0)
        if not isinstance(n, int) or n % SUBLANES:
            lax.fori_loop(whole * SUBLANES, n, body1, 0)

    refs = (e1_ref, e2_ref, grp_ref, prev_ref, src_ref, next_ref, g_ref, wr_ref, br_ref,
            wg1_ref, wu1_ref, wd1_ref, wg2_ref, wu2_ref, wd2_ref, x_ref, y_ref)
    for slot in range(2):
        pl.when(i % 2 == slot)(functools.partial(
            _moe_routed_step, slot, i, last, tr, n_rows, n_prev, row_in, row_out, each, refs))


def _moe_routed_step(slot, i, last, tr, n_rows, n_prev, row_in, row_out, each, refs):
    (e1_ref, e2_ref, grp_ref, prev_ref, src_ref, next_ref, g_ref, wr_ref, br_ref,
     wg1_ref, wu1_ref, wd1_ref, wg2_ref, wu2_ref, wd2_ref, x_ref, y_ref) = refs

    @pl.when((i == 0) & (n_rows > 0))
    def _():
        each(tr, lambda r: row_in(src_ref, slot, r).start())

    @pl.when(((i == 0) & (n_rows > 0)) | (n_prev > 0))
    def _():
        each(tr, lambda r: row_in(src_ref, slot, r).wait())

    @pl.when(n_rows > 0)
    def _():
        for r in range(tr):
            row_in(next_ref, 1 - slot, r).start()
        x = x_ref[slot]
        xn = _rms(x, g_ref[...])
        logits = _dot_x3(xn, wr_ref[...]) + br_ref[...]
        lane = lax.broadcasted_iota(jnp.int32, logits.shape, 1)

        def pick(idx):
            return jnp.sum(jnp.where(lane == idx, logits, 0.0), axis=-1, keepdims=True)

        la, lb, lg = pick(e1_ref[i]), pick(e2_ref[i]), pick(MOE_E + grp_ref[i])
        is_g = (lane >= MOE_E) & (lane < MOE_E + MOE_GROUPS)
        gprob = 1.0 / jnp.sum(jnp.where(is_g, jnp.exp(logits - lg), 0.0), axis=-1, keepdims=True)
        top = jnp.maximum(la, lb)
        pa, pb = jnp.exp(la - top), jnp.exp(lb - top)
        scale = gprob / (pa + pb)
        xb = xn.astype(BF16)
        hid1 = (_silu(_dot(xb, wg1_ref[0])) * _dot(xb, wu1_ref[0]) * (pa * scale)).astype(BF16)
        hid2 = (_silu(_dot(xb, wg2_ref[0])) * _dot(xb, wu2_ref[0]) * (pb * scale)).astype(BF16)
        y_ref[slot] = x + _dot(hid1, wd1_ref[0]) + _dot(hid2, wd2_ref[0])

    each(n_prev, lambda r: row_out(prev_ref, 1 - slot, r).wait())
    each(n_rows, lambda r: row_out(src_ref, slot, r).start())

    @pl.when(i == last)
    def _():
        each(n_rows, lambda r: row_out(src_ref, slot, r).wait())


def moe_routed(h, cls, g, w_router, b_router, w_gate, w_up, w_down, *, tr=256):
    T, D = h.shape
    E, _, Fd = w_gate.shape
    assert T % tr == 0
    e1, e2, grp, n_rows, src = _moe_plan(cls, tr)
    n_tiles = src.shape[0]

    def wspec(shape, which):
        return pl.BlockSpec((1,) + shape, lambda i, e1, e2, grp, nr: ((e1, e2)[which][i], 0, 0))

    const = lambda shape: pl.BlockSpec(shape, lambda i, e1, e2, grp, nr: (0,) * len(shape))

    def rows_of(shift):
        return pl.BlockSpec((1, 1, tr), lambda i, e1, e2, grp, nr: (jnp.clip(i + shift, 0, n_tiles - 1), 0, 0),
                            memory_space=pltpu.SMEM)

    grid_spec = pltpu.PrefetchScalarGridSpec(
        num_scalar_prefetch=4, grid=(n_tiles,),
        in_specs=[rows_of(-1), rows_of(0), rows_of(1),
                  pl.BlockSpec(memory_space=pl.ANY), const((1, D)), const((D, LANES)), const((1, LANES)),
                  wspec((D, Fd), 0), wspec((D, Fd), 0), wspec((Fd, D), 0),
                  wspec((D, Fd), 1), wspec((D, Fd), 1), wspec((Fd, D), 1)],
        out_specs=pl.BlockSpec(memory_space=pl.ANY),
        scratch_shapes=[pltpu.VMEM((2, tr, D), F32), pltpu.VMEM((2, tr, D), F32),
                        pltpu.SemaphoreType.DMA((2,)), pltpu.SemaphoreType.DMA((2,))])
    src3 = src.reshape(n_tiles, 1, tr)
    return pl.pallas_call(
        _moe_routed_kernel,
        grid_spec=grid_spec,
        out_shape=jax.ShapeDtypeStruct((T, D), F32),
        compiler_params=_cparams(("arbitrary",)),
        name="moe_routed",
    )(e1, e2, grp, n_rows, src3, src3, src3, h, g.reshape(1, D), w_router, b_router,
      w_gate, w_up, w_down, w_gate, w_up, w_down)


def _conv_tile(xpad_ref, x, cw_ref, cb_ref, first):
    tl = x.shape[0]

    @pl.when(first)
    def _():
        xpad_ref[0:SUBLANES, :] = jnp.zeros((SUBLANES, x.shape[1]), F32)

    xpad_ref[SUBLANES:SUBLANES + tl, :] = x
    out = cb_ref[...] + x * cw_ref[CONV_W - 1:CONV_W, :]
    for k in range(CONV_W - 1):
        lo = SUBLANES - (CONV_W - 1) + k
        out = out + xpad_ref[lo:lo + tl, :] * cw_ref[k:k + 1, :]
    tail = xpad_ref[tl:tl + SUBLANES, :]
    xpad_ref[0:SUBLANES, :] = tail
    return out, tail[SUBLANES - (CONV_W - 1):, :]


def _lru_gates(xc, wa_ref, ba_ref, wi_ref, bi_ref, lam_ref):
    xcb = xc.astype(BF16)
    nb = wa_ref.shape[0]
    cw = wa_ref.shape[1]
    ra = jnp.concatenate([_dot(xcb[:, c * cw:(c + 1) * cw], wa_ref[c]) for c in range(nb)], axis=1)
    ia = jnp.concatenate([_dot(xcb[:, c * cw:(c + 1) * cw], wi_ref[c]) for c in range(nb)], axis=1)
    r = _sigmoid(ra + ba_ref[...])
    ig = _sigmoid(ia + bi_ref[...])
    log_a = (-LRU_C) * r * _softplus(-lam_ref[...])
    a = jnp.exp(log_a)
    b = jnp.sqrt(-jnp.tanh(log_a) * (a * a + 1.0)) * (ig * xc)
    return a, b


def _lru_seq_kernel(xb_ref, yb_ref, cw_ref, cb_ref, wa_ref, ba_ref, wi_ref, bi_ref, lam_ref,
                    o_ref, buf_ref, hl_ref, xpad_ref, a_ref, b_ref, hc_ref):
    t = pl.program_id(1)
    tl, W = xb_ref.shape[1], xb_ref.shape[2]
    xc, tail = _conv_tile(xpad_ref, xb_ref[0], cw_ref, cb_ref, t == 0)
    a, b = _lru_gates(xc, wa_ref, ba_ref, wi_ref, bi_ref, lam_ref)
    a_ref[...] = a
    b_ref[...] = b

    @pl.when(t == 0)
    def _():
        hc_ref[...] = jnp.zeros_like(hc_ref)

    row = lax.broadcasted_iota(jnp.int32, (SUBLANES, W), 0)

    def body(i, hc):
        r0 = pl.multiple_of(i * SUBLANES, SUBLANES)
        av = a_ref[pl.ds(r0, SUBLANES), :]
        bv = b_ref[pl.ds(r0, SUBLANES), :]
        for s in (1, 2, 4):
            keep = row >= s
            a_sh = pltpu.roll(av, s, 0)
            b_sh = pltpu.roll(bv, s, 0)
            bv = jnp.where(keep, av * b_sh + bv, bv)
            av = jnp.where(keep, av * a_sh, av)
        h = av * hc + bv
        b_ref[pl.ds(r0, SUBLANES), :] = h
        return jnp.broadcast_to(h[SUBLANES - 1:SUBLANES, :], (SUBLANES, W))

    hc = lax.fori_loop(0, tl // SUBLANES, body, hc_ref[...])
    hc_ref[...] = hc
    o_ref[0] = (b_ref[...] * _gelu_tanh(yb_ref[0])).astype(o_ref.dtype)
    buf_ref[0] = tail
    hl_ref[0] = hc[0:1, :]


def lru_seq(xy, conv_w, conv_b, wa_bd, b_a, wi_bd, b_i, lam, *, tl=512):
    B, L, W2 = xy.shape
    W = W2 // 2
    tl = min(tl, L)
    assert L % tl == 0 and tl % SUBLANES == 0
    vec = pl.BlockSpec((1, W), lambda b, t: (0, 0))
    wbd = pl.BlockSpec(wa_bd.shape, lambda b, t: (0, 0, 0))
    return pl.pallas_call(
        _lru_seq_kernel,
        grid=(B, L // tl),
        in_specs=[pl.BlockSpec((1, tl, W), lambda b, t: (b, t, 0)),
                  pl.BlockSpec((1, tl, W), lambda b, t: (b, t, 1)),
                  pl.BlockSpec((CONV_W, W), lambda b, t: (0, 0)), vec, wbd, vec, wbd, vec, vec],
        out_specs=(pl.BlockSpec((1, tl, W), lambda b, t: (b, t, 0)),
                   pl.BlockSpec((1, CONV_W - 1, W), lambda b, t: (b, 0, 0)),
                   pl.BlockSpec((1, 1, W), lambda b, t: (b, 0, 0))),
        out_shape=(jax.ShapeDtypeStruct((B, L, W), BF16),
                   jax.ShapeDtypeStruct((B, CONV_W - 1, W), F32),
                   jax.ShapeDtypeStruct((B, 1, W), F32)),
        scratch_shapes=[pltpu.VMEM((tl + SUBLANES, W), F32), pltpu.VMEM((tl, W), F32),
                        pltpu.VMEM((tl, W), F32), pltpu.VMEM((SUBLANES, W), F32)],
        compiler_params=_cparams(("parallel", "arbitrary")),
        name="lru_seq",
    )(xy, xy, conv_w, conv_b.reshape(1, W), wa_bd, b_a.reshape(1, W), wi_bd, b_i.reshape(1, W), lam.reshape(1, W))


def _lru_step_kernel(xb_ref, yb_ref, buf_ref, h0_ref, cw_ref, cb_ref, wa_ref, ba_ref, wi_ref, bi_ref, lam_ref,
                     o_ref, nbuf_ref, hl_ref):
    x = xb_ref[...]
    xc = cb_ref[...] + x * cw_ref[CONV_W - 1:CONV_W, :]
    for k in range(CONV_W - 1):
        xc = xc + buf_ref[k] * cw_ref[k:k + 1, :]
    a, b = _lru_gates(xc, wa_ref, ba_ref, wi_ref, bi_ref, lam_ref)
    h = a * h0_ref[...] + b
    o_ref[...] = (h * _gelu_tanh(yb_ref[...])).astype(o_ref.dtype)
    hl_ref[...] = h
    for k in range(CONV_W - 2):
        nbuf_ref[k] = buf_ref[k + 1]
    nbuf_ref[CONV_W - 2] = x


def lru_step(xy, buf_t, h0, conv_w, conv_b, wa_bd, b_a, wi_bd, b_i, lam):
    B, W2 = xy.shape
    W = W2 // 2
    vec = pl.BlockSpec((1, W), lambda i: (0, 0))
    wbd = pl.BlockSpec(wa_bd.shape, lambda i: (0, 0, 0))
    mat = pl.BlockSpec((B, W), lambda i: (0, 0))
    cube = pl.BlockSpec((CONV_W - 1, B, W), lambda i: (0, 0, 0))
    return pl.pallas_call(
        _lru_step_kernel,
        grid=(1,),
        in_specs=[mat, pl.BlockSpec((B, W), lambda i: (0, 1)), cube, mat,
                  pl.BlockSpec((CONV_W, W), lambda i: (0, 0)), vec, wbd, vec, wbd, vec, vec],
        out_specs=(mat, cube, mat),
        out_shape=(jax.ShapeDtypeStruct((B, W), BF16),
                   jax.ShapeDtypeStruct((CONV_W - 1, B, W), F32),
                   jax.ShapeDtypeStruct((B, W), F32)),
        compiler_params=_cparams(("arbitrary",)),
        name="lru_step",
    )(xy, xy, buf_t, h0, conv_w, conv_b.reshape(1, W), wa_bd, b_a.reshape(1, W), wi_bd, b_i.reshape(1, W),
      lam.reshape(1, W))


def _group_norm_gate(y, z, ng, n_groups):
    y = y * _silu(z)
    gw = y.shape[1] // n_groups
    outs = []
    for g in range(n_groups):
        yg = y[:, g * gw:(g + 1) * gw]
        ms = jnp.mean(yg * yg, axis=-1, keepdims=True)
        outs.append(yg * lax.rsqrt(ms + EPS))
    return jnp.concatenate(outs, axis=1) * ng


def _ssd_seq_kernel(z_ref, xbc_ref, dt_ref, cw_ref, cb_ref, dtb_ref, alog_ref, dsk_ref, ng_ref, ex_ref,
                    o_ref, buf_ref, st_ref, xpad_ref, s_ref):
    t = pl.program_id(1)
    Q = xbc_ref.shape[1]
    DI = z_ref.shape[2]
    GN = SSM_G * SSM_N
    RP = DI // SSM_G
    xc, tail = _conv_tile(xpad_ref, xbc_ref[0], cw_ref, cb_ref, t == 0)
    xc = _silu(xc)
    xs = xc[:, :DI]
    bm = xc[:, DI:DI + GN]
    cm = xc[:, DI + GN:]

    @pl.when(t == 0)
    def _():
        s_ref[...] = jnp.zeros_like(s_ref)

    dt = _softplus(dt_ref[0] + dtb_ref[...])
    a_neg = -jnp.exp(alog_ref[...])
    ri = lax.broadcasted_iota(jnp.int32, (Q, Q), 0)
    ci = lax.broadcasted_iota(jnp.int32, (Q, Q), 1)
    tri = ci <= ri
    acum = _dot_hi(tri.astype(F32), dt * a_neg)
    acum_t = acum.T
    ex = ex_ref[...]
    dt_e = _place_dot(dt, ex)
    acum_e = _place_dot(acum, ex)
    last_e = acum_e[Q - 1:Q, :]
    xdt = xs * dt_e
    xdtw = (xdt * jnp.exp(last_e - acum_e)).astype(BF16)
    xdt = xdt.astype(BF16)
    eacum = jnp.exp(acum_e)
    edec = jnp.exp(last_e)
    lane = lax.broadcasted_iota(jnp.int32, (Q, LANES), 1)
    lo = lane < SSM_P
    ys = []
    for g in range(SSM_G):
        cg = cm[:, g * SSM_N:(g + 1) * SSM_N].astype(BF16)
        bg32 = bm[:, g * SSM_N:(g + 1) * SSM_N]
        bg = bg32.astype(BF16)
        cb = _dot_nt(cg, bg)
        st = s_ref[g]
        yoff = _dot(cg, st.astype(BF16)) * eacum[:, g * RP:(g + 1) * RP]
        for pr in range(RP // LANES):
            ms = []
            for k in range(LANES // SSM_P):
                hd = (g * RP + pr * LANES) // SSM_P + k
                seg = acum[:, hd:hd + 1] - acum_t[hd:hd + 1, :]
                ms.append((cb * jnp.exp(jnp.where(tri, seg, NEG))).astype(BF16))
            c0 = g * RP + pr * LANES
            xp = xdt[:, c0:c0 + LANES]
            zero = jnp.zeros_like(xp)
            rhs = jnp.concatenate([jnp.where(lo, xp, zero), jnp.where(lo, zero, xp)], axis=0)
            ydiag = _dot(jnp.concatenate(ms, axis=1), rhs)
            ys.append(ydiag + yoff[:, pr * LANES:(pr + 1) * LANES])
        new = st * edec[:, g * RP:(g + 1) * RP] + _dot(bg32.T.astype(BF16), xdtw[:, g * RP:(g + 1) * RP])
        s_ref[g] = new
    y = jnp.concatenate(ys, axis=1) + xs * dsk_ref[...]
    o_ref[0] = _group_norm_gate(y, z_ref[0], ng_ref[...], SSM_G).astype(o_ref.dtype)
    buf_ref[0] = tail

    @pl.when(t == pl.num_programs(1) - 1)
    def _():
        hpg = RP // SSM_P
        for g in range(SSM_G):
            st_ref[0, g * hpg:(g + 1) * hpg] = s_ref[g].T.reshape(hpg, SSM_P, SSM_N)


def ssd_seq(z, xbc, dt, conv_w, conv_b, dt_bias, a_log, d_lanes, norm_g, expand, *, q=128):
    B, L, DI = z.shape
    C = xbc.shape[2]
    H = DI // SSM_P
    q = min(q, L)
    assert L % q == 0
    vecd = pl.BlockSpec((1, DI), lambda b, t: (0, 0))
    vecl = pl.BlockSpec((1, LANES), lambda b, t: (0, 0))
    return pl.pallas_call(
        _ssd_seq_kernel,
        grid=(B, L // q),
        in_specs=[pl.BlockSpec((1, q, DI), lambda b, t: (b, t, 0)),
                  pl.BlockSpec((1, q, C), lambda b, t: (b, t, 0)),
                  pl.BlockSpec((1, q, LANES), lambda b, t: (b, t, 0)),
                  pl.BlockSpec((CONV_W, C), lambda b, t: (0, 0)),
                  pl.BlockSpec((1, C), lambda b, t: (0, 0)),
                  vecl, vecl, vecd, vecd,
                  pl.BlockSpec((LANES, DI), lambda b, t: (0, 0))],
        out_specs=(pl.BlockSpec((1, q, DI), lambda b, t: (b, t, 0)),
                   pl.BlockSpec((1, CONV_W - 1, C), lambda b, t: (b, 0, 0)),
                   pl.BlockSpec((1, H, SSM_P, SSM_N), lambda b, t: (b, 0, 0, 0))),
        out_shape=(jax.ShapeDtypeStruct((B, L, DI), BF16),
                   jax.ShapeDtypeStruct((B, CONV_W - 1, C), F32),
                   jax.ShapeDtypeStruct((B, H, SSM_P, SSM_N), F32)),
        scratch_shapes=[pltpu.VMEM((q + SUBLANES, C), F32),
                        pltpu.VMEM((SSM_G, SSM_N, DI // SSM_G), F32)],
        compiler_params=_cparams(("parallel", "arbitrary")),
        name="ssd_seq",
    )(z, xbc, dt, conv_w, conv_b.reshape(1, C), dt_bias, a_log, d_lanes, norm_g.reshape(1, DI), expand)


def _ssd_step_kernel(z_ref, xbc_ref, dt_ref, buf_ref, h0_ref, cw_ref, cb_ref, dtb_ref, alog_ref, dsk_ref, ng_ref,
                     ex_ref, o_ref, nbuf_ref, hn_ref):
    DI = z_ref.shape[2]
    GN = SSM_G * SSM_N
    hpg = DI // SSM_P // SSM_G
    x = xbc_ref[0]
    buf = buf_ref[0]
    xc = cb_ref[...] + x * cw_ref[CONV_W - 1:CONV_W, :]
    for k in range(CONV_W - 1):
        xc = xc + buf[k:k + 1, :] * cw_ref[k:k + 1, :]
    nbuf_ref[0, 0:CONV_W - 2, :] = buf[1:CONV_W - 1, :]
    nbuf_ref[0, CONV_W - 2:CONV_W - 1, :] = x
    xc = _silu(xc)
    xs = xc[:, :DI]
    dt = _softplus(dt_ref[0] + dtb_ref[...])
    dta = dt * (-jnp.exp(alog_ref[...]))
    ex = ex_ref[...]
    dec = jnp.exp(dta)
    dt_e = _place_dot(jnp.broadcast_to(dt, (SUBLANES, LANES)), ex)[0:1, :]
    xdt = xs * dt_e
    RP = hpg * SSM_P
    eye = (lax.broadcasted_iota(jnp.int32, (RP, RP), 0) == lax.broadcasted_iota(jnp.int32, (RP, RP), 1))
    ys = []
    for g in range(SSM_G):
        brow = xc[:, DI + g * SSM_N:DI + (g + 1) * SSM_N]
        crow = xc[:, DI + GN + g * SSM_N:DI + GN + (g + 1) * SSM_N]
        xg = xdt[:, g * RP:(g + 1) * RP]
        xcol = jnp.sum(jnp.where(eye, jnp.broadcast_to(xg, (RP, RP)), 0.0), axis=-1, keepdims=True)
        news = []
        for r in range(hpg):
            hd = g * hpg + r
            new = h0_ref[0, hd] * dec[:, hd:hd + 1] + xcol[r * SSM_P:(r + 1) * SSM_P, :] * brow
            hn_ref[0, hd] = new
            news.append(new)
        new_g = jnp.concatenate(news, axis=0).astype(BF16)
        ys.append(_dot_nt(jnp.broadcast_to(crow, (SUBLANES, SSM_N)).astype(BF16), new_g)[0:1, :])
    y = jnp.concatenate(ys, axis=1) + xs * dsk_ref[...]
    o_ref[0] = _group_norm_gate(y, z_ref[0], ng_ref[...], SSM_G).astype(o_ref.dtype)


def ssd_step(z, xbc, dt, buf, h0, conv_w, conv_b, dt_bias, a_log, d_lanes, norm_g, expand):
    B, _, DI = z.shape
    C = xbc.shape[2]
    H = DI // SSM_P
    vecd = pl.BlockSpec((1, DI), lambda b: (0, 0))
    vecl = pl.BlockSpec((1, LANES), lambda b: (0, 0))
    st = pl.BlockSpec((1, H, SSM_P, SSM_N), lambda b: (b, 0, 0, 0))
    return pl.pallas_call(
        _ssd_step_kernel,
        grid=(B,),
        in_specs=[pl.BlockSpec((1, 1, DI), lambda b: (b, 0, 0)),
                  pl.BlockSpec((1, 1, C), lambda b: (b, 0, 0)),
                  pl.BlockSpec((1, 1, LANES), lambda b: (b, 0, 0)),
                  pl.BlockSpec((1, CONV_W - 1, C), lambda b: (b, 0, 0)), st,
                  pl.BlockSpec((CONV_W, C), lambda b: (0, 0)),
                  pl.BlockSpec((1, C), lambda b: (0, 0)),
                  vecl, vecl, vecd, vecd,
                  pl.BlockSpec((LANES, DI), lambda b: (0, 0))],
        out_specs=(pl.BlockSpec((1, 1, DI), lambda b: (b, 0, 0)),
                   pl.BlockSpec((1, CONV_W - 1, C), lambda b: (b, 0, 0)), st),
        out_shape=(jax.ShapeDtypeStruct((B, 1, DI), BF16),
                   jax.ShapeDtypeStruct((B, CONV_W - 1, C), F32),
                   jax.ShapeDtypeStruct((B, H, SSM_P, SSM_N), F32)),
        compiler_params=_cparams(("parallel",)),
        name="ssd_step",
    )(z, xbc, dt, buf, h0, conv_w, conv_b.reshape(1, C), dt_bias, a_log, d_lanes, norm_g.reshape(1, DI), expand)


def _split_dot(x, m):
    hi = x.astype(BF16)
    lo = (x - hi.astype(F32)).astype(BF16)
    return _dot(hi, m) + _dot(lo, m)


LOG2E = 1.4426950408889634
N_PIECES = 3
PAIR = LANES // ATT_HD


def _fox_prep_kernel(q_ref, k_ref, v_ref, fl_ref, bf_ref, qg_ref, kg_ref, hs_ref, he_ref, pl_ref,
                     qo_ref, ko_ref, lf_ref, kx_ref, vt_ref, kt32_ref, vt32_ref, carry_ref):
    t = pl.program_id(1)
    tl = q_ref.shape[1]
    D = q_ref.shape[2]

    def head_norm(x, g):
        ss = _split_dot(x * x, hs_ref[...])
        inv = lax.rsqrt(ss * (1.0 / ATT_HD) + EPS)
        return x * _split_dot(inv, he_ref[...]) * g

    qo_ref[0] = (head_norm(q_ref[0], qg_ref[...]) * (ATT_HD ** -0.5 * LOG2E)).astype(qo_ref.dtype)
    kn = head_norm(k_ref[0], kg_ref[...])
    ko_ref[0] = kn
    kt32_ref[0] = kn.T
    vt = v_ref[0].T
    vt32_ref[0] = vt
    vt_ref[0] = vt.astype(BF16)
    z = fl_ref[0] + bf_ref[...]
    logf = jnp.minimum(z, 0.0) - jnp.log1p(jnp.exp(-jnp.abs(z)))
    lf_ref[0] = logf

    @pl.when(t == 0)
    def _():
        carry_ref[...] = jnp.zeros_like(carry_ref)

    ri = lax.broadcasted_iota(jnp.int32, (tl, tl), 0)
    ci = lax.broadcasted_iota(jnp.int32, (tl, tl), 1)
    c = _dot_hi((ci <= ri).astype(F32), logf) + carry_ref[0:1, :]
    carry_ref[...] = jnp.broadcast_to(c[tl - 1:tl, :], carry_ref.shape)
    rest = c * (-LOG2E)
    extra = jnp.zeros((tl, D), F32)
    for j in range(N_PIECES):
        piece = rest.astype(BF16)
        rest = rest - piece.astype(F32)
        extra = extra + _dot(piece, pl_ref[j])
    knb = kn.astype(BF16)
    extra = extra.astype(BF16)
    kx_ref[0] = jnp.concatenate(
        [x[:, p * LANES:(p + 1) * LANES] for p in range(D // LANES) for x in (knb, extra)], axis=1)


def fox_prep(q, k, v, fl, b_f, q_g, k_g, head_sum, head_expand, place, *, tl=256):
    B, L, D = q.shape
    tl = min(tl, L)
    assert L % tl == 0
    row = pl.BlockSpec((1, tl, D), lambda b, t: (b, t, 0))
    colt = pl.BlockSpec((1, D, tl), lambda b, t: (b, 0, t))
    nar = pl.BlockSpec((1, tl, LANES), lambda b, t: (b, t, 0))
    vecd = pl.BlockSpec((1, D), lambda b, t: (0, 0))
    return pl.pallas_call(
        _fox_prep_kernel,
        grid=(B, L // tl),
        in_specs=[row, row, row, nar, pl.BlockSpec((1, LANES), lambda b, t: (0, 0)), vecd, vecd,
                  pl.BlockSpec((D, LANES), lambda b, t: (0, 0)),
                  pl.BlockSpec((LANES, D), lambda b, t: (0, 0)),
                  pl.BlockSpec((N_PIECES, LANES, D), lambda b, t: (0, 0, 0))],
        out_specs=(row, row, nar, pl.BlockSpec((1, tl, 2 * D), lambda b, t: (b, t, 0)),
                   colt, colt, colt),
        out_shape=(jax.ShapeDtypeStruct((B, L, D), BF16),
                   jax.ShapeDtypeStruct((B, L, D), F32),
                   jax.ShapeDtypeStruct((B, L, LANES), F32),
                   jax.ShapeDtypeStruct((B, L, 2 * D), BF16),
                   jax.ShapeDtypeStruct((B, D, L), BF16),
                   jax.ShapeDtypeStruct((B, D, L), F32),
                   jax.ShapeDtypeStruct((B, D, L), F32)),
        scratch_shapes=[pltpu.VMEM((SUBLANES, LANES), F32)],
        compiler_params=_cparams(("parallel", "arbitrary")),
        name="fox_prep",
    )(q, k, v, fl, b_f, q_g, k_g, head_sum, head_expand, place)


def _fox_attn_kernel(q_ref, kx_ref, vt_ref, g_ref, o_ref, qx_ref, m_ref, l_ref, acc_ref, s_ref):
    qi = pl.program_id(2)
    tq = q_ref.shape[1]
    q = q_ref[0].astype(F32)
    lane = lax.broadcasted_iota(jnp.int32, (tq, LANES), 1)
    for k in range(PAIR):
        mine = (lane >= k * ATT_HD) & (lane < (k + 1) * ATT_HD)
        pick = (lane >= k * N_PIECES) & (lane < (k + 1) * N_PIECES)
        qx_ref[k] = jnp.concatenate([jnp.where(mine, q, 0.0), jnp.where(pick, 1.0, 0.0)], axis=1).astype(BF16)
    m_ref[...] = jnp.full_like(m_ref, NEG)
    l_ref[...] = jnp.zeros_like(l_ref)
    acc_ref[...] = jnp.zeros_like(acc_ref)

    def scores(j):
        r0 = pl.multiple_of(j * tq, tq)
        kb = kx_ref[0, pl.ds(r0, tq), :]
        return [_dot_nt(kb, qx_ref[k]) for k in range(PAIR)]

    def consume(j, diagonal):
        r0 = pl.multiple_of(j * tq, tq)
        vt = vt_ref[0, :, pl.ds(r0, tq)]
        for k in range(PAIR):
            s = s_ref[k]
            if diagonal:
                rows = lax.broadcasted_iota(jnp.int32, (tq, tq), 0)
                cols = lax.broadcasted_iota(jnp.int32, (tq, tq), 1)
                s = jnp.where(rows <= cols, s, NEG)
            m_old = m_ref[k]
            m_new = jnp.maximum(m_old, jnp.max(s, axis=0, keepdims=True))
            alpha = jnp.exp2(m_old - m_new)
            p = jnp.exp2(s - m_new)
            l_ref[k] = l_ref[k] * alpha + jnp.sum(p, axis=0, keepdims=True)
            m_ref[k] = m_new
            acc_ref[k] = acc_ref[k] * alpha + _dot(vt[k * ATT_HD:(k + 1) * ATT_HD, :], p.astype(BF16))

    def stash(s_list):
        for k in range(PAIR):
            s_ref[k] = s_list[k]

    stash(scores(0))

    def body(j, carry):
        nxt = scores(j + 1)
        consume(j, False)
        stash(nxt)
        return carry

    lax.fori_loop(0, qi, body, 0)
    consume(qi, True)
    o = jnp.concatenate([acc_ref[k] / l_ref[k] for k in range(PAIR)], axis=0)
    o_ref[0] = (o.T * _sigmoid(g_ref[0])).astype(o_ref.dtype)


def fox_attn(q, kx, vt, g, *, tq=512):
    B, L, D = q.shape
    tq = min(tq, L)
    assert L % tq == 0
    qs = pl.BlockSpec((1, tq, LANES), lambda b, p, i: (b, i, p))
    return pl.pallas_call(
        _fox_attn_kernel,
        grid=(B, D // LANES, L // tq),
        in_specs=[qs,
                  pl.BlockSpec((1, L, 2 * LANES), lambda b, p, i: (b, 0, p)),
                  pl.BlockSpec((1, LANES, L), lambda b, p, i: (b, p, 0)),
                  qs],
        out_specs=qs,
        out_shape=jax.ShapeDtypeStruct((B, L, D), BF16),
        scratch_shapes=[pltpu.VMEM((PAIR, tq, 2 * LANES), BF16), pltpu.VMEM((PAIR, 1, tq), F32),
                        pltpu.VMEM((PAIR, 1, tq), F32), pltpu.VMEM((PAIR, ATT_HD, tq), F32),
                        pltpu.VMEM((PAIR, tq, tq), F32)],
        compiler_params=_cparams(("parallel", "parallel", "arbitrary")),
        name="fox_attn",
    )(q, kx, vt, g)


def _lanes_to_sublanes(row):
    n = row.shape[1]
    eye = lax.broadcasted_iota(jnp.int32, (n, n), 0) == lax.broadcasted_iota(jnp.int32, (n, n), 1)
    return jnp.sum(jnp.where(eye, jnp.broadcast_to(row, (n, n)), 0.0), axis=1, keepdims=True)


def _sublanes_to_lanes(col):
    n = col.shape[0]
    eye = lax.broadcasted_iota(jnp.int32, (n, n), 0) == lax.broadcasted_iota(jnp.int32, (n, n), 1)
    return jnp.sum(jnp.where(eye, jnp.broadcast_to(col, (n, n)), 0.0), axis=0, keepdims=True)


def _fox_decode_kernel(pt_ref, q_ref, kn_ref, vn_ref, lfn_ref, g_ref, *rest, npg):
    k_refs = rest[:npg]
    v_refs = rest[npg:2 * npg]
    lf_refs = rest[2 * npg:3 * npg]
    o_ref = rest[3 * npg]
    qb_ref, m_ref, l_ref, acc_ref, coff_ref = rest[3 * npg + 1:]
    s_id = pl.program_id(1)
    H, hd, ps = k_refs[0].shape[1:]

    @pl.when(s_id == 0)
    def _():
        q = q_ref[0]
        for h in range(H):
            qb_ref[h] = jnp.broadcast_to(_lanes_to_sublanes(q[h:h + 1, :]), (hd, ps))
        m_ref[...] = jnp.full_like(m_ref, NEG)
        l_ref[...] = jnp.zeros_like(l_ref)
        acc_ref[...] = jnp.zeros_like(acc_ref)
        coff_ref[...] = jnp.zeros_like(coff_ref)

    ri = lax.broadcasted_iota(jnp.int32, (ps, ps), 0)
    ci = lax.broadcasted_iota(jnp.int32, (ps, ps), 1)
    upper = (ri <= ci).astype(F32)

    for i in range(npg):
        qk = jnp.concatenate([jnp.sum(k_refs[i][0, h] * qb_ref[h], axis=0, keepdims=True) for h in range(H)], axis=0)
        c = _dot_hi(lf_refs[i][0], upper) + coff_ref[...]
        coff_ref[...] = c[:, ps - 1:ps]
        s = qk - c * LOG2E
        m_old = m_ref[...]
        m_new = jnp.maximum(m_old, jnp.max(s, axis=1, keepdims=True))
        alpha = jnp.exp2(m_old - m_new)
        p = jnp.exp2(s - m_new)
        l_ref[...] = l_ref[...] * alpha + jnp.sum(p, axis=1, keepdims=True)
        m_ref[...] = m_new
        for h in range(H):
            acc_ref[h] = acc_ref[h] * alpha[h:h + 1, :] + p[h:h + 1, :] * v_refs[i][0, h]

    @pl.when(s_id == pl.num_programs(1) - 1)
    def _():
        s = (jnp.sum(q_ref[0] * kn_ref[0], axis=1, keepdims=True)
             - (coff_ref[...] + _lanes_to_sublanes(lfn_ref[0][:, 0:H])) * LOG2E)
        m_old = m_ref[...]
        m_new = jnp.maximum(m_old, s)
        alpha = jnp.exp2(m_old - m_new)
        p = jnp.exp2(s - m_new)
        l = l_ref[...] * alpha + p
        past = jnp.concatenate(
            [_sublanes_to_lanes(jnp.sum(acc_ref[h], axis=1, keepdims=True)) for h in range(H)], axis=0)
        o = (past * alpha + p * vn_ref[0]) / l
        o_ref[0] = (o * _sigmoid(g_ref[0])).astype(o_ref.dtype)


def fox_decode(q, k_new, v_new, lf_new, g, cache_kt, cache_vt, cache_lft, page_table, *, npg=16):
    B, H, hd = q.shape
    n_pages = page_table.shape[1]
    ps = cache_kt.shape[3]
    npg = min(npg, n_pages)
    assert n_pages % npg == 0
    steps = n_pages // npg
    row = pl.BlockSpec((1, H, hd), lambda b, s, pt: (b, 0, 0))

    def page(i, *tail):
        return pl.BlockSpec((1,) + tail, lambda b, s, pt: (pt[b * n_pages + s * npg + i],) + (0,) * len(tail))

    in_specs = ([row, row, row, pl.BlockSpec((1, 1, LANES), lambda b, s, pt: (b, 0, 0)), row]
                + [page(i, H, hd, ps) for i in range(npg)] + [page(i, H, hd, ps) for i in range(npg)]
                + [page(i, H, ps) for i in range(npg)])
    grid_spec = pltpu.PrefetchScalarGridSpec(
        num_scalar_prefetch=1, grid=(B, steps), in_specs=in_specs, out_specs=row,
        scratch_shapes=[pltpu.VMEM((H, hd, ps), F32), pltpu.VMEM((H, 1), F32), pltpu.VMEM((H, 1), F32),
                        pltpu.VMEM((H, hd, ps), F32), pltpu.VMEM((H, 1), F32)])
    return pl.pallas_call(
        functools.partial(_fox_decode_kernel, npg=npg),
        grid_spec=grid_spec,
        out_shape=jax.ShapeDtypeStruct((B, H, hd), BF16),
        compiler_params=_cparams(("parallel", "arbitrary")),
        name="fox_decode",
    )(page_table.reshape(-1), q, k_new, v_new, lf_new, g,
      *([cache_kt] * npg), *([cache_vt] * npg), *([cache_lft] * npg))


def _block_diag_chunks(w, per_chunk):
    nblk, bw, _ = w.shape
    w = w.reshape(nblk // per_chunk, per_chunk, bw, bw)
    eye = jnp.eye(per_chunk, dtype=w.dtype)
    out = jnp.einsum('cpij,pq->cpiqj', w, eye)
    return out.reshape(nblk // per_chunk, per_chunk * bw, per_chunk * bw)


def _head_expand(n_heads, width):
    r = jnp.arange(LANES)[:, None]
    c = jnp.arange(n_heads * width)[None, :] // width
    return (r == c).astype(F32)


def _pad_lanes(v):
    return jnp.pad(v.reshape(1, -1), ((0, 0), (0, LANES - v.size)))


MOE_ROUTED_MIN_TOKENS = 2048


def _moe_route(i, prm, n_tokens):
    if n_tokens < MOE_ROUTED_MIN_TOKENS:
        return None
    return prm['norm_ffn'][i], prm['moe_w_router'][i], prm['moe_b_router'][i]


def _moe_and_ple(h, p_i, i, prm, last):
    moe_args = (prm['norm_ffn'][i], prm['moe_w_router'][i], prm['moe_b_router'][i],
                prm['moe_w_gate'][i], prm['moe_w_up'][i], prm['moe_w_down'][i])
    if isinstance(h, tuple):
        h = moe_routed(h[0], h[1][:, 0], *moe_args)
    else:
        h = moe_dense(h, *moe_args)
    return ple(h, p_i, prm['norm_ple'][i], prm['ple_w_gate'][i], prm['ple_w_proj'][i],
               prm['norm_final'] if last else None)


def _lru_layer_prompt(h, B, L, j, g, prm, route=None):
    D = h.shape[1]
    xy = norm_matmul(h, g, prm['lru_w_in'][j])
    gated, buf, hl = lru_seq(xy.reshape(B, L, -1), prm['lru_conv_w'][j], prm['lru_conv_b'][j],
                             prm['lru_wa_bd'][j], prm['lru_b_a'][j], prm['lru_wi_bd'][j], prm['lru_b_i'][j],
                             prm['lru_lambda'][j])
    h = matmul_res(gated.reshape(B * L, -1), prm['lru_w_out'][j], h, route)
    return h, buf, hl.reshape(B, D)


def _lru_layer_sample(h, buf, h0, j, g, prm):
    xy = norm_matmul(h, g, prm['lru_w_in'][j])
    gated, nbuf, hl = lru_step(xy, jnp.swapaxes(buf, 0, 1), h0, prm['lru_conv_w'][j], prm['lru_conv_b'][j],
                               prm['lru_wa_bd'][j], prm['lru_b_a'][j], prm['lru_wi_bd'][j], prm['lru_b_i'][j],
                               prm['lru_lambda'][j])
    h = matmul_res(gated, prm['lru_w_out'][j], h)
    return h, jnp.swapaxes(nbuf, 0, 1), hl


def _ssd_proj(h, g, j, prm):
    z = norm_matmul(h, g, prm['ssm_w_z'][j])
    xbc = norm_matmul(h, g, prm['ssm_w_xbc'][j])
    dt = norm_matmul_hi(h, g, prm['ssm_w_dt'][j])
    return z, xbc, dt


def _ssd_args(j, prm):
    return (prm['ssm_conv_w'][j], prm['ssm_conv_b'][j], prm['ssm_dt_bias'][j], prm['ssm_a_log'][j],
            prm['ssm_d_lanes'][j], prm['ssm_norm'][j], prm['ssm_expand'])


def _ssd_layer_prompt(h, B, L, j, g, prm, route=None):
    z, xbc, dt = _ssd_proj(h, g, j, prm)
    y, buf, st = ssd_seq(z.reshape(B, L, -1), xbc.reshape(B, L, -1), dt.reshape(B, L, -1), *_ssd_args(j, prm))
    return matmul_res(y.reshape(B * L, -1), prm['ssm_w_out'][j], h, route), buf, st


def _ssd_layer_sample(h, buf, h0, j, g, prm):
    B = h.shape[0]
    z, xbc, dt = _ssd_proj(h, g, j, prm)
    y, nbuf, st = ssd_step(z.reshape(B, 1, -1), xbc.reshape(B, 1, -1), dt.reshape(B, 1, -1), buf, h0,
                           *_ssd_args(j, prm))
    return matmul_res(y.reshape(B, -1), prm['ssm_w_out'][j], h), nbuf, st


def _fox_proj(h, g, j, prm):
    q = norm_matmul(h, g, prm['fox_w_q'][j])
    k = norm_matmul(h, g, prm['fox_w_k'][j])
    v = norm_matmul(h, g, prm['fox_w_v'][j])
    og = norm_matmul(h, g, prm['fox_w_g'][j])
    fl = norm_matmul_hi(h, g, prm['fox_w_f'][j], exact=False)
    return q, k, v, og, fl


def _fox_prep_args(j, prm):
    return (prm['fox_b_f'][j], prm['fox_q_norm'][j], prm['fox_k_norm'][j], prm['fox_head_sum'],
            prm['fox_head_expand'], prm['fox_place'])


def _fox_layer_prompt(h, B, L, j, g, prm, route=None, tq=512):
    q, k, v, og, fl = _fox_proj(h, g, j, prm)
    shp = (B, L, -1)
    qs, _, logf, kx, vt, kt32, vt32 = fox_prep(q.reshape(shp), k.reshape(shp), v.reshape(shp), fl.reshape(shp),
                                               *_fox_prep_args(j, prm))
    o = fox_attn(qs, kx, vt, og.reshape(shp), tq=tq)

    def per_head(xt):
        return jnp.transpose(xt.reshape(B, -1, ATT_HD, L), (0, 3, 1, 2))

    return matmul_res(o.reshape(B * L, -1), prm['fox_w_out'][j], h, route), per_head(kt32), per_head(vt32), logf


def _fox_layer_sample(h, cache, n_phys, page_table, j, g, prm):
    B, D = h.shape
    q, k, v, og, fl = _fox_proj(h, g, j, prm)
    one = (1, B, -1)
    qs, kn, logf = fox_prep(q.reshape(one), k.reshape(one), v.reshape(one), fl.reshape(one),
                            *_fox_prep_args(j, prm))[:3]
    ck, cv, clf = cache
    heads = (B, D // ATT_HD, ATT_HD)
    tok = (B, 1, -1)
    o = fox_decode(qs.astype(F32).reshape(heads), kn.reshape(heads), v.reshape(heads), logf.reshape(tok),
                   og.reshape(heads), ck, cv, clf, page_table + j * n_phys)
    return matmul_res(o.reshape(B, D), prm['fox_w_out'][j], h), kn.reshape(tok), v.reshape(tok), logf.reshape(tok)


def _prepare_params(raw):
    prm = dict(raw)
    D = raw['norm_final'].shape[0]
    for name in ('lru_w_in', 'lru_w_out', 'ssm_w_out', 'fox_w_out', 'moe_w_gate', 'moe_w_up', 'moe_w_down',
                 'ple_w_proj', 'ple_w_gate'):
        prm[name] = raw[name].astype(BF16)
    per = 2 * LANES // (D // LRU_BLOCKS)
    prm['lru_wa_bd'] = jax.vmap(lambda w: _block_diag_chunks(w, per))(raw['lru_w_a']).astype(BF16)
    prm['lru_wi_bd'] = jax.vmap(lambda w: _block_diag_chunks(w, per))(raw['lru_w_i']).astype(BF16)
    prm['lru_b_a'] = raw['lru_b_a'].reshape(raw['lru_b_a'].shape[0], -1)
    prm['lru_b_i'] = raw['lru_b_i'].reshape(raw['lru_b_i'].shape[0], -1)
    n_h = raw['ssm_a_log'].shape[1]
    di = n_h * SSM_P
    w = raw['ssm_w_in']
    conv_dim = raw['ssm_conv_w'].shape[2]
    prm['ssm_w_z'] = w[:, :, :di].astype(BF16)
    prm['ssm_w_xbc'] = w[:, :, di:di + conv_dim].astype(BF16)
    prm['ssm_w_dt'] = jax.vmap(lambda m: _pad_cols(m, LANES))(w[:, :, di + conv_dim:])
    prm['ssm_dt_bias'] = jax.vmap(_pad_lanes)(raw['ssm_dt_bias'])
    prm['ssm_a_log'] = jax.vmap(_pad_lanes)(raw['ssm_a_log'])
    prm['ssm_d_lanes'] = jnp.repeat(raw['ssm_d'], SSM_P, axis=1)[:, None, :]
    prm['ssm_expand'] = _head_expand(n_h, SSM_P).astype(BF16)
    w = raw['fox_w_in']
    for n, name in enumerate(('fox_w_q', 'fox_w_k', 'fox_w_v', 'fox_w_g')):
        prm[name] = w[:, :, n * D:(n + 1) * D].astype(BF16)
    prm['fox_w_f'] = jax.vmap(lambda m: _pad_cols(m, LANES))(w[:, :, 4 * D:])
    prm['fox_b_f'] = jax.vmap(_pad_lanes)(raw['fox_b_f'])
    n_ah = D // ATT_HD
    prm['fox_q_norm'] = jnp.tile(raw['fox_q_norm'], (1, n_ah))[:, None, :]
    prm['fox_k_norm'] = jnp.tile(raw['fox_k_norm'], (1, n_ah))[:, None, :]
    prm['fox_head_expand'] = _head_expand(n_ah, ATT_HD)
    prm['fox_head_sum'] = prm['fox_head_expand'].T.astype(BF16)
    hh = jnp.arange(LANES)[None, :, None]
    jj = jnp.arange(N_PIECES)[:, None, None]
    col = jnp.arange(D)[None, None, :]
    prm['fox_place'] = ((hh < n_ah) & (col == LANES * (hh // PAIR) + N_PIECES * (hh % PAIR) + jj)).astype(BF16)
    prm['moe_w_router'] = jax.vmap(lambda we, wg: _pad_cols(jnp.concatenate([we, wg], axis=1), LANES))(
        raw['moe_w_expert'], raw['moe_w_group'])
    prm['moe_b_router'] = jax.vmap(lambda be, bg: _pad_lanes(jnp.concatenate([be, bg])))(
        raw['moe_b_expert'], raw['moe_b_group'])
    return prm


def kernel(x_prompt, x_sample, state_lru_h, state_lru_conv, state_ssm_h, state_ssm_conv, cache_k, cache_v, cache_logf, page_table, p_prompt, p_sample, lru_w_in, lru_conv_w, lru_conv_b, lru_w_a, lru_b_a, lru_w_i, lru_b_i, lru_lambda, lru_w_out, ssm_w_in, ssm_conv_w, ssm_conv_b, ssm_dt_bias, ssm_a_log, ssm_d, ssm_norm, ssm_w_out, fox_w_in, fox_b_f, fox_q_norm, fox_k_norm, fox_w_out, moe_w_group, moe_b_group, moe_w_expert, moe_b_expert, moe_w_gate, moe_w_up, moe_w_down, ple_w_proj, ple_w_gate, norm_mix, norm_ffn, norm_ple, norm_final):
    prm = _prepare_params(dict(
        lru_w_in=lru_w_in, lru_conv_w=lru_conv_w, lru_conv_b=lru_conv_b, lru_w_a=lru_w_a, lru_b_a=lru_b_a,
        lru_w_i=lru_w_i, lru_b_i=lru_b_i, lru_lambda=lru_lambda, lru_w_out=lru_w_out, ssm_w_in=ssm_w_in,
        ssm_conv_w=ssm_conv_w, ssm_conv_b=ssm_conv_b, ssm_dt_bias=ssm_dt_bias, ssm_a_log=ssm_a_log, ssm_d=ssm_d,
        ssm_norm=ssm_norm, ssm_w_out=ssm_w_out, fox_w_in=fox_w_in, fox_b_f=fox_b_f, fox_q_norm=fox_q_norm,
        fox_k_norm=fox_k_norm, fox_w_out=fox_w_out, moe_w_group=moe_w_group, moe_b_group=moe_b_group,
        moe_w_expert=moe_w_expert, moe_b_expert=moe_b_expert, moe_w_gate=moe_w_gate, moe_w_up=moe_w_up,
        moe_w_down=moe_w_down, ple_w_proj=ple_w_proj, ple_w_gate=ple_w_gate, norm_mix=norm_mix,
        norm_ffn=norm_ffn, norm_ple=norm_ple, norm_final=norm_final))
    depth = norm_mix.shape[0]
    B, L, D = x_prompt.shape
    Bs = x_sample.shape[0]
    n_mix = 3
    att_h = D // ATT_HD
    n_phys, page = cache_k.shape[1], cache_k.shape[2]
    cache = (jnp.transpose(cache_k, (0, 1, 3, 4, 2)).reshape(-1, att_h, ATT_HD, page),
             jnp.transpose(cache_v, (0, 1, 3, 4, 2)).reshape(-1, att_h, ATT_HD, page),
             jnp.transpose(cache_logf, (0, 1, 3, 2)).reshape(-1, att_h, page))

    hp = x_prompt.reshape(B * L, D)
    hs = x_sample.reshape(Bs, D)
    outs = {k: [] for k in ('lru_h_p', 'lru_h_s', 'lru_c_p', 'lru_c_s', 'ssm_h_p', 'ssm_h_s', 'ssm_c_p', 'ssm_c_s',
                            'k_p', 'k_s', 'v_p', 'v_s', 'lf_p', 'lf_s')}
    yp = ys = None
    for i in range(depth):
        j = i // n_mix
        g = prm['norm_mix'][i]
        if i % n_mix == 0:
            hp, buf, hl = _lru_layer_prompt(hp, B, L, j, g, prm, _moe_route(i, prm, B * L))
            outs['lru_c_p'].append(buf)
            outs['lru_h_p'].append(hl)
            hs, buf, hl = _lru_layer_sample(hs, state_lru_conv[j], state_lru_h[j], j, g, prm)
            outs['lru_c_s'].append(buf)
            outs['lru_h_s'].append(hl)
        elif i % n_mix == 1:
            hp, buf, st = _ssd_layer_prompt(hp, B, L, j, g, prm, _moe_route(i, prm, B * L))
            outs['ssm_c_p'].append(buf)
            outs['ssm_h_p'].append(st)
            hs, buf, st = _ssd_layer_sample(hs, state_ssm_conv[j], state_ssm_h[j], j, g, prm)
            outs['ssm_c_s'].append(buf)
            outs['ssm_h_s'].append(st)
        else:
            hp, k, v, lf = _fox_layer_prompt(hp, B, L, j, g, prm, _moe_route(i, prm, B * L))
            outs['k_p'].append(k)
            outs['v_p'].append(v)
            outs['lf_p'].append(lf[:, :, :att_h])
            hs, k, v, lf = _fox_layer_sample(hs, cache, n_phys, page_table, j, g, prm)
            outs['k_s'].append(k.reshape(Bs, 1, att_h, ATT_HD))
            outs['v_s'].append(v.reshape(Bs, 1, att_h, ATT_HD))
            outs['lf_s'].append(lf[:, :, :att_h])
        last = i == depth - 1
        hp = _moe_and_ple(hp, p_prompt[i].reshape(B * L, -1), i, prm, last)
        hs = _moe_and_ple(hs, p_sample[i].reshape(Bs, -1), i, prm, last)
        if last:
            hp, yp = hp
            hs, ys = hs
    st = {k: jnp.stack(v) for k, v in outs.items()}
    return (yp.reshape(B, L, D), ys.reshape(Bs, 1, D), st['lru_h_p'], st['lru_h_s'], st['lru_c_p'], st['lru_c_s'],
            st['ssm_h_p'], st['ssm_h_s'], st['ssm_c_p'], st['ssm_c_s'], st['k_p'], st['k_s'], st['v_p'], st['v_s'],
            st['lf_p'], st['lf_s'])
```
